```python
import jax, jax.numpy as jnp
from jax import lax
import numpy as np

D_MODEL = 2048
BATCH = 4
SEQ = 4096
DEPTH = 2

GRID_W = 64
CTX_LEN = 256
RMS_EPS = 1e-6

N_Q_HEADS = 8
N_KV_HEADS = 2
HEAD_DIM = 128
WINDOW = 128
ATTN_BLOCK = 128
ROPE_THETA = 10000.0
ROPE_AXIS_DIM = HEAD_DIM // 2

GLA_HEADS = 4
GLA_DK = 64
GLA_DV = 128
GLA_RANK = 16
GLA_TAU = 16.0
GLA_CHUNK = 16

LRU_WIDTH = 512
LRU_BLOCKS = 8
LRU_BLOCK_W = LRU_WIDTH // LRU_BLOCKS
LRU_C = 8.0
CONV_W = 4
CONV_LEFT = 2

ATTN_Q = N_Q_HEADS * HEAD_DIM
ATTN_KV = N_KV_HEADS * HEAD_DIM
GLA_QK = GLA_HEADS * GLA_DK
GLA_V = GLA_HEADS * GLA_DV
D_MIX = ATTN_Q + GLA_V + LRU_WIDTH
IN_SPLITS = (ATTN_Q, ATTN_KV, ATTN_KV, GLA_QK, GLA_QK, GLA_V, GLA_V, 2 * GLA_RANK, LRU_WIDTH, LRU_WIDTH)
D_IN = 4128

N_GROUPS = 4
EXPERTS_PER_GROUP = 8
N_EXPERTS = N_GROUPS * EXPERTS_PER_GROUP
TOP_K = 2
D_FF_EXPERT = 1024
MOE_BLOCK = 256

kernel_name = 'hybrid_dit_swa_gla_rglru_hmoe'


def rms_norm(x, g):
    x32 = x.astype(jnp.float32)
    y = x32 * lax.rsqrt(jnp.mean(x32 * x32, axis=-1, keepdims=True) + RMS_EPS)
    return (y * g.astype(jnp.float32)).astype(x.dtype)


def modulate(h, shift, scale):
    return h * (1 + scale) + shift


def rev(a):
    return jnp.flip(a, axis=1)


def split_cols(z):
    cuts = np.cumsum(IN_SPLITS)[:-1].tolist()
    return jnp.split(z, cuts, axis=-1)


def axial_rope_angles(n_tokens):
    rows = n_tokens // GRID_W
    row = jnp.repeat(jnp.arange(rows, dtype=jnp.float32), GRID_W)
    col = jnp.tile(jnp.arange(GRID_W, dtype=jnp.float32), rows)
    n_freq = ROPE_AXIS_DIM // 2
    inv = ROPE_THETA ** (-jnp.arange(n_freq, dtype=jnp.float32) / n_freq)
    ang = jnp.concatenate([row[:, None] * inv, col[:, None] * inv], axis=-1)
    return jnp.cos(ang), jnp.sin(ang)


def apply_axial_rope(x, cos, sin):
    xp = x.astype(jnp.float32).reshape(x.shape[:-1] + (HEAD_DIM // 2, 2))
    x1, x2 = xp[..., 0], xp[..., 1]
    cs, sn = cos[None, :, None, :], sin[None, :, None, :]
    out = jnp.stack([x1 * cs - x2 * sn, x1 * sn + x2 * cs], axis=-1)
    return out.reshape(x.shape).astype(x.dtype)


def windowed_attention(q, k, v, k_ctx, v_ctx, sink):
    B, S, Hq, hd = q.shape
    G = Hq // N_KV_HEADS
    nb = S // ATTN_BLOCK
    C = k_ctx.shape[1]
    nk = 3 * ATTN_BLOCK
    scale = HEAD_DIM ** -0.5
    qb = q.reshape(B, nb, ATTN_BLOCK, N_KV_HEADS, G, hd)
    pad = ((0, 0), (ATTN_BLOCK, ATTN_BLOCK), (0, 0), (0, 0))
    kp = jnp.pad(k, pad).reshape(B, nb + 2, ATTN_BLOCK, N_KV_HEADS, hd)
    vp = jnp.pad(v, pad).reshape(B, nb + 2, ATTN_BLOCK, N_KV_HEADS, hd)
    k_band = jnp.concatenate([kp[:, :-2], kp[:, 1:-1], kp[:, 2:]], axis=2)
    v_band = jnp.concatenate([vp[:, :-2], vp[:, 1:-1], vp[:, 2:]], axis=2)
    s_loc = jnp.einsum('bnqhgd,bnkhd->bnhgqk', qb, k_band).astype(jnp.float32) * scale
    s_ctx = jnp.einsum('bnqhgd,bchd->bnhgqc', qb, k_ctx).astype(jnp.float32) * scale
    blk = jnp.arange(nb)[:, None, None]
    qpos = blk * ATTN_BLOCK + jnp.arange(ATTN_BLOCK)[None, :, None]
    kpos = (blk - 1) * ATTN_BLOCK + jnp.arange(nk)[None, None, :]
    valid = (jnp.abs(kpos - qpos) <= WINDOW) & (kpos >= 0) & (kpos < S)
    s_loc = jnp.where(valid[None, :, None, None], s_loc, -jnp.inf)
    sink_col = jnp.broadcast_to(sink.astype(jnp.float32).reshape(N_KV_HEADS, G)[None, None, :, :, None, None], s_loc.shape[:-1] + (1,))
    p = jax.nn.softmax(jnp.concatenate([s_loc, s_ctx, sink_col], axis=-1), axis=-1).astype(v.dtype)
    o = jnp.einsum('bnhgqk,bnkhd->bnqhgd', p[..., :nk], v_band) + jnp.einsum('bnhgqc,bchd->bnqhgd', p[..., nk:nk + C], v_ctx)
    return o.reshape(B, S, Hq * hd)


def context_attention(q, k, v, sink):
    B, C, Hq, hd = q.shape
    G = Hq // N_KV_HEADS
    qg = q.reshape(B, C, N_KV_HEADS, G, hd)
    s = jnp.einsum('bqhgd,bkhd->bhgqk', qg, k).astype(jnp.float32) * (HEAD_DIM ** -0.5)
    sink_col = jnp.broadcast_to(sink.astype(jnp.float32).reshape(N_KV_HEADS, G)[None, :, :, None, None], s.shape[:-1] + (1,))
    p = jax.nn.softmax(jnp.concatenate([s, sink_col], axis=-1), axis=-1)[..., :C].astype(v.dtype)
    o = jnp.einsum('bhgqk,bkhd->bqhgd', p, v)
    return o.reshape(B, C, Hq * hd)


def gla_qkv(gq, gk, gv):
    B, T, _ = gq.shape
    q = gq.reshape(B, T, GLA_HEADS, GLA_DK) * (GLA_DK ** -0.5)
    k = gk.reshape(B, T, GLA_HEADS, GLA_DK)
    v = gv.reshape(B, T, GLA_HEADS, GLA_DV)
    return q, k, v


def gla_log_decay(z_low, w2, b):
    z = (z_low @ w2 + b).astype(jnp.float32)
    return (jax.nn.log_sigmoid(z) / GLA_TAU).reshape(z.shape[:-1] + (GLA_HEADS, GLA_DK))


def gla_chunked(q, k, v, log_a, s0):
    B, T, H, dk = q.shape
    dv = v.shape[-1]
    L = GLA_CHUNK
    nc = T // L
    qc = q.astype(jnp.float32).reshape(B, nc, L, H, dk)
    kc = k.astype(jnp.float32).reshape(B, nc, L, H, dk)
    vc = v.astype(jnp.float32).reshape(B, nc, L, H, dv)
    b = jnp.cumsum(log_a.reshape(B, nc, L, H, dk), axis=2)
    b_last = b[:, :, -1]
    tri = jnp.tril(jnp.ones((L, L), dtype=bool))
    diff = b[:, :, :, None] - b[:, :, None, :]
    decay = jnp.exp(jnp.where(tri[None, None, :, :, None, None], diff, -jnp.inf))
    A = jnp.einsum('bntshd,bnthd,bnshd->bnhts', decay, qc, kc)
    o_intra = jnp.einsum('bnhts,bnshv->bnthv', A, vc)
    contrib = jnp.einsum('bnshd,bnshv->bnhdv', kc * jnp.exp(b_last[:, :, None] - b), vc)
    a_tot = jnp.exp(b_last)

    def step(s, inp):
        a, u = inp
        return a[..., None] * s + u, s

    s_final, s_before = lax.scan(step, s0, (jnp.moveaxis(a_tot, 1, 0), jnp.moveaxis(contrib, 1, 0)))
    s_before = jnp.moveaxis(s_before, 0, 1)
    o_inter = jnp.einsum('bnthd,bnhdv->bnthv', qc * jnp.exp(b), s_before)
    return (o_intra + o_inter).reshape(B, T, H, dv), s_final


def gla_output(o, r, g):
    y = rms_norm(o, g).reshape(o.shape[:2] + (GLA_V,))
    return (y * jax.nn.silu(r.astype(jnp.float32))).astype(r.dtype)


def short_conv(x, w, b):
    T = x.shape[1]
    xp = jnp.pad(x, ((0, 0), (CONV_LEFT, CONV_W - 1 - CONV_LEFT), (0, 0)))
    y = b
    for j in range(CONV_W):
        y = y + xp[:, j:j + T] * w[j]
    return y


def lru_gates(x, wa, ba, wx, bx, lam):
    B, T, _ = x.shape
    xb = x.reshape(B, T, LRU_BLOCKS, LRU_BLOCK_W)
    r = jax.nn.sigmoid((jnp.einsum('btnc,ncd->btnd', xb, wa).reshape(B, T, LRU_WIDTH) + ba).astype(jnp.float32))
    i = jax.nn.sigmoid((jnp.einsum('btnc,ncd->btnd', xb, wx).reshape(B, T, LRU_WIDTH) + bx).astype(jnp.float32))
    log_a = -LRU_C * jax.nn.softplus(-lam.astype(jnp.float32)) * r
    u = jnp.sqrt(-jnp.expm1(2.0 * log_a)) * (i * x.astype(jnp.float32))
    return log_a, u


def _affine_combine(e1, e2):
    a1, b1 = e1
    a2, b2 = e2
    return a1 * a2, a2 * b1 + b2


def linear_scan(log_a, u, h0):
    A, Bv = lax.associative_scan(_affine_combine, (jnp.exp(log_a), u), axis=1)
    h = Bv + A * h0[:, None, :]
    return h, h[:, -1]


def lru_output(h, gate):
    return (h * jax.nn.gelu(gate.astype(jnp.float32))).astype(gate.dtype)


def hybrid_mixer(u_l, u_c, cos, sin, w_in, sink, gla_w2, gla_b, gla_g, conv_w, conv_b, lru_wa, lru_ba, lru_wx, lru_bx, lru_lam, w_out, need_ctx_out):
    B, S, _ = u_l.shape
    C = u_c.shape[1]
    aq_l, ak_l, av_l, gq_l, gk_l, gv_l, gr_l, gz_l, lx_l, lg_l = split_cols(u_l @ w_in)
    aq_c, ak_c, av_c, gq_c, gk_c, gv_c, gr_c, gz_c, lx_c, lg_c = split_cols(u_c @ w_in)

    q_l = apply_axial_rope(aq_l.reshape(B, S, N_Q_HEADS, HEAD_DIM), cos, sin)
    k_l = apply_axial_rope(ak_l.reshape(B, S, N_KV_HEADS, HEAD_DIM), cos, sin)
    v_l = av_l.reshape(B, S, N_KV_HEADS, HEAD_DIM)
    k_c = ak_c.reshape(B, C, N_KV_HEADS, HEAD_DIM)
    v_c = av_c.reshape(B, C, N_KV_HEADS, HEAD_DIM)
    attn_l = windowed_attention(q_l, k_l, v_l, k_c, v_c, sink)

    qg_l, kg_l, vg_l = gla_qkv(gq_l, gk_l, gv_l)
    qg_c, kg_c, vg_c = gla_qkv(gq_c, gk_c, gv_c)
    la_l_f = gla_log_decay(gz_l[..., :GLA_RANK], gla_w2[0], gla_b[0])
    la_l_b = gla_log_decay(gz_l[..., GLA_RANK:], gla_w2[1], gla_b[1])
    la_c_f = gla_log_decay(gz_c[..., :GLA_RANK], gla_w2[0], gla_b[0])
    la_c_b = gla_log_decay(gz_c[..., GLA_RANK:], gla_w2[1], gla_b[1])
    s_zero = jnp.zeros((B, GLA_HEADS, GLA_DK, GLA_DV), jnp.float32)
    oc_f, st_f = gla_chunked(qg_c, kg_c, vg_c, la_c_f, s_zero)
    oc_b, st_b = gla_chunked(rev(qg_c), rev(kg_c), rev(vg_c), rev(la_c_b), s_zero)
    ol_f, _ = gla_chunked(qg_l, kg_l, vg_l, la_l_f, st_f)
    ol_b, _ = gla_chunked(rev(qg_l), rev(kg_l), rev(vg_l), rev(la_l_b), st_b)
    gla_l = gla_output(ol_f + rev(ol_b), gr_l, gla_g)

    xc_l = short_conv(lx_l, conv_w, conv_b)
    xc_c = short_conv(lx_c, conv_w, conv_b)
    h_zero = jnp.zeros((B, LRU_WIDTH), jnp.float32)
    lcf, ucf = lru_gates(xc_c, lru_wa[0], lru_ba[0], lru_wx[0], lru_bx[0], lru_lam[0])
    lcb, ucb = lru_gates(xc_c, lru_wa[1], lru_ba[1], lru_wx[1], lru_bx[1], lru_lam[1])
    llf, ulf = lru_gates(xc_l, lru_wa[0], lru_ba[0], lru_wx[0], lru_bx[0], lru_lam[0])
    llb, ulb = lru_gates(xc_l, lru_wa[1], lru_ba[1], lru_wx[1], lru_bx[1], lru_lam[1])
    hc_f, hs_f = linear_scan(lcf, ucf, h_zero)
    hc_b, hs_b = linear_scan(rev(lcb), rev(ucb), h_zero)
    hl_f, _ = linear_scan(llf, ulf, hs_f)
    hl_b, _ = linear_scan(rev(llb), rev(ulb), hs_b)
    lru_l = lru_output(hl_f + rev(hl_b), lg_l)

    y_l = jnp.concatenate([attn_l, gla_l, lru_l], axis=-1) @ w_out
    if not need_ctx_out:
        return y_l, None
    attn_c = context_attention(aq_c.reshape(B, C, N_Q_HEADS, HEAD_DIM), k_c, v_c, sink)
    gla_c = gla_output(oc_f + rev(oc_b), gr_c, gla_g)
    lru_c = lru_output(hc_f + rev(hc_b), lg_c)
    y_c = jnp.concatenate([attn_c, gla_c, lru_c], axis=-1) @ w_out
    return y_l, y_c


def expert_dispatch(h, experts, gates, w1, w3, w2):
    N, D = h.shape
    M = N * TOP_K
    flat_e = experts.reshape(M).astype(jnp.int32)
    flat_tok = jnp.arange(M, dtype=jnp.int32) // TOP_K
    flat_w = gates.reshape(M)
    order = jnp.argsort(flat_e)
    e_sorted = flat_e[order]
    tok_sorted = flat_tok[order]
    w_sorted = flat_w[order]
    counts = jnp.bincount(flat_e, length=N_EXPERTS).astype(jnp.int32)
    padded = (counts + MOE_BLOCK - 1) // MOE_BLOCK * MOE_BLOCK
    pad_end = jnp.cumsum(padded)
    pad_start = pad_end - padded
    start = jnp.cumsum(counts) - counts
    dest = pad_start[e_sorted] + (jnp.arange(M, dtype=jnp.int32) - start[e_sorted])
    n_blocks = -(-(M + N_EXPERTS * (MOE_BLOCK - 1)) // MOE_BLOCK)
    P = n_blocks * MOE_BLOCK
    src = jnp.full((P,), N, dtype=jnp.int32).at[dest].set(tok_sorted)
    w_pad = jnp.zeros((P,), jnp.float32).at[dest].set(w_sorted)
    h_ext = jnp.concatenate([h, jnp.zeros((1, D), h.dtype)], axis=0)
    xs = h_ext[src].reshape(n_blocks, MOE_BLOCK, D)
    blk_e = jnp.minimum(jnp.searchsorted(pad_end, jnp.arange(n_blocks, dtype=jnp.int32) * MOE_BLOCK, side='right'), N_EXPERTS - 1)

    def expert_block(args):
        xb, e = args
        return (jax.nn.silu(xb @ w1[e]) * (xb @ w3[e])) @ w2[e]

    ys = lax.map(expert_block, (xs, blk_e)).reshape(P, D)
    y = jax.ops.segment_sum(ys * w_pad[:, None].astype(ys.dtype), src, num_segments=N + 1)
    return y[:N]


def hierarchical_moe(h, wg, bg, we, be, w1, w3, w2):
    N, _ = h.shape
    pg = jax.nn.softmax((h @ wg).astype(jnp.float32) + bg.astype(jnp.float32), axis=-1)
    p_grp, grp = lax.top_k(pg, 1)
    le = ((h @ we).astype(jnp.float32) + be.astype(jnp.float32)).reshape(N, N_GROUPS, EXPERTS_PER_GROUP)
    le_sel = jnp.take_along_axis(le, grp[:, :, None], axis=1)[:, 0]
    w_top, e_top = lax.top_k(jax.nn.softmax(le_sel, axis=-1), TOP_K)
    w_top = w_top / jnp.sum(w_top, axis=-1, keepdims=True)
    gates = p_grp * w_top
    experts = grp * EXPERTS_PER_GROUP + e_top
    return expert_dispatch(h, experts, gates, w1, w3, w2)


def setup_inputs(seed: int = 0) -> dict:
    key = jax.random.key(seed)
    ks = jax.random.split(key, 32)
    f32 = jnp.float32
    D, L = D_MODEL, DEPTH

    def nrm(k, shape, scale):
        return jax.random.normal(k, shape, f32) * scale

    lam_u = jax.random.uniform(ks[19], (L, 2, LRU_WIDTH), f32, 0.9, 0.999)
    lam_s = lam_u ** (1.0 / LRU_C)
    return {
        'x': nrm(ks[0], (BATCH, SEQ, D), 1.0),
        'c': nrm(ks[1], (BATCH, D), 1.0),
        'ctx': nrm(ks[2], (BATCH, CTX_LEN, D), 1.0),
        'c_ctx': nrm(ks[3], (D,), 1.0),
        'ada_w': nrm(ks[4], (L, D, 6 * D), 0.5 * D ** -0.5),
        'ada_b': nrm(ks[5], (L, 6 * D), 0.02),
        'norm_mix_g': 1.0 + nrm(ks[6], (L, D), 0.02),
        'norm_ffn_g': 1.0 + nrm(ks[7], (L, D), 0.02),
        'w_in': nrm(ks[8], (L, D, D_IN), D ** -0.5),
        'attn_sink': nrm(ks[9], (L, N_Q_HEADS), 0.5),
        'gla_gate_w2': nrm(ks[10], (L, 2, GLA_RANK, GLA_QK), GLA_RANK ** -0.5),
        'gla_gate_b': nrm(ks[11], (L, 2, GLA_QK), 0.1),
        'gla_norm_g': 1.0 + nrm(ks[12], (L, GLA_DV), 0.02),
        'lru_conv_w': nrm(ks[13], (L, CONV_W, LRU_WIDTH), CONV_W ** -0.5),
        'lru_conv_b': nrm(ks[14], (L, LRU_WIDTH), 0.02),
        'lru_wa': nrm(ks[15], (L, 2, LRU_BLOCKS, LRU_BLOCK_W, LRU_BLOCK_W), LRU_BLOCK_W ** -0.5),
        'lru_ba': nrm(ks[16], (L, 2, LRU_WIDTH), 0.02),
        'lru_wx': nrm(ks[17], (L, 2, LRU_BLOCKS, LRU_BLOCK_W, LRU_BLOCK_W), LRU_BLOCK_W ** -0.5),
        'lru_bx': nrm(ks[18], (L, 2, LRU_WIDTH), 0.02),
        'lru_lambda': jnp.log(lam_s) - jnp.log1p(-lam_s),
        'w_out': nrm(ks[20], (L, D_MIX, D), D_MIX ** -0.5),
        'router_g_w': nrm(ks[21], (L, D, N_GROUPS), D ** -0.5),
        'router_g_b': nrm(ks[22], (L, N_GROUPS), 0.01),
        'router_e_w': nrm(ks[23], (L, D, N_EXPERTS), D ** -0.5),
        'router_e_b': nrm(ks[24], (L, N_EXPERTS), 0.01),
        'moe_w1': nrm(ks[25], (L, N_EXPERTS, D, D_FF_EXPERT), D ** -0.5),
        'moe_w3': nrm(ks[26], (L, N_EXPERTS, D, D_FF_EXPERT), D ** -0.5),
        'moe_w2': nrm(ks[27], (L, N_EXPERTS, D_FF_EXPERT, D), D_FF_EXPERT ** -0.5),
        'final_norm_g': 1.0 + nrm(ks[28], (D,), 0.02),
    }


def reference(x, c, ctx, c_ctx, ada_w, ada_b, norm_mix_g, norm_ffn_g, w_in, attn_sink, gla_gate_w2, gla_gate_b, gla_norm_g, lru_conv_w, lru_conv_b, lru_wa, lru_ba, lru_wx, lru_bx, lru_lambda, w_out, router_g_w, router_g_b, router_e_w, router_e_b, moe_w1, moe_w3, moe_w2, final_norm_g):
    B, S, D = x.shape
    C = ctx.shape[1]
    cos, sin = axial_rope_angles(S)
    cond_l = jax.nn.silu(c)
    cond_c = jax.nn.silu(c_ctx)
    h_lat, h_ctx = x, ctx
    for i in range(DEPTH):
        last = i == DEPTH - 1
        mods_l = [m[:, None, :] for m in jnp.split(cond_l @ ada_w[i] + ada_b[i], 6, axis=-1)]
        mods_c = jnp.split(cond_c @ ada_w[i] + ada_b[i], 6, axis=-1)
        sh1_l, sc1_l, g1_l, sh2_l, sc2_l, g2_l = mods_l
        sh1_c, sc1_c, g1_c, sh2_c, sc2_c, g2_c = mods_c
        u_l = modulate(rms_norm(h_lat, norm_mix_g[i]), sh1_l, sc1_l)
        u_c = modulate(rms_norm(h_ctx, norm_mix_g[i]), sh1_c, sc1_c)
        y_l, y_c = hybrid_mixer(u_l, u_c, cos, sin, w_in[i], attn_sink[i], gla_gate_w2[i], gla_gate_b[i], gla_norm_g[i], lru_conv_w[i], lru_conv_b[i], lru_wa[i], lru_ba[i], lru_wx[i], lru_bx[i], lru_lambda[i], w_out[i], not last)
        h_lat = h_lat + g1_l * y_l
        v_l = modulate(rms_norm(h_lat, norm_ffn_g[i]), sh2_l, sc2_l).reshape(B * S, D)
        moe_args = (router_g_w[i], router_g_b[i], router_e_w[i], router_e_b[i], moe_w1[i], moe_w3[i], moe_w2[i])
        if last:
            f_l = hierarchical_moe(v_l, *moe_args)
        else:
            h_ctx = h_ctx + g1_c * y_c
            v_c = modulate(rms_norm(h_ctx, norm_ffn_g[i]), sh2_c, sc2_c).reshape(B * C, D)
            f = hierarchical_moe(jnp.concatenate([v_l, v_c], axis=0), *moe_args)
            f_l = f[:B * S]
            h_ctx = h_ctx + g2_c * f[B * S:].reshape(B, C, D)
        h_lat = h_lat + g2_l * f_l.reshape(B, S, D)
    return rms_norm(h_lat, final_norm_g)
```

```python
import functools

import numpy as np
import jax
import jax.numpy as jnp
from jax import lax
from jax.experimental import pallas as pl
from jax.experimental.pallas import tpu as pltpu

F32 = jnp.float32
BF16 = jnp.bfloat16
I32 = jnp.int32

GRID_W = 64
RMS_EPS = 1e-6
N_Q_HEADS = 8
N_KV_HEADS = 2
HEAD_DIM = 128
WINDOW = 128
ATTN_BLOCK = 128
ROPE_THETA = 10000.0
GLA_HEADS = 4
GLA_DK = 64
GLA_DV = 128
GLA_RANK = 16
GLA_TAU = 16.0
LRU_WIDTH = 512
LRU_BLOCKS = 8
LRU_C = 8.0
CONV_W = 4
N_GROUPS = 4
EXPERTS_PER_GROUP = 8
N_EXPERTS = 32
TOP_K = 2
ATTN_Q = N_Q_HEADS * HEAD_DIM
ATTN_KV = N_KV_HEADS * HEAD_DIM
GLA_QK = GLA_HEADS * GLA_DK
GLA_V = GLA_HEADS * GLA_DV
D_MIX = ATTN_Q + GLA_V + LRU_WIDTH

LANES = 128
SUBLANES = 8
VMEM_LIMIT_BYTES = 56 * 1024 * 1024

QKV_W = ATTN_Q + 2 * ATTN_KV
GLA_W = 2 * GLA_QK + 2 * GLA_V
GZ_W = LANES
LRU_W = 2 * LRU_WIDTH
COL_GLA = QKV_W
COL_GZ = COL_GLA + GLA_W
COL_LRU = COL_GZ + GZ_W
W_IN_PACKED = COL_LRU + LRU_W

GLA_CHUNK = 128
GLA_LEVELS = 7
LRU_CHUNK = 128
ROW_BLOCK = 256


def _cparams(*sem):
    return pltpu.CompilerParams(dimension_semantics=sem, vmem_limit_bytes=VMEM_LIMIT_BYTES)


def _tile(n, pref):
    t = min(n, pref)
    while n % t:
        t -= SUBLANES
    return t


def _softplus(x):
    return jnp.maximum(x, 0.0) + jnp.log1p(jnp.exp(-jnp.abs(x)))


def _pack_bf16_pairs(x):
    n = x.shape[1] // 2
    lo = lax.bitcast_convert_type(x[:, :n].astype(BF16).astype(F32), I32)
    hi = lax.bitcast_convert_type(x[:, n:].astype(BF16).astype(F32), I32)
    return hi | lax.shift_right_logical(lo, jnp.full(lo.shape, 16, I32))


def _unpack_bf16_pairs(p):
    lo = lax.bitcast_convert_type(lax.shift_left(p, jnp.full(p.shape, 16, I32)), F32)
    hi = lax.bitcast_convert_type(p & jnp.int32(-65536), F32)
    return jnp.concatenate([lo, hi], axis=1)


def _ada_kernel(c_ref, w_ref, b_ref, o_ref):
    c = c_ref[...]
    s = c * jax.nn.sigmoid(c)
    o_ref[0] = jnp.dot(s.astype(BF16), w_ref[0].astype(BF16), preferred_element_type=F32) + b_ref[0]


def _ada_mods(cond, ada_w, ada_b):
    L, D, D6 = ada_w.shape
    tn = _tile(D6, 1536)
    while tn % LANES:
        tn -= SUBLANES
    return pl.pallas_call(
        _ada_kernel,
        grid=(L, D6 // tn),
        in_specs=[
            pl.BlockSpec((SUBLANES, D), lambda l, j: (0, 0)),
            pl.BlockSpec((1, D, tn), lambda l, j: (l, 0, j)),
            pl.BlockSpec((1, 1, tn), lambda l, j: (l, 0, j)),
        ],
        out_specs=pl.BlockSpec((1, SUBLANES, tn), lambda l, j: (l, 0, j)),
        out_shape=jax.ShapeDtypeStruct((L, SUBLANES, D6), F32),
        compiler_params=_cparams("parallel", "parallel"),
        name="ada_mods",
    )(cond, ada_w, ada_b.reshape(L, 1, D6))


def _in_kernel(x_ref, a_ref, s_ref, cos_ref, se_ref, so_ref, w_ref, qkv_ref, gla_ref, gz_ref, lru_ref):
    x = x_ref[...]
    ms = jnp.mean(x * x, axis=-1, keepdims=True)
    u = (x * lax.rsqrt(ms + RMS_EPS)) * a_ref[0] + s_ref[0]
    ub = u.astype(BF16)
    cos, se, so = cos_ref[...], se_ref[...], so_ref[...]
    n_rot = N_Q_HEADS + N_KV_HEADS
    for j in range(n_rot):
        zh = jnp.dot(ub, w_ref[:, j * HEAD_DIM:(j + 1) * HEAD_DIM], preferred_element_type=F32)
        rot = zh * cos + pltpu.roll(zh, HEAD_DIM - 1, 1) * se + pltpu.roll(zh, 1, 1) * so
        qkv_ref[:, j * HEAD_DIM:(j + 1) * HEAD_DIM] = rot.astype(BF16)
    c0 = n_rot * HEAD_DIM
    qkv_ref[:, c0:QKV_W] = jnp.dot(ub, w_ref[:, c0:QKV_W], preferred_element_type=F32).astype(BF16)
    gla_ref[...] = jnp.dot(ub, w_ref[:, COL_GLA:COL_GZ], preferred_element_type=F32)
    gz_ref[...] = jnp.dot(ub, w_ref[:, COL_GZ:COL_LRU], preferred_element_type=F32)
    lru_ref[...] = jnp.dot(ub, w_ref[:, COL_LRU:W_IN_PACKED], preferred_element_type=F32)


def _in_proj(h, a, s, cos, se, so, w, seq_len):
    rows, D = h.shape
    tm = _tile(seq_len, 512)
    per = seq_len // tm
    row = lambda i: (i, 0)
    bat = lambda i: (i // per, 0, 0)
    tab = lambda i: (i % per, 0)
    return pl.pallas_call(
        _in_kernel,
        grid=(rows // tm,),
        in_specs=[
            pl.BlockSpec((tm, D), row),
            pl.BlockSpec((1, 1, D), bat),
            pl.BlockSpec((1, 1, D), bat),
            pl.BlockSpec((tm, HEAD_DIM), tab),
            pl.BlockSpec((tm, HEAD_DIM), tab),
            pl.BlockSpec((tm, HEAD_DIM), tab),
            pl.BlockSpec((D, W_IN_PACKED), lambda i: (0, 0), pipeline_mode=pl.Buffered(1)),
        ],
        out_specs=[
            pl.BlockSpec((tm, QKV_W), row),
            pl.BlockSpec((tm, GLA_W), row),
            pl.BlockSpec((tm, GZ_W), row),
            pl.BlockSpec((tm, LRU_W), row),
        ],
        out_shape=[
            jax.ShapeDtypeStruct((rows, QKV_W), BF16),
            jax.ShapeDtypeStruct((rows, GLA_W), F32),
            jax.ShapeDtypeStruct((rows, GZ_W), F32),
            jax.ShapeDtypeStruct((rows, LRU_W), F32),
        ],
        compiler_params=_cparams("parallel"),
        name="in_proj",
    )(h, a, s, cos, se, so, w)


def _attn_heads(q, k_loc, v_loc, valid, k_ctx, v_ctx, sink_ref, o_ref):
    scale = HEAD_DIM ** -0.5
    group = N_Q_HEADS // N_KV_HEADS
    nt = (((1,), (1,)), ((), ()))
    for head in range(N_Q_HEADS):
        hk = head // group
        qh = q[:, head * HEAD_DIM:(head + 1) * HEAD_DIM]
        sink = sink_ref[head:head + 1, 0:1]
        s_ctx = lax.dot_general(qh, k_ctx[hk], nt, preferred_element_type=F32) * scale
        m = jnp.maximum(jnp.max(s_ctx, axis=-1, keepdims=True), sink)
        if k_loc is not None:
            s_loc = lax.dot_general(qh, k_loc[hk], nt, preferred_element_type=F32) * scale
            s_loc = jnp.where(valid, s_loc, -jnp.inf)
            m = jnp.maximum(m, jnp.max(s_loc, axis=-1, keepdims=True))
        p_ctx = jnp.exp(s_ctx - m)
        den = jnp.sum(p_ctx, axis=-1, keepdims=True) + jnp.exp(sink - m)
        o = jnp.dot(p_ctx.astype(BF16), v_ctx[hk], preferred_element_type=F32)
        if k_loc is not None:
            p_loc = jnp.exp(s_loc - m)
            den = den + jnp.sum(p_loc, axis=-1, keepdims=True)
            o = o + jnp.dot(p_loc.astype(BF16), v_loc[hk], preferred_element_type=F32)
        o_ref[:, head * HEAD_DIM:(head + 1) * HEAD_DIM] = (o / den).astype(o_ref.dtype)


def _split_kv(kv):
    ks = [kv[:, h * HEAD_DIM:(h + 1) * HEAD_DIM] for h in range(N_KV_HEADS)]
    vs = [kv[:, ATTN_KV + h * HEAD_DIM:ATTN_KV + (h + 1) * HEAD_DIM] for h in range(N_KV_HEADS)]
    return ks, vs


def _attn_lat_kernel(q_ref, kvp_ref, kvc_ref, kvn_ref, kvx_ref, sink_ref, o_ref, *, nb):
    n = pl.program_id(1)
    kp, vp = _split_kv(kvp_ref[...])
    kc, vc = _split_kv(kvc_ref[...])
    kn, vn = _split_kv(kvn_ref[...])
    k_ctx, v_ctx = _split_kv(kvx_ref[...])
    k_loc = [jnp.concatenate([kp[h], kc[h], kn[h]], axis=0) for h in range(N_KV_HEADS)]
    v_loc = [jnp.concatenate([vp[h], vc[h], vn[h]], axis=0) for h in range(N_KV_HEADS)]
    i = lax.broadcasted_iota(I32, (ATTN_BLOCK, 3 * ATTN_BLOCK), 0)
    j = lax.broadcasted_iota(I32, (ATTN_BLOCK, 3 * ATTN_BLOCK), 1)
    valid = (j >= i) & (j <= i + 2 * WINDOW)
    valid = valid & ((j >= ATTN_BLOCK) | (n > 0)) & ((j < 2 * ATTN_BLOCK) | (n < nb - 1))
    _attn_heads(q_ref[...], k_loc, v_loc, valid, k_ctx, v_ctx, sink_ref, o_ref)


def _attn_ctx_kernel(q_ref, kvx_ref, sink_ref, o_ref):
    k_ctx, v_ctx = _split_kv(kvx_ref[...])
    _attn_heads(q_ref[...], None, None, None, k_ctx, v_ctx, sink_ref, o_ref)


def _attn_latent(qkv_l, qkv_c, sink_b, B, S, C):
    nb = S // ATTN_BLOCK
    kvw = 2 * ATTN_KV
    kvcol = ATTN_Q // kvw
    return pl.pallas_call(
        functools.partial(_attn_lat_kernel, nb=nb),
        grid=(B, nb),
        in_specs=[
            pl.BlockSpec((ATTN_BLOCK, ATTN_Q), lambda b, n: (b * nb + n, 0)),
            pl.BlockSpec((ATTN_BLOCK, kvw), lambda b, n: (b * nb + jnp.maximum(n - 1, 0), kvcol)),
            pl.BlockSpec((ATTN_BLOCK, kvw), lambda b, n: (b * nb + n, kvcol)),
            pl.BlockSpec((ATTN_BLOCK, kvw), lambda b, n: (b * nb + jnp.minimum(n + 1, nb - 1), kvcol)),
            pl.BlockSpec((C, kvw), lambda b, n: (b, kvcol)),
            pl.BlockSpec((SUBLANES, LANES), lambda b, n: (0, 0)),
        ],
        out_specs=pl.BlockSpec((ATTN_BLOCK, ATTN_Q), lambda b, n: (b * nb + n, 0)),
        out_shape=jax.ShapeDtypeStruct((B * S, ATTN_Q), BF16),
        compiler_params=_cparams("parallel", "parallel"),
        name="attn_latent",
    )(qkv_l, qkv_l, qkv_l, qkv_l, qkv_c, sink_b)


def _attn_context(qkv_c, sink_b, B, C):
    nb = C // ATTN_BLOCK
    kvw = 2 * ATTN_KV
    kvcol = ATTN_Q // kvw
    return pl.pallas_call(
        _attn_ctx_kernel,
        grid=(B, nb),
        in_specs=[
            pl.BlockSpec((ATTN_BLOCK, ATTN_Q), lambda b, n: (b * nb + n, 0)),
            pl.BlockSpec((C, kvw), lambda b, n: (b, kvcol)),
            pl.BlockSpec((SUBLANES, LANES), lambda b, n: (0, 0)),
        ],
        out_specs=pl.BlockSpec((ATTN_BLOCK, ATTN_Q), lambda b, n: (b * nb + n, 0)),
        out_shape=jax.ShapeDtypeStruct((B * C, ATTN_Q), BF16),
        compiler_params=_cparams("parallel", "parallel"),
        name="attn_context",
    )(qkv_c, qkv_c, sink_b)


def _gla_constants():
    Lc = GLA_CHUNK
    e = np.zeros((GLA_LEVELS + 2, Lc, Lc), np.float32)
    msk = np.zeros((GLA_LEVELS + 1, Lc, Lc), np.float32)
    t = np.arange(Lc)
    for l in range(GLA_LEVELS):
        m = 1 << l
        blk = t // (2 * m)
        upper = (t % (2 * m)) >= m
        bnd = blk * 2 * m + m
        r = t[None, :]
        eq = upper[:, None] & (r >= bnd[:, None]) & (r <= t[:, None])
        ek = (~upper)[:, None] & (r > t[:, None]) & (r <= bnd[:, None] - 1)
        e[l] = (eq | ek).astype(np.float32)
        msk[l] = (upper[:, None] & (~upper)[None, :] & (blk[:, None] == blk[None, :])).astype(np.float32)
    e[GLA_LEVELS] = (t[None, :] <= t[:, None]).astype(np.float32)
    e[GLA_LEVELS + 1] = (t[None, :] > t[:, None]).astype(np.float32)
    msk[GLA_LEVELS] = np.eye(Lc, dtype=np.float32)
    e2 = np.stack([e, e[:, ::-1, ::-1]]).reshape(2, (GLA_LEVELS + 2) * Lc, Lc)
    m2 = np.stack([msk, msk[:, ::-1, ::-1]])
    return e2, m2


def _gla_kernel(x_ref, gz_ref, w2_ref, bias_ref, e_ref, m_ref, s0_ref, o_ref, sfin_ref, s_scr, *, nch):
    i = pl.program_id(2)
    Lc = GLA_CHUNK

    @pl.when(i == 0)
    def _():
        s_scr[...] = s0_ref[0, 0]

    z = jnp.dot(gz_ref[...].astype(BF16), w2_ref[0], preferred_element_type=F32) + bias_ref[0]
    la = (jnp.minimum(z, 0.0) - jnp.log1p(jnp.exp(-jnp.abs(z)))) * (1.0 / GLA_TAU)
    la_hi = la.astype(BF16)
    la_lo = (la - la_hi.astype(F32)).astype(BF16)
    la2 = jnp.concatenate([la_hi, la_lo], axis=1)
    ex = jnp.dot(e_ref[0], la2, preferred_element_type=F32)
    decay = jnp.exp(ex[:, :GLA_QK] + ex[:, GLA_QK:])
    tot = lax.dot_general(la2, jnp.ones((Lc, LANES), BF16), (((0,), (0,)), ((), ())), preferred_element_type=F32)
    a_tot = jnp.exp(tot[:GLA_QK] + tot[GLA_QK:])

    q = x_ref[:, 0:GLA_QK] * (GLA_DK ** -0.5)
    k = x_ref[:, GLA_QK:2 * GLA_QK]
    v = x_ref[:, 2 * GLA_QK:2 * GLA_QK + GLA_V].astype(BF16)
    lane_head = lax.broadcasted_iota(I32, (Lc, GLA_QK), 1) >> 6
    row_head = lax.broadcasted_iota(I32, (GLA_QK, GLA_DV), 0) >> 6
    nt = (((1,), (1,)), ((), ()))

    att = [jnp.zeros((Lc, Lc), F32) for _ in range(GLA_HEADS)]
    for l in range(GLA_LEVELS + 1):
        if l < GLA_LEVELS:
            dl = decay[l * Lc:(l + 1) * Lc]
            ql = (q * dl).astype(BF16)
            kl = (k * dl).astype(BF16)
        else:
            ql = q.astype(BF16)
            kl = k.astype(BF16)
        ml = m_ref[0, l]
        for h in range(GLA_HEADS):
            kh = jnp.where(lane_head == h, kl, jnp.zeros_like(kl))
            att[h] = att[h] + ml * lax.dot_general(ql, kh, nt, preferred_element_type=F32)

    s_old = s_scr[...]
    s_b = s_old.astype(BF16)
    q_in = (q * decay[GLA_LEVELS * Lc:(GLA_LEVELS + 1) * Lc]).astype(BF16)
    k_out = (k * decay[(GLA_LEVELS + 1) * Lc:(GLA_LEVELS + 2) * Lc]).astype(BF16)
    s_new = a_tot * s_old
    for h in range(GLA_HEADS):
        vh = v[:, h * GLA_DV:(h + 1) * GLA_DV]
        qh = jnp.where(lane_head == h, q_in, jnp.zeros_like(q_in))
        o_h = jnp.dot(att[h].astype(BF16), vh, preferred_element_type=F32)
        o_h = o_h + jnp.dot(qh, s_b, preferred_element_type=F32)
        o_ref[0, :, h * GLA_DV:(h + 1) * GLA_DV] = o_h
        c_h = lax.dot_general(k_out, vh, (((0,), (0,)), ((), ())), preferred_element_type=F32)
        s_new = s_new + jnp.where(row_head == h, c_h, 0.0)
    s_scr[...] = s_new

    @pl.when(i == nch - 1)
    def _():
        sfin_ref[0, 0] = s_new


def _gla(gla_arr, gz_arr, w2p, bias, e2, m2, s0, B, T):
    nch = T // GLA_CHUNK
    blk = lambda b, d, i: b * nch + jnp.where(d == 0, i, nch - 1 - i)
    st = lambda b, d, i: (b, d, 0, 0)
    return pl.pallas_call(
        functools.partial(_gla_kernel, nch=nch),
        grid=(B, 2, nch),
        in_specs=[
            pl.BlockSpec((GLA_CHUNK, 2 * GLA_QK + GLA_V), lambda b, d, i: (blk(b, d, i), 0)),
            pl.BlockSpec((GLA_CHUNK, GZ_W), lambda b, d, i: (blk(b, d, i), 0)),
            pl.BlockSpec((1, GZ_W, GLA_QK), lambda b, d, i: (d, 0, 0)),
            pl.BlockSpec((1, 1, GLA_QK), lambda b, d, i: (d, 0, 0)),
            pl.BlockSpec((1, (GLA_LEVELS + 2) * GLA_CHUNK, GLA_CHUNK), lambda b, d, i: (d, 0, 0)),
            pl.BlockSpec((1, GLA_LEVELS + 1, GLA_CHUNK, GLA_CHUNK), lambda b, d, i: (d, 0, 0, 0)),
            pl.BlockSpec((1, 1, GLA_QK, GLA_DV), st),
        ],
        out_specs=[
            pl.BlockSpec((1, GLA_CHUNK, GLA_V), lambda b, d, i: (d, blk(b, d, i), 0)),
            pl.BlockSpec((1, 1, GLA_QK, GLA_DV), st),
        ],
        out_shape=[
            jax.ShapeDtypeStruct((2, B * T, GLA_V), F32),
            jax.ShapeDtypeStruct((B, 2, GLA_QK, GLA_DV), F32),
        ],
        scratch_shapes=[pltpu.VMEM((GLA_QK, GLA_DV), F32)],
        compiler_params=_cparams("parallel", "parallel", "arbitrary"),
        name="gla_scan",
    )(gla_arr, gz_arr, w2p, bias, e2, m2, s0)


def _lru_kernel(x_ref, pv_ref, nx_ref, cw_ref, cb_ref, wg_ref, bg_ref, lam_ref, h0_ref, o_ref, hfin_ref, carry, *, nblk):
    d = pl.program_id(1)
    i = pl.program_id(2)
    T = LRU_CHUNK
    W = LRU_WIDTH

    @pl.when(i == 0)
    def _():
        carry[...] = jnp.broadcast_to(h0_ref[0, 0], (SUBLANES, W))

    li = jnp.where(d == 0, i, nblk - 1 - i)
    pv = jnp.where(li > 0, pv_ref[...], 0.0)
    nx = jnp.where(li < nblk - 1, nx_ref[...], 0.0)
    xe = jnp.concatenate([pv, x_ref[...], nx], axis=0)
    n_ext = T + 2 * SUBLANES
    win = lambda off: pltpu.roll(xe, n_ext - off, 0)[0:T]
    cw = cw_ref[...]
    xc = cb_ref[...] + win(6) * cw[0:1] + win(7) * cw[1:2] + xe[SUBLANES:SUBLANES + T] * cw[2:3] + win(9) * cw[3:4]

    g = jnp.dot(xc.astype(BF16), wg_ref[0], preferred_element_type=F32) + bg_ref[0]
    r = jax.nn.sigmoid(g[:, :W])
    gi = jax.nn.sigmoid(g[:, W:])
    log_a = (-LRU_C * _softplus(-lam_ref[0])) * r
    a = jnp.exp(log_a)
    u = jnp.sqrt(-jnp.tanh(log_a) * (a * a + 1.0)) * (gi * xc)
    row = lax.broadcasted_iota(I32, (T, W), 0)

    @pl.when(d == 0)
    def _():
        aa, uu = a, u
        sh = 1
        while sh < T:
            ok = row >= sh
            uu = uu + aa * jnp.where(ok, pltpu.roll(uu, sh, 0), 0.0)
            aa = aa * jnp.where(ok, pltpu.roll(aa, sh, 0), 1.0)
            sh *= 2
        h = uu + aa * carry[0:1]
        o_ref[0] = h
        carry[...] = jnp.broadcast_to(h[T - 1:T], (SUBLANES, W))

    @pl.when(d == 1)
    def _():
        aa, uu = a, u
        sh = 1
        while sh < T:
            ok = row < T - sh
            uu = uu + aa * jnp.where(ok, pltpu.roll(uu, T - sh, 0), 0.0)
            aa = aa * jnp.where(ok, pltpu.roll(aa, T - sh, 0), 1.0)
            sh *= 2
        h = uu + aa * carry[0:1]
        o_ref[0] = h
        carry[...] = jnp.broadcast_to(h[0:1], (SUBLANES, W))

    @pl.when(i == nblk - 1)
    def _():
        hfin_ref[0, 0] = carry[0:1]


def _lru(lru_arr, cw, cb, wg, bg, lam, h0, B, T):
    nblk = T // LRU_CHUNK
    per8 = LRU_CHUNK // SUBLANES
    n8 = B * T // SUBLANES
    blk = lambda b, d, i: b * nblk + jnp.where(d == 0, i, nblk - 1 - i)
    dirw = lambda b, d, i: (d, 0, 0)
    st = lambda b, d, i: (b, d, 0, 0)
    return pl.pallas_call(
        functools.partial(_lru_kernel, nblk=nblk),
        grid=(B, 2, nblk),
        in_specs=[
            pl.BlockSpec((LRU_CHUNK, LRU_WIDTH), lambda b, d, i: (blk(b, d, i), 0)),
            pl.BlockSpec((SUBLANES, LRU_WIDTH), lambda b, d, i: (jnp.maximum(blk(b, d, i) * per8 - 1, 0), 0)),
            pl.BlockSpec((SUBLANES, LRU_WIDTH), lambda b, d, i: (jnp.minimum((blk(b, d, i) + 1) * per8, n8 - 1), 0)),
            pl.BlockSpec((SUBLANES, LRU_WIDTH), lambda b, d, i: (0, 0)),
            pl.BlockSpec((1, LRU_WIDTH), lambda b, d, i: (0, 0)),
            pl.BlockSpec((1, LRU_WIDTH, 2 * LRU_WIDTH), dirw),
            pl.BlockSpec((1, 1, 2 * LRU_WIDTH), dirw),
            pl.BlockSpec((1, 1, LRU_WIDTH), dirw),
            pl.BlockSpec((1, 1, 1, LRU_WIDTH), st),
        ],
        out_specs=[
            pl.BlockSpec((1, LRU_CHUNK, LRU_WIDTH), lambda b, d, i: (d, blk(b, d, i), 0)),
            pl.BlockSpec((1, 1, 1, LRU_WIDTH), st),
        ],
        out_shape=[
            jax.ShapeDtypeStruct((2, B * T, LRU_WIDTH), F32),
            jax.ShapeDtypeStruct((B, 2, 1, LRU_WIDTH), F32),
        ],
        scratch_shapes=[pltpu.VMEM((SUBLANES, LRU_WIDTH), F32)],
        compiler_params=_cparams("parallel", "parallel", "arbitrary"),
        name="lru_scan",
    )(lru_arr, lru_arr, lru_arr, cw, cb, wg, bg, lam, h0)


def _out_kernel(h_ref, attn_ref, go_ref, gr_ref, gg_ref, lh_ref, lg_ref, w_ref, g1_ref, a2_ref, s2_ref, wr_ref, br_ref,
                vt_ref, lt_ref, hn_ref, v_ref, lgt_ref, *, n_main):
    i = pl.program_id(0)

    @pl.when(i < n_main)
    def _():
        _out_body(h_ref, attn_ref, go_ref, gr_ref, gg_ref, lh_ref, lg_ref, w_ref, g1_ref, a2_ref, s2_ref, wr_ref, br_ref,
                  hn_ref, v_ref, lgt_ref)

    @pl.when(i >= n_main)
    def _():
        v_ref[...] = vt_ref[...]
        lgt_ref[...] = lt_ref[...]


def _out_body(h_ref, attn_ref, go_ref, gr_ref, gg_ref, lh_ref, lg_ref, w_ref, g1_ref, a2_ref, s2_ref, wr_ref, br_ref,
              hn_ref, v_ref, lgt_ref):
    o = go_ref[0] + go_ref[1]
    gr = gr_ref[...]
    gate = gr * jax.nn.sigmoid(gr)
    parts = []
    for hh in range(GLA_HEADS):
        oh = o[:, hh * GLA_DV:(hh + 1) * GLA_DV]
        y = (oh * lax.rsqrt(jnp.mean(oh * oh, axis=-1, keepdims=True) + RMS_EPS)) * gg_ref[...]
        parts.append(y * gate[:, hh * GLA_DV:(hh + 1) * GLA_DV])
    gla = jnp.concatenate(parts, axis=1).astype(BF16)
    lg = lg_ref[...]
    gelu = lg * (0.5 * (1.0 + jnp.tanh(np.sqrt(2.0 / np.pi).astype(np.float32) * (lg + 0.044715 * (lg * lg * lg)))))
    lru = ((lh_ref[0] + lh_ref[1]) * gelu).astype(BF16)
    y = jnp.dot(attn_ref[...], w_ref[0:ATTN_Q], preferred_element_type=F32)
    y = y + jnp.dot(gla, w_ref[ATTN_Q:ATTN_Q + GLA_V], preferred_element_type=F32)
    y = y + jnp.dot(lru, w_ref[ATTN_Q + GLA_V:D_MIX], preferred_element_type=F32)
    hn = h_ref[...] + g1_ref[0] * y
    hn_ref[...] = hn
    v = (hn * lax.rsqrt(jnp.mean(hn * hn, axis=-1, keepdims=True) + RMS_EPS)) * a2_ref[0] + s2_ref[0]
    v_ref[...] = _pack_bf16_pairs(v)
    lgt_ref[...] = jnp.dot(v, wr_ref[...], preferred_element_type=F32, precision=lax.Precision.HIGHEST) + br_ref[...]


def _mixer_out(h, attn, go, gla_arr, gg, lh, lru_arr, w_out, g1, a2, s2, wr, br, seq_len, tail):
    rows, D = h.shape
    tm = _tile(seq_len, 256)
    per = seq_len // tm
    n_main = rows // tm
    if tail is None:
        tail = (jnp.zeros((tm, D // 2), I32), jnp.zeros((tm, LANES), F32))
        n_tail = 0
    else:
        assert tail[0].shape[0] % tm == 0
        n_tail = tail[0].shape[0] // tm
    n_tok = rows + n_tail * tm
    main = lambda i: jnp.minimum(i, n_main - 1)
    row = lambda i: (main(i), 0)
    row3 = lambda i: (0, main(i), 0)
    bat = lambda i: (main(i) // per, 0, 0)
    const = lambda i: (0, 0)
    tok = lambda i: (i, 0)
    trow = lambda i: (jnp.maximum(i - n_main, 0), 0)
    in_specs = [
        pl.BlockSpec((tm, D), row),
        pl.BlockSpec((tm, ATTN_Q), row),
        pl.BlockSpec((2, tm, GLA_V), row3),
        pl.BlockSpec((tm, GLA_V), lambda i: (main(i), 2)),
        pl.BlockSpec((1, GLA_DV), const),
        pl.BlockSpec((2, tm, LRU_WIDTH), row3),
        pl.BlockSpec((tm, LRU_WIDTH), lambda i: (main(i), 1)),
        pl.BlockSpec((D_MIX, D), const, pipeline_mode=pl.Buffered(1)),
        pl.BlockSpec((1, 1, D), bat),
        pl.BlockSpec((1, 1, D), bat),
        pl.BlockSpec((1, 1, D), bat),
        pl.BlockSpec((D, LANES), const),
        pl.BlockSpec((1, LANES), const),
        pl.BlockSpec((tm, D // 2), trow),
        pl.BlockSpec((tm, LANES), trow),
    ]
    args = [h, attn, go, gla_arr, gg, lh, lru_arr, w_out, g1, a2, s2, wr, br, tail[0], tail[1]]
    return pl.pallas_call(
        functools.partial(_out_kernel, n_main=n_main),
        grid=(n_main + n_tail,),
        in_specs=in_specs,
        out_specs=[
            pl.BlockSpec((tm, D), row),
            pl.BlockSpec((tm, D // 2), tok),
            pl.BlockSpec((tm, LANES), tok),
        ],
        out_shape=[
            jax.ShapeDtypeStruct((rows, D), F32),
            jax.ShapeDtypeStruct((n_tok, D // 2), I32),
            jax.ShapeDtypeStruct((n_tok, LANES), F32),
        ],
        compiler_params=_cparams("arbitrary"),
        name="mixer_out",
    )(*args)


def _route_kernel(lg_ref, oi_ref, of_ref, cnt_ref, carry):
    i = pl.program_id(0)

    @pl.when(i == 0)
    def _():
        carry[...] = jnp.zeros_like(carry)

    lg = lg_ref[...]
    tm = lg.shape[0]
    col = lax.broadcasted_iota(I32, lg.shape, 1)
    colf = col.astype(F32)
    big = float(LANES)
    is_g = col < N_GROUPS
    gm = jnp.max(jnp.where(is_g, lg, -jnp.inf), axis=-1, keepdims=True)
    eg = jnp.where(is_g, jnp.exp(lg - gm), 0.0)
    pg = eg / jnp.sum(eg, axis=-1, keepdims=True)
    p_grp = jnp.max(pg, axis=-1, keepdims=True)
    grp = jnp.min(jnp.where(is_g & (pg == p_grp), colf, big), axis=-1, keepdims=True).astype(I32)

    sel = (col >= N_GROUPS) & (col < N_GROUPS + N_EXPERTS) & (((col - N_GROUPS) >> 3) == grp)
    em = jnp.max(jnp.where(sel, lg, -jnp.inf), axis=-1, keepdims=True)
    ee = jnp.where(sel, jnp.exp(lg - em), 0.0)
    pe = ee / jnp.sum(ee, axis=-1, keepdims=True)
    p1 = jnp.max(jnp.where(sel, pe, -1.0), axis=-1, keepdims=True)
    c1 = jnp.min(jnp.where(sel & (pe == p1), colf, big), axis=-1, keepdims=True).astype(I32)
    rest = sel & (col != c1)
    p2 = jnp.max(jnp.where(rest, pe, -1.0), axis=-1, keepdims=True)
    c2 = jnp.min(jnp.where(rest & (pe == p2), colf, big), axis=-1, keepdims=True).astype(I32)
    e1 = c1 - N_GROUPS
    e2 = c2 - N_GROUPS
    den = p1 + p2
    g1 = p_grp * (p1 / den)
    g2 = p_grp * (p2 / den)

    hit1 = col == e1
    hit2 = col == e2
    oh = jnp.where(hit1 | hit2, 1.0, 0.0)
    r_i = lax.broadcasted_iota(I32, (tm, tm), 0)
    c_i = lax.broadcasted_iota(I32, (tm, tm), 1)
    tri = jnp.where(c_i < r_i, 1.0, 0.0).astype(BF16)
    before = jnp.dot(tri, oh.astype(BF16), preferred_element_type=F32) + carry[0:1]
    r1 = jnp.sum(jnp.where(hit1, before, 0.0), axis=-1, keepdims=True).astype(I32)
    r2 = jnp.sum(jnp.where(hit2, before, 0.0), axis=-1, keepdims=True).astype(I32)
    new = carry[0:1] + jnp.sum(oh, axis=0, keepdims=True)
    carry[...] = jnp.broadcast_to(new, carry.shape)
    cnt_ref[...] = jnp.broadcast_to(new, cnt_ref.shape)

    zero = jnp.zeros_like(col)
    oi_ref[...] = jnp.where(col == 0, e1, jnp.where(col == 1, e2, jnp.where(col == 2, r1, jnp.where(col == 3, r2, zero))))
    of_ref[...] = jnp.where(col == 0, g1, jnp.where(col == 1, g2, 0.0))


def _route(logits):
    n_tok = logits.shape[0]
    tm = _tile(n_tok, 256)
    row = lambda i: (i, 0)
    return pl.pallas_call(
        _route_kernel,
        grid=(n_tok // tm,),
        in_specs=[pl.BlockSpec((tm, LANES), row)],
        out_specs=[
            pl.BlockSpec((tm, LANES), row),
            pl.BlockSpec((tm, LANES), row),
            pl.BlockSpec((SUBLANES, LANES), lambda i: (0, 0)),
        ],
        out_shape=[
            jax.ShapeDtypeStruct((n_tok, LANES), I32),
            jax.ShapeDtypeStruct((n_tok, LANES), F32),
            jax.ShapeDtypeStruct((SUBLANES, LANES), F32),
        ],
        scratch_shapes=[pltpu.VMEM((SUBLANES, LANES), F32)],
        compiler_params=_cparams("arbitrary"),
        name="route",
    )(logits)


def _dispatch_kernel(dest_ref, v_ref, xs_in_ref, xs_ref, sem):
    del xs_in_ref
    tm = v_ref.shape[0]

    def copy(j):
        r = j >> 1
        return pltpu.make_async_copy(v_ref.at[pl.ds(r, 1)], xs_ref.at[pl.ds(dest_ref[0, 0, j], 1)], sem)

    def start(j, c):
        copy(j).start()
        return c

    def wait(j, c):
        copy(j).wait()
        return c

    lax.fori_loop(0, tm * TOP_K, start, 0)
    lax.fori_loop(0, tm * TOP_K, wait, 0)


def _dispatch(v_packed, dest, xs_zero):
    n_tok, Dh = v_packed.shape
    tm = _tile(n_tok, 256)
    nblk = n_tok // tm
    return pl.pallas_call(
        _dispatch_kernel,
        grid=(nblk,),
        in_specs=[
            pl.BlockSpec((1, 1, tm * TOP_K), lambda i: (i, 0, 0), memory_space=pltpu.SMEM),
            pl.BlockSpec((tm, Dh), lambda i: (i, 0)),
            pl.BlockSpec(memory_space=pl.ANY),
        ],
        out_specs=pl.BlockSpec(memory_space=pl.ANY),
        out_shape=jax.ShapeDtypeStruct(xs_zero.shape, I32),
        scratch_shapes=[pltpu.SemaphoreType.DMA(())],
        input_output_aliases={2: 0},
        compiler_params=pltpu.CompilerParams(dimension_semantics=("arbitrary",), vmem_limit_bytes=VMEM_LIMIT_BYTES,
                                             has_side_effects=True),
        name="dispatch",
    )(dest.reshape(nblk, 1, tm * TOP_K), v_packed, xs_zero)


def _cast_rows(w_ref, dst):
    step = _tile(dst.shape[0], 256)

    def body(r, c):
        rows = pl.ds(pl.multiple_of(r * step, step), step)
        dst[rows, :] = w_ref[0, rows, :].astype(BF16)
        return c

    lax.fori_loop(0, dst.shape[0] // step, body, 0)


def _expert_changed(be_ref, i):
    prev = be_ref[jnp.maximum(i - 1, 0)]
    return (i == 0) | (be_ref[i] != prev)


def _moe_up_kernel(be_ref, nu_ref, x_ref, w1_ref, w3_ref, o_ref, w1s, w3s):
    i = pl.program_id(0)

    @pl.when(i < nu_ref[0])
    def _():
        @pl.when(_expert_changed(be_ref, i))
        def _():
            _cast_rows(w1_ref, w1s)
            _cast_rows(w3_ref, w3s)

        xb = _unpack_bf16_pairs(x_ref[...]).astype(BF16)
        a = jnp.dot(xb, w1s[...], preferred_element_type=F32)
        b = jnp.dot(xb, w3s[...], preferred_element_type=F32)
        o_ref[...] = ((a * jax.nn.sigmoid(a)) * b).astype(BF16)

    @pl.when(i >= nu_ref[0])
    def _():
        o_ref[...] = jnp.zeros_like(o_ref)


def _moe_down_kernel(be_ref, nu_ref, h_ref, w2_ref, o_ref, w2s):
    i = pl.program_id(0)

    @pl.when(i < nu_ref[0])
    def _():
        @pl.when(_expert_changed(be_ref, i))
        def _():
            _cast_rows(w2_ref, w2s)

        o_ref[...] = _pack_bf16_pairs(jnp.dot(h_ref[...], w2s[...], preferred_element_type=F32))

    @pl.when(i >= nu_ref[0])
    def _():
        o_ref[...] = jnp.zeros_like(o_ref)


def _moe_experts(xs, blk_e, n_used, w1, w3, w2):
    P, Dh = xs.shape
    D = 2 * Dh
    FF = w1.shape[-1]
    nb = P // ROW_BLOCK
    rowi = lambda i, be, nu: (jnp.minimum(i, nu[0] - 1), 0)
    rowo = lambda i, be, nu: (i, 0)
    wi = lambda i, be, nu: (be[jnp.minimum(i, nu[0] - 1)], 0, 0)
    h1 = pl.pallas_call(
        _moe_up_kernel,
        grid_spec=pltpu.PrefetchScalarGridSpec(
            num_scalar_prefetch=2,
            grid=(nb,),
            in_specs=[
                pl.BlockSpec((ROW_BLOCK, Dh), rowi),
                pl.BlockSpec((1, D, FF), wi),
                pl.BlockSpec((1, D, FF), wi),
            ],
            out_specs=pl.BlockSpec((ROW_BLOCK, FF), rowo),
            scratch_shapes=[pltpu.VMEM((D, FF), BF16), pltpu.VMEM((D, FF), BF16)],
        ),
        out_shape=jax.ShapeDtypeStruct((P, FF), BF16),
        compiler_params=_cparams("arbitrary"),
        name="moe_up",
    )(blk_e, n_used, xs, w1, w3)
    return pl.pallas_call(
        _moe_down_kernel,
        grid_spec=pltpu.PrefetchScalarGridSpec(
            num_scalar_prefetch=2,
            grid=(nb,),
            in_specs=[
                pl.BlockSpec((ROW_BLOCK, FF), rowi),
                pl.BlockSpec((1, FF, D), wi),
            ],
            out_specs=pl.BlockSpec((ROW_BLOCK, Dh), rowo),
            scratch_shapes=[pltpu.VMEM((FF, D), BF16)],
        ),
        out_shape=jax.ShapeDtypeStruct((P, Dh), I32),
        compiler_params=_cparams("arbitrary"),
        name="moe_down",
    )(blk_e, n_used, h1, w2)


def _combine_kernel(dest_ref, h_ref, gate_ref, g2_ref, fg_ref, ys_ref, o_ref, buf, sem, *, final):
    tm = h_ref.shape[0]

    def copy(j):
        r = j >> 1
        kk = j & 1
        return pltpu.make_async_copy(ys_ref.at[pl.ds(dest_ref[0, 0, j], 1)], buf.at[kk, pl.ds(r, 1)], sem)

    def start(j, c):
        copy(j).start()
        return c

    def wait(j, c):
        copy(j).wait()
        return c

    lax.fori_loop(0, tm * TOP_K, start, 0)
    lax.fori_loop(0, tm * TOP_K, wait, 0)
    gate = gate_ref[...]
    f = _unpack_bf16_pairs(buf[0]) * gate[:, 0:1] + _unpack_bf16_pairs(buf[1]) * gate[:, 1:2]
    hn = h_ref[...] + g2_ref[0] * f
    if final:
        hn = (hn * lax.rsqrt(jnp.mean(hn * hn, axis=-1, keepdims=True) + RMS_EPS)) * fg_ref[...]
    o_ref[...] = hn


def _combine(h, dest, gates, g2, fg, ys, seq_len, tok_off, final):
    rows, D = h.shape
    tm = _tile(seq_len, 256)
    per = seq_len // tm
    off = tok_off // tm
    n_blk_all = dest.shape[0] // (tm * TOP_K)
    return pl.pallas_call(
        functools.partial(_combine_kernel, final=final),
        grid=(rows // tm,),
        in_specs=[
            pl.BlockSpec((1, 1, tm * TOP_K), lambda i: (i + off, 0, 0), memory_space=pltpu.SMEM),
            pl.BlockSpec((tm, D), lambda i: (i, 0)),
            pl.BlockSpec((tm, LANES), lambda i: (i + off, 0)),
            pl.BlockSpec((1, 1, D), lambda i: (i // per, 0, 0)),
            pl.BlockSpec((1, D), lambda i: (0, 0)),
            pl.BlockSpec(memory_space=pl.ANY),
        ],
        out_specs=pl.BlockSpec((tm, D), lambda i: (i, 0)),
        out_shape=jax.ShapeDtypeStruct((rows, D), F32),
        scratch_shapes=[pltpu.VMEM((TOP_K, tm, D // 2), I32), pltpu.SemaphoreType.DMA(())],
        compiler_params=_cparams("arbitrary"),
        name="combine",
    )(dest.reshape(n_blk_all, 1, tm * TOP_K), h, gates, g2, fg, ys)


def _rope_tables(S):
    rows = S // GRID_W
    row = jnp.repeat(jnp.arange(rows, dtype=F32), GRID_W)
    col = jnp.tile(jnp.arange(GRID_W, dtype=F32), rows)
    n_freq = HEAD_DIM // 4
    inv = ROPE_THETA ** (-jnp.arange(n_freq, dtype=F32) / n_freq)
    ang = jnp.concatenate([row[:, None] * inv, col[:, None] * inv], axis=-1)
    cos = jnp.repeat(jnp.cos(ang), 2, axis=1)
    sin = jnp.repeat(jnp.sin(ang), 2, axis=1)
    even = (jnp.arange(HEAD_DIM) % 2) == 0
    return cos, jnp.where(even, -sin, 0.0), jnp.where(even, 0.0, sin)


def _block_diag(w):
    n, c, _ = w.shape
    eye = jnp.eye(n, dtype=w.dtype)
    return (eye[:, None, :, None] * w[:, :, None, :]).reshape(n * c, n * c)


def _slot_plan(ids, counts_row, n_tok):
    e = ids[:, 0:TOP_K]
    rank = ids[:, TOP_K:2 * TOP_K]
    counts = counts_row[:N_EXPERTS].astype(I32)
    padded = (counts + ROW_BLOCK - 1) // ROW_BLOCK * ROW_BLOCK
    pad_end = jnp.cumsum(padded)
    pad_start = pad_end - padded
    dest = (pad_start[e] + rank).reshape(n_tok * TOP_K)
    n_blocks = -(-(n_tok * TOP_K + N_EXPERTS * (ROW_BLOCK - 1)) // ROW_BLOCK)
    starts = jnp.arange(n_blocks, dtype=I32) * ROW_BLOCK
    blk_e = jnp.minimum(jnp.searchsorted(pad_end, starts, side="right"), N_EXPERTS - 1).astype(I32)
    n_used = (pad_end[-1] // ROW_BLOCK).astype(I32).reshape(1)
    return dest, blk_e, n_used, n_blocks


def kernel(x, c, ctx, c_ctx, ada_w, ada_b, norm_mix_g, norm_ffn_g, w_in, attn_sink, gla_gate_w2, gla_gate_b, gla_norm_g, lru_conv_w, lru_conv_b, lru_wa, lru_ba, lru_wx, lru_bx, lru_lambda, w_out, router_g_w, router_g_b, router_e_w, router_e_b, moe_w1, moe_w3, moe_w2, final_norm_g):
    B, S, D = x.shape
    C = ctx.shape[1]
    L = ada_w.shape[0]
    assert S % max(ATTN_BLOCK, GLA_CHUNK, LRU_CHUNK, GRID_W) == 0 and C % max(ATTN_BLOCK, GLA_CHUNK, LRU_CHUNK) == 0
    assert B + 1 <= SUBLANES and D % (2 * LANES) == 0

    cond = jnp.concatenate([c, c_ctx[None], jnp.zeros((SUBLANES - B - 1, D), F32)], axis=0)
    mods = _ada_mods(cond, ada_w, ada_b).reshape(L, SUBLANES, 6, D)

    cos_l, se_l, so_l = _rope_tables(S)
    cos_c = jnp.ones((C, HEAD_DIM), F32)
    zero_c = jnp.zeros((C, HEAD_DIM), F32)
    e2_np, m2_np = _gla_constants()
    gla_e = jnp.asarray(e2_np, BF16)
    gla_m = jnp.asarray(m2_np, F32)

    h_lat = x.reshape(B * S, D)
    h_ctx = ctx.reshape(B * C, D)
    out = None
    for l in range(L):
        last = l == L - 1
        ml = mods[l, :B]
        mc = jnp.broadcast_to(mods[l, B][None], (B, 6, D))
        per_b = lambda m, j: m[:, j][:, None, :]
        a1_l, a1_c = [(1.0 + per_b(m, 1)) * norm_mix_g[l] for m in (ml, mc)]
        a2_l, a2_c = [(1.0 + per_b(m, 4)) * norm_ffn_g[l] for m in (ml, mc)]

        w = w_in[l]
        c_gz = ATTN_Q + 2 * ATTN_KV + 2 * GLA_QK + 2 * GLA_V
        w_packed = jnp.concatenate(
            [w[:, :c_gz], w[:, c_gz:c_gz + 2 * GLA_RANK], jnp.zeros((D, GZ_W - 2 * GLA_RANK), F32), w[:, c_gz + 2 * GLA_RANK:]],
            axis=1).astype(BF16)
        qkv_l, gla_l, gz_l, lru_l = _in_proj(h_lat, a1_l, per_b(ml, 0), cos_l, se_l, so_l, w_packed, S)
        qkv_c, gla_c, gz_c, lru_c = _in_proj(h_ctx, a1_c, per_b(mc, 0), cos_c, zero_c, zero_c, w_packed, C)

        sink_b = jnp.broadcast_to(attn_sink[l][:, None], (N_Q_HEADS, LANES)).astype(F32)
        attn_l = _attn_latent(qkv_l, qkv_c, sink_b, B, S, C)

        w2p = jnp.zeros((2, GZ_W, GLA_QK), F32)
        w2p = w2p.at[0, :GLA_RANK].set(gla_gate_w2[l, 0]).at[1, GLA_RANK:2 * GLA_RANK].set(gla_gate_w2[l, 1]).astype(BF16)
        gbias = gla_gate_b[l].reshape(2, 1, GLA_QK)
        s_zero = jnp.zeros((B, 2, GLA_QK, GLA_DV), F32)
        go_c, s_ctx = _gla(gla_c, gz_c, w2p, gbias, gla_e, gla_m, s_zero, B, C)
        go_l, _ = _gla(gla_l, gz_l, w2p, gbias, gla_e, gla_m, s_ctx, B, S)

        cw = jnp.concatenate([lru_conv_w[l], jnp.zeros((SUBLANES - CONV_W, LRU_WIDTH), F32)], axis=0)
        cb = lru_conv_b[l].reshape(1, LRU_WIDTH)
        wg = jnp.stack([jnp.concatenate([_block_diag(lru_wa[l, d]), _block_diag(lru_wx[l, d])], axis=1) for d in range(2)]).astype(BF16)
        bg = jnp.concatenate([lru_ba[l], lru_bx[l]], axis=1).reshape(2, 1, 2 * LRU_WIDTH)
        lam = lru_lambda[l].reshape(2, 1, LRU_WIDTH)
        h_zero = jnp.zeros((B, 2, 1, LRU_WIDTH), F32)
        lh_c, hs_ctx = _lru(lru_c, cw, cb, wg, bg, lam, h_zero, B, C)
        lh_l, _ = _lru(lru_l, cw, cb, wg, bg, lam, hs_ctx, B, S)

        wo = w_out[l].astype(BF16)
        gg = gla_norm_g[l].reshape(1, GLA_DV)
        wr = jnp.concatenate([router_g_w[l], router_e_w[l], jnp.zeros((D, LANES - N_GROUPS - N_EXPERTS), F32)], axis=1)
        br = jnp.concatenate([router_g_b[l], router_e_b[l], jnp.zeros((LANES - N_GROUPS - N_EXPERTS,), F32)]).reshape(1, LANES)
        n_tok = B * S if last else B * (S + C)
        tail = None
        if not last:
            attn_c = _attn_context(qkv_c, sink_b, B, C)
            h_ctx, v_c, lgt_c = _mixer_out(h_ctx, attn_c, go_c, gla_c, gg, lh_c, lru_c, wo, per_b(mc, 2), a2_c, per_b(mc, 3),
                                           wr, br, C, None)
            tail = (v_c, lgt_c)
        h_lat, v_all, lgt_all = _mixer_out(h_lat, attn_l, go_l, gla_l, gg, lh_l, lru_l, wo, per_b(ml, 2), a2_l, per_b(ml, 3),
                                           wr, br, S, tail)

        ids, gates, counts = _route(lgt_all)
        dest, blk_e, n_used, n_blocks = _slot_plan(ids, counts[0], n_tok)
        xs = _dispatch(v_all, dest, jnp.zeros((n_blocks * ROW_BLOCK, D // 2), I32))
        ys = _moe_experts(xs, blk_e, n_used, moe_w1[l], moe_w3[l], moe_w2[l])
        fg = final_norm_g.reshape(1, D)
        h_lat = _combine(h_lat, dest, gates, per_b(ml, 5), fg, ys, S, 0, last)
        if not last:
            h_ctx = _combine(h_ctx, dest, gates, per_b(mc, 5), fg, ys, C, B * S, False)
        out = h_lat
    return out.reshape(B, S, D)
```

```python
import functools

import numpy as np
import jax
import jax.numpy as jnp
from jax import lax
from jax.experimental import pallas as pl
from jax.experimental.pallas import tpu as pltpu

F32 = jnp.float32
BF16 = jnp.bfloat16
I32 = jnp.int32

GRID_W = 64
RMS_EPS = 1e-6
N_Q_HEADS = 8
N_KV_HEADS = 2
HEAD_DIM = 128
WINDOW = 128
ATTN_BLOCK = 128
ROPE_THETA = 10000.0
GLA_HEADS = 4
GLA_DK = 64
GLA_DV = 128
GLA_RANK = 16
GLA_TAU = 16.0
LRU_WIDTH = 512
LRU_BLOCKS = 8
LRU_C = 8.0
CONV_W = 4
N_GROUPS = 4
EXPERTS_PER_GROUP = 8
N_EXPERTS = 32
TOP_K = 2
ATTN_Q = N_Q_HEADS * HEAD_DIM
ATTN_KV = N_KV_HEADS * HEAD_DIM
GLA_QK = GLA_HEADS * GLA_DK
GLA_V = GLA_HEADS * GLA_DV
D_MIX = ATTN_Q + GLA_V + LRU_WIDTH

LANES = 128
SUBLANES = 8
VMEM_LIMIT_BYTES = 56 * 1024 * 1024

QKV_W = ATTN_Q + 2 * ATTN_KV
GLA_W = 2 * GLA_QK + 2 * GLA_V
GZ_W = LANES
LRU_W = 2 * LRU_WIDTH
COL_GLA = QKV_W
COL_GZ = COL_GLA + GLA_W
COL_LRU = COL_GZ + GZ_W
W_IN_PACKED = COL_LRU + LRU_W

GLA_CHUNK = 128
GLA_LEVELS = 7
LRU_CHUNK = 128
ROW_BLOCK = 256


def _cparams(*sem):
    return pltpu.CompilerParams(dimension_semantics=sem, vmem_limit_bytes=VMEM_LIMIT_BYTES)


def _tile(n, pref):
    t = min(n, pref)
    while n % t:
        t -= SUBLANES
    return t


def _softplus(x):
    return jnp.maximum(x, 0.0) + jnp.log1p(jnp.exp(-jnp.abs(x)))


def _pack_bf16_pairs(x):
    n = x.shape[1] // 2
    lo = lax.bitcast_convert_type(x[:, :n].astype(BF16).astype(F32), I32)
    hi = lax.bitcast_convert_type(x[:, n:].astype(BF16).astype(F32), I32)
    return hi | lax.shift_right_logical(lo, jnp.full(lo.shape, 16, I32))


def _unpack_bf16_pairs(p):
    lo = lax.bitcast_convert_type(lax.shift_left(p, jnp.full(p.shape, 16, I32)), F32)
    hi = lax.bitcast_convert_type(p & jnp.int32(-65536), F32)
    return jnp.concatenate([lo, hi], axis=1)


def _store_token_tiles(ref, packed):
    m = packed.shape[0]
    sub = packed.shape[1] // LANES
    for s in range(sub):
        ref[pl.ds(s, m, stride=sub), :] = packed[:, s * LANES:(s + 1) * LANES]


def _load_token_tiles(ref, m):
    sub = ref.shape[0] // m
    return jnp.concatenate([ref[pl.ds(s, m, stride=sub), :] for s in range(sub)], axis=1)


def _ada_kernel(c_ref, w_ref, b_ref, o_ref):
    c = c_ref[...]
    s = c * jax.nn.sigmoid(c)
    o_ref[0] = jnp.dot(s.astype(BF16), w_ref[0].astype(BF16), preferred_element_type=F32) + b_ref[0]


def _ada_mods(cond, ada_w, ada_b):
    L, D, D6 = ada_w.shape
    tn = _tile(D6, 1536)
    while tn % LANES:
        tn -= SUBLANES
    return pl.pallas_call(
        _ada_kernel,
        grid=(L, D6 // tn),
        in_specs=[
            pl.BlockSpec((SUBLANES, D), lambda l, j: (0, 0)),
            pl.BlockSpec((1, D, tn), lambda l, j: (l, 0, j)),
            pl.BlockSpec((1, 1, tn), lambda l, j: (l, 0, j)),
        ],
        out_specs=pl.BlockSpec((1, SUBLANES, tn), lambda l, j: (l, 0, j)),
        out_shape=jax.ShapeDtypeStruct((L, SUBLANES, D6), F32),
        compiler_params=_cparams("parallel", "parallel"),
        name="ada_mods",
    )(cond, ada_w, ada_b.reshape(L, 1, D6))


def _in_kernel(x_ref, a_ref, s_ref, cos_ref, se_ref, so_ref, w_ref, qkv_ref, gla_ref, gz_ref, lru_ref):
    x = x_ref[...]
    ms = jnp.mean(x * x, axis=-1, keepdims=True)
    u = (x * lax.rsqrt(ms + RMS_EPS)) * a_ref[0] + s_ref[0]
    ub = u.astype(BF16)
    cos, se, so = cos_ref[...], se_ref[...], so_ref[...]
    n_rot = N_Q_HEADS + N_KV_HEADS
    for j in range(n_rot):
        zh = jnp.dot(ub, w_ref[:, j * HEAD_DIM:(j + 1) * HEAD_DIM], preferred_element_type=F32)
        rot = zh * cos + pltpu.roll(zh, HEAD_DIM - 1, 1) * se + pltpu.roll(zh, 1, 1) * so
        qkv_ref[:, j * HEAD_DIM:(j + 1) * HEAD_DIM] = rot.astype(BF16)
    c0 = n_rot * HEAD_DIM
    qkv_ref[:, c0:QKV_W] = jnp.dot(ub, w_ref[:, c0:QKV_W], preferred_element_type=F32).astype(BF16)
    gla_ref[...] = jnp.dot(ub, w_ref[:, COL_GLA:COL_GZ], preferred_element_type=F32)
    gz_ref[...] = jnp.dot(ub, w_ref[:, COL_GZ:COL_LRU], preferred_element_type=F32)
    lru_ref[...] = jnp.dot(ub, w_ref[:, COL_LRU:W_IN_PACKED], preferred_element_type=F32)


def _in_proj(h, a, s, cos, se, so, w, seq_len):
    rows, D = h.shape
    tm = _tile(seq_len, 512)
    per = seq_len // tm
    row = lambda i: (i, 0)
    bat = lambda i: (i // per, 0, 0)
    tab = lambda i: (i % per, 0)
    return pl.pallas_call(
        _in_kernel,
        grid=(rows // tm,),
        in_specs=[
            pl.BlockSpec((tm, D), row),
            pl.BlockSpec((1, 1, D), bat),
            pl.BlockSpec((1, 1, D), bat),
            pl.BlockSpec((tm, HEAD_DIM), tab),
            pl.BlockSpec((tm, HEAD_DIM), tab),
            pl.BlockSpec((tm, HEAD_DIM), tab),
            pl.BlockSpec((D, W_IN_PACKED), lambda i: (0, 0), pipeline_mode=pl.Buffered(1)),
        ],
        out_specs=[
            pl.BlockSpec((tm, QKV_W), row),
            pl.BlockSpec((tm, GLA_W), row),
            pl.BlockSpec((tm, GZ_W), row),
            pl.BlockSpec((tm, LRU_W), row),
        ],
        out_shape=[
            jax.ShapeDtypeStruct((rows, QKV_W), BF16),
            jax.ShapeDtypeStruct((rows, GLA_W), F32),
            jax.ShapeDtypeStruct((rows, GZ_W), F32),
            jax.ShapeDtypeStruct((rows, LRU_W), F32),
        ],
        compiler_params=_cparams("parallel"),
        name="in_proj",
    )(h, a, s, cos, se, so, w)


def _attn_heads(q, k_loc, v_loc, valid, k_ctx, v_ctx, sink_ref, o_ref):
    scale = HEAD_DIM ** -0.5
    group = N_Q_HEADS // N_KV_HEADS
    nt = (((1,), (1,)), ((), ()))
    for head in range(N_Q_HEADS):
        hk = head // group
        qh = q[:, head * HEAD_DIM:(head + 1) * HEAD_DIM]
        sink = sink_ref[head:head + 1, 0:1]
        s_ctx = lax.dot_general(qh, k_ctx[hk], nt, preferred_element_type=F32) * scale
        m = jnp.maximum(jnp.max(s_ctx, axis=-1, keepdims=True), sink)
        if k_loc is not None:
            s_loc = lax.dot_general(qh, k_loc[hk], nt, preferred_element_type=F32) * scale
            s_loc = jnp.where(valid, s_loc, -jnp.inf)
            m = jnp.maximum(m, jnp.max(s_loc, axis=-1, keepdims=True))
        p_ctx = jnp.exp(s_ctx - m)
        den = jnp.sum(p_ctx, axis=-1, keepdims=True) + jnp.exp(sink - m)
        o = jnp.dot(p_ctx.astype(BF16), v_ctx[hk], preferred_element_type=F32)
        if k_loc is not None:
            p_loc = jnp.exp(s_loc - m)
            den = den + jnp.sum(p_loc, axis=-1, keepdims=True)
            o = o + jnp.dot(p_loc.astype(BF16), v_loc[hk], preferred_element_type=F32)
        o_ref[:, head * HEAD_DIM:(head + 1) * HEAD_DIM] = (o / den).astype(o_ref.dtype)


def _split_kv(kv):
    ks = [kv[:, h * HEAD_DIM:(h + 1) * HEAD_DIM] for h in range(N_KV_HEADS)]
    vs = [kv[:, ATTN_KV + h * HEAD_DIM:ATTN_KV + (h + 1) * HEAD_DIM] for h in range(N_KV_HEADS)]
    return ks, vs


def _attn_lat_kernel(q_ref, kvp_ref, kvc_ref, kvn_ref, kvx_ref, sink_ref, o_ref, *, nb):
    n = pl.program_id(1)
    kp, vp = _split_kv(kvp_ref[...])
    kc, vc = _split_kv(kvc_ref[...])
    kn, vn = _split_kv(kvn_ref[...])
    k_ctx, v_ctx = _split_kv(kvx_ref[...])
    k_loc = [jnp.concatenate([kp[h], kc[h], kn[h]], axis=0) for h in range(N_KV_HEADS)]
    v_loc = [jnp.concatenate([vp[h], vc[h], vn[h]], axis=0) for h in range(N_KV_HEADS)]
    i = lax.broadcasted_iota(I32, (ATTN_BLOCK, 3 * ATTN_BLOCK), 0)
    j = lax.broadcasted_iota(I32, (ATTN_BLOCK, 3 * ATTN_BLOCK), 1)
    valid = (j >= i) & (j <= i + 2 * WINDOW)
    valid = valid & ((j >= ATTN_BLOCK) | (n > 0)) & ((j < 2 * ATTN_BLOCK) | (n < nb - 1))
    _attn_heads(q_ref[...], k_loc, v_loc, valid, k_ctx, v_ctx, sink_ref, o_ref)


def _attn_ctx_kernel(q_ref, kvx_ref, sink_ref, o_ref):
    k_ctx, v_ctx = _split_kv(kvx_ref[...])
    _attn_heads(q_ref[...], None, None, None, k_ctx, v_ctx, sink_ref, o_ref)


def _attn_latent(qkv_l, qkv_c, sink_b, B, S, C):
    nb = S // ATTN_BLOCK
    kvw = 2 * ATTN_KV
    kvcol = ATTN_Q // kvw
    return pl.pallas_call(
        functools.partial(_attn_lat_kernel, nb=nb),
        grid=(B, nb),
        in_specs=[
            pl.BlockSpec((ATTN_BLOCK, ATTN_Q), lambda b, n: (b * nb + n, 0)),
            pl.BlockSpec((ATTN_BLOCK, kvw), lambda b, n: (b * nb + jnp.maximum(n - 1, 0), kvcol)),
            pl.BlockSpec((ATTN_BLOCK, kvw), lambda b, n: (b * nb + n, kvcol)),
            pl.BlockSpec((ATTN_BLOCK, kvw), lambda b, n: (b * nb + jnp.minimum(n + 1, nb - 1), kvcol)),
            pl.BlockSpec((C, kvw), lambda b, n: (b, kvcol)),
            pl.BlockSpec((SUBLANES, LANES), lambda b, n: (0, 0)),
        ],
        out_specs=pl.BlockSpec((ATTN_BLOCK, ATTN_Q), lambda b, n: (b * nb + n, 0)),
        out_shape=jax.ShapeDtypeStruct((B * S, ATTN_Q), BF16),
        compiler_params=_cparams("parallel", "parallel"),
        name="attn_latent",
    )(qkv_l, qkv_l, qkv_l, qkv_l, qkv_c, sink_b)


def _attn_context(qkv_c, sink_b, B, C):
    nb = C // ATTN_BLOCK
    kvw = 2 * ATTN_KV
    kvcol = ATTN_Q // kvw
    return pl.pallas_call(
        _attn_ctx_kernel,
        grid=(B, nb),
        in_specs=[
            pl.BlockSpec((ATTN_BLOCK, ATTN_Q), lambda b, n: (b * nb + n, 0)),
            pl.BlockSpec((C, kvw), lambda b, n: (b, kvcol)),
            pl.BlockSpec((SUBLANES, LANES), lambda b, n: (0, 0)),
        ],
        out_specs=pl.BlockSpec((ATTN_BLOCK, ATTN_Q), lambda b, n: (b * nb + n, 0)),
        out_shape=jax.ShapeDtypeStruct((B * C, ATTN_Q), BF16),
        compiler_params=_cparams("parallel", "parallel"),
        name="attn_context",
    )(qkv_c, qkv_c, sink_b)


def _gla_constants():
    Lc = GLA_CHUNK
    e = np.zeros((GLA_LEVELS + 2, Lc, Lc), np.float32)
    msk = np.zeros((GLA_LEVELS + 1, Lc, Lc), np.float32)
    t = np.arange(Lc)
    for l in range(GLA_LEVELS):
        m = 1 << l
        blk = t // (2 * m)
        upper = (t % (2 * m)) >= m
        bnd = blk * 2 * m + m
        r = t[None, :]
        eq = upper[:, None] & (r >= bnd[:, None]) & (r <= t[:, None])
        ek = (~upper)[:, None] & (r > t[:, None]) & (r <= bnd[:, None] - 1)
        e[l] = (eq | ek).astype(np.float32)
        msk[l] = (upper[:, None] & (~upper)[None, :] & (blk[:, None] == blk[None, :])).astype(np.float32)
    e[GLA_LEVELS] = (t[None, :] <= t[:, None]).astype(np.float32)
    e[GLA_LEVELS + 1] = (t[None, :] > t[:, None]).astype(np.float32)
    msk[GLA_LEVELS] = np.eye(Lc, dtype=np.float32)
    e2 = np.stack([e, e[:, ::-1, ::-1]]).reshape(2, (GLA_LEVELS + 2) * Lc, Lc)
    m2 = np.stack([msk, msk[:, ::-1, ::-1]])
    return e2, m2


def _gla_kernel(x_ref, gz_ref, w2_ref, bias_ref, e_ref, m_ref, s0_ref, o_ref, sfin_ref, s_scr, *, nch):
    i = pl.program_id(2)
    Lc = GLA_CHUNK

    @pl.when(i == 0)
    def _():
        s_scr[...] = s0_ref[0, 0]

    z = jnp.dot(gz_ref[...].astype(BF16), w2_ref[0], preferred_element_type=F32) + bias_ref[0]
    la = (jnp.minimum(z, 0.0) - jnp.log1p(jnp.exp(-jnp.abs(z)))) * (1.0 / GLA_TAU)
    la_hi = la.astype(BF16)
    la_lo = (la - la_hi.astype(F32)).astype(BF16)
    la2 = jnp.concatenate([la_hi, la_lo], axis=1)
    ex = jnp.dot(e_ref[0], la2, preferred_element_type=F32)
    decay = jnp.exp(ex[:, :GLA_QK] + ex[:, GLA_QK:])
    tot = lax.dot_general(la2, jnp.ones((Lc, LANES), BF16), (((0,), (0,)), ((), ())), preferred_element_type=F32)
    a_tot = jnp.exp(tot[:GLA_QK] + tot[GLA_QK:])

    q = x_ref[:, 0:GLA_QK] * (GLA_DK ** -0.5)
    k = x_ref[:, GLA_QK:2 * GLA_QK]
    v = x_ref[:, 2 * GLA_QK:2 * GLA_QK + GLA_V].astype(BF16)
    lane_head = lax.broadcasted_iota(I32, (Lc, GLA_QK), 1) >> 6
    row_head = lax.broadcasted_iota(I32, (GLA_QK, GLA_DV), 0) >> 6
    nt = (((1,), (1,)), ((), ()))

    att = [jnp.zeros((Lc, Lc), F32) for _ in range(GLA_HEADS)]
    for l in range(GLA_LEVELS + 1):
        if l < GLA_LEVELS:
            dl = decay[l * Lc:(l + 1) * Lc]
            ql = (q * dl).astype(BF16)
            kl = (k * dl).astype(BF16)
        else:
            ql = q.astype(BF16)
            kl = k.astype(BF16)
        ml = m_ref[0, l]
        for h in range(GLA_HEADS):
            kh = jnp.where(lane_head == h, kl, jnp.zeros_like(kl))
            att[h] = att[h] + ml * lax.dot_general(ql, kh, nt, preferred_element_type=F32)

    s_old = s_scr[...]
    s_b = s_old.astype(BF16)
    q_in = (q * decay[GLA_LEVELS * Lc:(GLA_LEVELS + 1) * Lc]).astype(BF16)
    k_out = (k * decay[(GLA_LEVELS + 1) * Lc:(GLA_LEVELS + 2) * Lc]).astype(BF16)
    s_new = a_tot * s_old
    for h in range(GLA_HEADS):
        vh = v[:, h * GLA_DV:(h + 1) * GLA_DV]
        qh = jnp.where(lane_head == h, q_in, jnp.zeros_like(q_in))
        o_h = jnp.dot(att[h].astype(BF16), vh, preferred_element_type=F32)
        o_h = o_h + jnp.dot(qh, s_b, preferred_element_type=F32)
        o_ref[0, :, h * GLA_DV:(h + 1) * GLA_DV] = o_h
        c_h = lax.dot_general(k_out, vh, (((0,), (0,)), ((), ())), preferred_element_type=F32)
        s_new = s_new + jnp.where(row_head == h, c_h, 0.0)
    s_scr[...] = s_new

    @pl.when(i == nch - 1)
    def _():
        sfin_ref[0, 0] = s_new


def _gla(gla_arr, gz_arr, w2p, bias, e2, m2, s0, B, T):
    nch = T // GLA_CHUNK
    blk = lambda b, d, i: b * nch + jnp.where(d == 0, i, nch - 1 - i)
    st = lambda b, d, i: (b, d, 0, 0)
    return pl.pallas_call(
        functools.partial(_gla_kernel, nch=nch),
        grid=(B, 2, nch),
        in_specs=[
            pl.BlockSpec((GLA_CHUNK, 2 * GLA_QK + GLA_V), lambda b, d, i: (blk(b, d, i), 0)),
            pl.BlockSpec((GLA_CHUNK, GZ_W), lambda b, d, i: (blk(b, d, i), 0)),
            pl.BlockSpec((1, GZ_W, GLA_QK), lambda b, d, i: (d, 0, 0)),
            pl.BlockSpec((1, 1, GLA_QK), lambda b, d, i: (d, 0, 0)),
            pl.BlockSpec((1, (GLA_LEVELS + 2) * GLA_CHUNK, GLA_CHUNK), lambda b, d, i: (d, 0, 0)),
            pl.BlockSpec((1, GLA_LEVELS + 1, GLA_CHUNK, GLA_CHUNK), lambda b, d, i: (d, 0, 0, 0)),
            pl.BlockSpec((1, 1, GLA_QK, GLA_DV), st),
        ],
        out_specs=[
            pl.BlockSpec((1, GLA_CHUNK, GLA_V), lambda b, d, i: (d, blk(b, d, i), 0)),
            pl.BlockSpec((1, 1, GLA_QK, GLA_DV), st),
        ],
        out_shape=[
            jax.ShapeDtypeStruct((2, B * T, GLA_V), F32),
            jax.ShapeDtypeStruct((B, 2, GLA_QK, GLA_DV), F32),
        ],
        scratch_shapes=[pltpu.VMEM((GLA_QK, GLA_DV), F32)],
        compiler_params=_cparams("parallel", "parallel", "arbitrary"),
        name="gla_scan",
    )(gla_arr, gz_arr, w2p, bias, e2, m2, s0)


def _lru_kernel(x_ref, pv_ref, nx_ref, cw_ref, cb_ref, wg_ref, bg_ref, lam_ref, h0_ref, o_ref, hfin_ref, carry, *, nblk):
    d = pl.program_id(1)
    i = pl.program_id(2)
    T = LRU_CHUNK
    W = LRU_WIDTH

    @pl.when(i == 0)
    def _():
        carry[...] = jnp.broadcast_to(h0_ref[0, 0], (SUBLANES, W))

    li = jnp.where(d == 0, i, nblk - 1 - i)
    pv = jnp.where(li > 0, pv_ref[...], 0.0)
    nx = jnp.where(li < nblk - 1, nx_ref[...], 0.0)
    xe = jnp.concatenate([pv, x_ref[...], nx], axis=0)
    n_ext = T + 2 * SUBLANES
    win = lambda off: pltpu.roll(xe, n_ext - off, 0)[0:T]
    cw = cw_ref[...]
    xc = cb_ref[...] + win(6) * cw[0:1] + win(7) * cw[1:2] + xe[SUBLANES:SUBLANES + T] * cw[2:3] + win(9) * cw[3:4]

    g = jnp.dot(xc.astype(BF16), wg_ref[0], preferred_element_type=F32) + bg_ref[0]
    r = jax.nn.sigmoid(g[:, :W])
    gi = jax.nn.sigmoid(g[:, W:])
    log_a = (-LRU_C * _softplus(-lam_ref[0])) * r
    a = jnp.exp(log_a)
    u = jnp.sqrt(-jnp.tanh(log_a) * (a * a + 1.0)) * (gi * xc)
    row = lax.broadcasted_iota(I32, (T, W), 0)

    @pl.when(d == 0)
    def _():
        aa, uu = a, u
        sh = 1
        while sh < T:
            ok = row >= sh
            uu = uu + aa * jnp.where(ok, pltpu.roll(uu, sh, 0), 0.0)
            aa = aa * jnp.where(ok, pltpu.roll(aa, sh, 0), 1.0)
            sh *= 2
        h = uu + aa * carry[0:1]
        o_ref[0] = h
        carry[...] = jnp.broadcast_to(h[T - 1:T], (SUBLANES, W))

    @pl.when(d == 1)
    def _():
        aa, uu = a, u
        sh = 1
        while sh < T:
            ok = row < T - sh
            uu = uu + aa * jnp.where(ok, pltpu.roll(uu, T - sh, 0), 0.0)
            aa = aa * jnp.where(ok, pltpu.roll(aa, T - sh, 0), 1.0)
            sh *= 2
        h = uu + aa * carry[0:1]
        o_ref[0] = h
        carry[...] = jnp.broadcast_to(h[0:1], (SUBLANES, W))

    @pl.when(i == nblk - 1)
    def _():
        hfin_ref[0, 0] = carry[0:1]


def _lru(lru_arr, cw, cb, wg, bg, lam, h0, B, T):
    nblk = T // LRU_CHUNK
    per8 = LRU_CHUNK // SUBLANES
    n8 = B * T // SUBLANES
    blk = lambda b, d, i: b * nblk + jnp.where(d == 0, i, nblk - 1 - i)
    dirw = lambda b, d, i: (d, 0, 0)
    st = lambda b, d, i: (b, d, 0, 0)
    return pl.pallas_call(
        functools.partial(_lru_kernel, nblk=nblk),
        grid=(B, 2, nblk),
        in_specs=[
            pl.BlockSpec((LRU_CHUNK, LRU_WIDTH), lambda b, d, i: (blk(b, d, i), 0)),
            pl.BlockSpec((SUBLANES, LRU_WIDTH), lambda b, d, i: (jnp.maximum(blk(b, d, i) * per8 - 1, 0), 0)),
            pl.BlockSpec((SUBLANES, LRU_WIDTH), lambda b, d, i: (jnp.minimum((blk(b, d, i) + 1) * per8, n8 - 1), 0)),
            pl.BlockSpec((SUBLANES, LRU_WIDTH), lambda b, d, i: (0, 0)),
            pl.BlockSpec((1, LRU_WIDTH), lambda b, d, i: (0, 0)),
            pl.BlockSpec((1, LRU_WIDTH, 2 * LRU_WIDTH), dirw),
            pl.BlockSpec((1, 1, 2 * LRU_WIDTH), dirw),
            pl.BlockSpec((1, 1, LRU_WIDTH), dirw),
            pl.BlockSpec((1, 1, 1, LRU_WIDTH), st),
        ],
        out_specs=[
            pl.BlockSpec((1, LRU_CHUNK, LRU_WIDTH), lambda b, d, i: (d, blk(b, d, i), 0)),
            pl.BlockSpec((1, 1, 1, LRU_WIDTH), st),
        ],
        out_shape=[
            jax.ShapeDtypeStruct((2, B * T, LRU_WIDTH), F32),
            jax.ShapeDtypeStruct((B, 2, 1, LRU_WIDTH), F32),
        ],
        scratch_shapes=[pltpu.VMEM((SUBLANES, LRU_WIDTH), F32)],
        compiler_params=_cparams("parallel", "parallel", "arbitrary"),
        name="lru_scan",
    )(lru_arr, lru_arr, lru_arr, cw, cb, wg, bg, lam, h0)


def _out_kernel(h_ref, attn_ref, go_ref, gr_ref, gg_ref, lh_ref, lg_ref, w_ref, g1_ref, a2_ref, s2_ref, wr_ref, br_ref,
                vt_ref, lt_ref, hn_ref, v_ref, lgt_ref, *, n_main):
    i = pl.program_id(0)

    @pl.when(i < n_main)
    def _():
        _out_body(h_ref, attn_ref, go_ref, gr_ref, gg_ref, lh_ref, lg_ref, w_ref, g1_ref, a2_ref, s2_ref, wr_ref, br_ref,
                  hn_ref, v_ref, lgt_ref)

    @pl.when(i >= n_main)
    def _():
        v_ref[...] = vt_ref[...]
        lgt_ref[...] = lt_ref[...]


def _out_body(h_ref, attn_ref, go_ref, gr_ref, gg_ref, lh_ref, lg_ref, w_ref, g1_ref, a2_ref, s2_ref, wr_ref, br_ref,
              hn_ref, v_ref, lgt_ref):
    o = go_ref[0] + go_ref[1]
    gr = gr_ref[...]
    gate = gr * jax.nn.sigmoid(gr)
    parts = []
    for hh in range(GLA_HEADS):
        oh = o[:, hh * GLA_DV:(hh + 1) * GLA_DV]
        y = (oh * lax.rsqrt(jnp.mean(oh * oh, axis=-1, keepdims=True) + RMS_EPS)) * gg_ref[...]
        parts.append(y * gate[:, hh * GLA_DV:(hh + 1) * GLA_DV])
    gla = jnp.concatenate(parts, axis=1).astype(BF16)
    lg = lg_ref[...]
    gelu = lg * (0.5 * (1.0 + jnp.tanh(np.sqrt(2.0 / np.pi).astype(np.float32) * (lg + 0.044715 * (lg * lg * lg)))))
    lru = ((lh_ref[0] + lh_ref[1]) * gelu).astype(BF16)
    y = jnp.dot(attn_ref[...], w_ref[0:ATTN_Q], preferred_element_type=F32)
    y = y + jnp.dot(gla, w_ref[ATTN_Q:ATTN_Q + GLA_V], preferred_element_type=F32)
    y = y + jnp.dot(lru, w_ref[ATTN_Q + GLA_V:D_MIX], preferred_element_type=F32)
    hn = h_ref[...] + g1_ref[0] * y
    hn_ref[...] = hn
    v = (hn * lax.rsqrt(jnp.mean(hn * hn, axis=-1, keepdims=True) + RMS_EPS)) * a2_ref[0] + s2_ref[0]
    _store_token_tiles(v_ref, _pack_bf16_pairs(v))
    v_hi = v.astype(BF16)
    v_lo = (v - v_hi.astype(F32)).astype(BF16)
    t = jnp.dot(v_hi, wr_ref[...], preferred_element_type=F32)
    t_lo = jnp.dot(v_lo, wr_ref[:, 0:LANES], preferred_element_type=F32)
    lgt_ref[...] = (t[:, :LANES] + t[:, LANES:]) + t_lo + br_ref[...]


def _mixer_out(h, attn, go, gla_arr, gg, lh, lru_arr, w_out, g1, a2, s2, wr, br, seq_len, tail):
    rows, D = h.shape
    tm = _tile(seq_len, 256)
    per = seq_len // tm
    n_main = rows // tm
    tw = tm * (D // 2 // LANES)
    if tail is None:
        tail = (jnp.zeros((tw, LANES), I32), jnp.zeros((tm, LANES), F32))
        n_tail = 0
    else:
        assert tail[1].shape[0] % tm == 0
        n_tail = tail[1].shape[0] // tm
    n_tok = rows + n_tail * tm
    main = lambda i: jnp.minimum(i, n_main - 1)
    row = lambda i: (main(i), 0)
    row3 = lambda i: (0, main(i), 0)
    bat = lambda i: (main(i) // per, 0, 0)
    const = lambda i: (0, 0)
    tok = lambda i: (i, 0)
    trow = lambda i: (jnp.maximum(i - n_main, 0), 0)
    in_specs = [
        pl.BlockSpec((tm, D), row),
        pl.BlockSpec((tm, ATTN_Q), row),
        pl.BlockSpec((2, tm, GLA_V), row3),
        pl.BlockSpec((tm, GLA_V), lambda i: (main(i), 2)),
        pl.BlockSpec((1, GLA_DV), const),
        pl.BlockSpec((2, tm, LRU_WIDTH), row3),
        pl.BlockSpec((tm, LRU_WIDTH), lambda i: (main(i), 1)),
        pl.BlockSpec((D_MIX, D), const, pipeline_mode=pl.Buffered(1)),
        pl.BlockSpec((1, 1, D), bat),
        pl.BlockSpec((1, 1, D), bat),
        pl.BlockSpec((1, 1, D), bat),
        pl.BlockSpec((D, 2 * LANES), const),
        pl.BlockSpec((1, LANES), const),
        pl.BlockSpec((tw, LANES), trow),
        pl.BlockSpec((tm, LANES), trow),
    ]
    args = [h, attn, go, gla_arr, gg, lh, lru_arr, w_out, g1, a2, s2, wr, br, tail[0], tail[1]]
    return pl.pallas_call(
        functools.partial(_out_kernel, n_main=n_main),
        grid=(n_main + n_tail,),
        in_specs=in_specs,
        out_specs=[
            pl.BlockSpec((tm, D), row),
            pl.BlockSpec((tw, LANES), tok),
            pl.BlockSpec((tm, LANES), tok),
        ],
        out_shape=[
            jax.ShapeDtypeStruct((rows, D), F32),
            jax.ShapeDtypeStruct((n_tok * (tw // tm), LANES), I32),
            jax.ShapeDtypeStruct((n_tok, LANES), F32),
        ],
        compiler_params=_cparams("arbitrary"),
        name="mixer_out",
    )(*args)


def _route_kernel(lg_ref, oi_ref, of_ref, cnt_ref, carry):
    i = pl.program_id(0)

    @pl.when(i == 0)
    def _():
        carry[...] = jnp.zeros_like(carry)

    lg = lg_ref[...]
    tm = lg.shape[0]
    col = lax.broadcasted_iota(I32, lg.shape, 1)
    colf = col.astype(F32)
    big = float(LANES)
    is_g = col < N_GROUPS
    gm = jnp.max(jnp.where(is_g, lg, -jnp.inf), axis=-1, keepdims=True)
    eg = jnp.where(is_g, jnp.exp(lg - gm), 0.0)
    pg = eg / jnp.sum(eg, axis=-1, keepdims=True)
    p_grp = jnp.max(pg, axis=-1, keepdims=True)
    grp = jnp.min(jnp.where(is_g & (pg == p_grp), colf, big), axis=-1, keepdims=True).astype(I32)

    sel = (col >= N_GROUPS) & (col < N_GROUPS + N_EXPERTS) & (((col - N_GROUPS) >> 3) == grp)
    em = jnp.max(jnp.where(sel, lg, -jnp.inf), axis=-1, keepdims=True)
    ee = jnp.where(sel, jnp.exp(lg - em), 0.0)
    pe = ee / jnp.sum(ee, axis=-1, keepdims=True)
    p1 = jnp.max(jnp.where(sel, pe, -1.0), axis=-1, keepdims=True)
    c1 = jnp.min(jnp.where(sel & (pe == p1), colf, big), axis=-1, keepdims=True).astype(I32)
    rest = sel & (col != c1)
    p2 = jnp.max(jnp.where(rest, pe, -1.0), axis=-1, keepdims=True)
    c2 = jnp.min(jnp.where(rest & (pe == p2), colf, big), axis=-1, keepdims=True).astype(I32)
    e1 = c1 - N_GROUPS
    e2 = c2 - N_GROUPS
    den = p1 + p2
    g1 = p_grp * (p1 / den)
    g2 = p_grp * (p2 / den)

    hit1 = col == e1
    hit2 = col == e2
    oh = jnp.where(hit1 | hit2, 1.0, 0.0)
    r_i = lax.broadcasted_iota(I32, (tm, tm), 0)
    c_i = lax.broadcasted_iota(I32, (tm, tm), 1)
    tri = jnp.where(c_i < r_i, 1.0, 0.0).astype(BF16)
    before = jnp.dot(tri, oh.astype(BF16), preferred_element_type=F32) + carry[0:1]
    r1 = jnp.sum(jnp.where(hit1, before, 0.0), axis=-1, keepdims=True).astype(I32)
    r2 = jnp.sum(jnp.where(hit2, before, 0.0), axis=-1, keepdims=True).astype(I32)
    new = carry[0:1] + jnp.sum(oh, axis=0, keepdims=True)
    carry[...] = jnp.broadcast_to(new, carry.shape)
    cnt_ref[...] = jnp.broadcast_to(new, cnt_ref.shape)

    zero = jnp.zeros_like(col)
    oi_ref[...] = jnp.where(col == 0, e1, jnp.where(col == 1, e2, jnp.where(col == 2, r1, jnp.where(col == 3, r2, zero))))
    of_ref[...] = jnp.where(col == 0, g1, jnp.where(col == 1, g2, 0.0))


def _route(logits):
    n_tok = logits.shape[0]
    tm = _tile(n_tok, 256)
    row = lambda i: (i, 0)
    return pl.pallas_call(
        _route_kernel,
        grid=(n_tok // tm,),
        in_specs=[pl.BlockSpec((tm, LANES), row)],
        out_specs=[
            pl.BlockSpec((tm, LANES), row),
            pl.BlockSpec((tm, LANES), row),
            pl.BlockSpec((SUBLANES, LANES), lambda i: (0, 0)),
        ],
        out_shape=[
            jax.ShapeDtypeStruct((n_tok, LANES), I32),
            jax.ShapeDtypeStruct((n_tok, LANES), F32),
            jax.ShapeDtypeStruct((SUBLANES, LANES), F32),
        ],
        scratch_shapes=[pltpu.VMEM((SUBLANES, LANES), F32)],
        compiler_params=_cparams("arbitrary"),
        name="route",
    )(logits)


def _dispatch_kernel(dest_ref, v_ref, xs_in_ref, xs_ref, sem, *, sub):
    del xs_in_ref
    rows = v_ref.shape[0]

    def start(j, c):
        src = pl.multiple_of((j >> 1) * sub, sub)
        dst = pl.multiple_of(dest_ref[0, 0, j] * sub, sub)
        pltpu.make_async_copy(v_ref.at[pl.ds(src, sub)], xs_ref.at[pl.ds(dst, sub)], sem).start()
        return c

    lax.fori_loop(0, (rows // sub) * TOP_K, start, 0, unroll=8)
    for _ in range(TOP_K):
        pltpu.make_async_copy(v_ref, xs_ref.at[pl.ds(0, rows)], sem).wait()


def _dispatch(v_packed, dest, xs_zero, n_tok):
    sub = v_packed.shape[0] // n_tok
    tm = _tile(n_tok, 256)
    nblk = n_tok // tm
    return pl.pallas_call(
        functools.partial(_dispatch_kernel, sub=sub),
        grid=(nblk,),
        in_specs=[
            pl.BlockSpec((1, 1, tm * TOP_K), lambda i: (i, 0, 0), memory_space=pltpu.SMEM),
            pl.BlockSpec((tm * sub, LANES), lambda i: (i, 0)),
            pl.BlockSpec(memory_space=pl.ANY),
        ],
        out_specs=pl.BlockSpec(memory_space=pl.ANY),
        out_shape=jax.ShapeDtypeStruct(xs_zero.shape, I32),
        scratch_shapes=[pltpu.SemaphoreType.DMA(())],
        input_output_aliases={2: 0},
        compiler_params=pltpu.CompilerParams(dimension_semantics=("arbitrary",), vmem_limit_bytes=VMEM_LIMIT_BYTES,
                                             has_side_effects=True),
        name="dispatch",
    )(dest.reshape(nblk, 1, tm * TOP_K), v_packed, xs_zero)


def _cast_rows(w_ref, dst):
    step = _tile(dst.shape[0], 256)

    def body(r, c):
        rows = pl.ds(pl.multiple_of(r * step, step), step)
        dst[rows, :] = w_ref[0, 0, rows, :].astype(BF16)
        return c

    lax.fori_loop(0, dst.shape[0] // step, body, 0)


def _expert_changed(be_ref, i):
    prev = be_ref[jnp.maximum(i - 1, 0)]
    return (i == 0) | (be_ref[i] != prev)


def _moe_up_kernel(be_ref, nu_ref, x_ref, w1_ref, w3_ref, o_ref, w1s, w3s):
    i = pl.program_id(0)

    @pl.when(i < nu_ref[0])
    def _():
        @pl.when(_expert_changed(be_ref, i))
        def _():
            _cast_rows(w1_ref, w1s)
            _cast_rows(w3_ref, w3s)

        xb = _unpack_bf16_pairs(_load_token_tiles(x_ref, ROW_BLOCK)).astype(BF16)
        a = jnp.dot(xb, w1s[...], preferred_element_type=F32)
        b = jnp.dot(xb, w3s[...], preferred_element_type=F32)
        o_ref[...] = ((a * jax.nn.sigmoid(a)) * b).astype(BF16)

    @pl.when(i >= nu_ref[0])
    def _():
        o_ref[...] = jnp.zeros_like(o_ref)


def _moe_down_kernel(be_ref, nu_ref, h_ref, w2_ref, o_ref, w2s):
    i = pl.program_id(0)

    @pl.when(i < nu_ref[0])
    def _():
        @pl.when(_expert_changed(be_ref, i))
        def _():
            _cast_rows(w2_ref, w2s)

        _store_token_tiles(o_ref, _pack_bf16_pairs(jnp.dot(h_ref[...], w2s[...], preferred_element_type=F32)))

    @pl.when(i >= nu_ref[0])
    def _():
        o_ref[...] = jnp.zeros_like(o_ref)


def _moe_experts(xs, blk_e, n_used, w1, w3, w2, layer):
    D, FF = w1.shape[-2:]
    sub = D // 2 // LANES
    P = xs.shape[0] // sub
    nb = P // ROW_BLOCK
    tw = ROW_BLOCK * sub
    rowi = lambda i, be, nu: (jnp.minimum(i, nu[0] - 1), 0)
    rowo = lambda i, be, nu: (i, 0)
    wi = lambda i, be, nu: (layer, be[jnp.minimum(i, nu[0] - 1)], 0, 0)
    h1 = pl.pallas_call(
        _moe_up_kernel,
        grid_spec=pltpu.PrefetchScalarGridSpec(
            num_scalar_prefetch=2,
            grid=(nb,),
            in_specs=[
                pl.BlockSpec((tw, LANES), rowi),
                pl.BlockSpec((1, 1, D, FF), wi),
                pl.BlockSpec((1, 1, D, FF), wi),
            ],
            out_specs=pl.BlockSpec((ROW_BLOCK, FF), rowo),
            scratch_shapes=[pltpu.VMEM((D, FF), BF16), pltpu.VMEM((D, FF), BF16)],
        ),
        out_shape=jax.ShapeDtypeStruct((P, FF), BF16),
        compiler_params=_cparams("arbitrary"),
        name="moe_up",
    )(blk_e, n_used, xs, w1, w3)
    return pl.pallas_call(
        _moe_down_kernel,
        grid_spec=pltpu.PrefetchScalarGridSpec(
            num_scalar_prefetch=2,
            grid=(nb,),
            in_specs=[
                pl.BlockSpec((ROW_BLOCK, FF), rowi),
                pl.BlockSpec((1, 1, FF, D), wi),
            ],
            out_specs=pl.BlockSpec((tw, LANES), rowo),
            scratch_shapes=[pltpu.VMEM((FF, D), BF16)],
        ),
        out_shape=jax.ShapeDtypeStruct((P * sub, LANES), I32),
        compiler_params=_cparams("arbitrary"),
        name="moe_down",
    )(blk_e, n_used, h1, w2)


def _combine_kernel(dest_ref, h_ref, gate_ref, g2_ref, fg_ref, ys_ref, o_ref, buf, sem, *, final):
    tm = h_ref.shape[0]
    rows = buf.shape[1]
    sub = rows // tm

    def start(j, c):
        src = pl.multiple_of(dest_ref[0, 0, j] * sub, sub)
        dst = pl.multiple_of((j >> 1) * sub, sub)
        pltpu.make_async_copy(ys_ref.at[pl.ds(src, sub)], buf.at[j & 1, pl.ds(dst, sub)], sem).start()
        return c

    lax.fori_loop(0, tm * TOP_K, start, 0, unroll=8)
    for kk in range(TOP_K):
        pltpu.make_async_copy(ys_ref.at[pl.ds(0, rows)], buf.at[kk], sem).wait()
    gate = gate_ref[...]
    f = (_unpack_bf16_pairs(_load_token_tiles(buf.at[0], tm)) * gate[:, 0:1]
         + _unpack_bf16_pairs(_load_token_tiles(buf.at[1], tm)) * gate[:, 1:2])
    hn = h_ref[...] + g2_ref[0] * f
    if final:
        hn = (hn * lax.rsqrt(jnp.mean(hn * hn, axis=-1, keepdims=True) + RMS_EPS)) * fg_ref[...]
    o_ref[...] = hn


def _combine(h, dest, gates, g2, fg, ys, seq_len, tok_off, final):
    rows, D = h.shape
    tm = _tile(seq_len, 256)
    per = seq_len // tm
    off = tok_off // tm
    n_blk_all = dest.shape[0] // (tm * TOP_K)
    return pl.pallas_call(
        functools.partial(_combine_kernel, final=final),
        grid=(rows // tm,),
        in_specs=[
            pl.BlockSpec((1, 1, tm * TOP_K), lambda i: (i + off, 0, 0), memory_space=pltpu.SMEM),
            pl.BlockSpec((tm, D), lambda i: (i, 0)),
            pl.BlockSpec((tm, LANES), lambda i: (i + off, 0)),
            pl.BlockSpec((1, 1, D), lambda i: (i // per, 0, 0)),
            pl.BlockSpec((1, D), lambda i: (0, 0)),
            pl.BlockSpec(memory_space=pl.ANY),
        ],
        out_specs=pl.BlockSpec((tm, D), lambda i: (i, 0)),
        out_shape=jax.ShapeDtypeStruct((rows, D), F32),
        scratch_shapes=[pltpu.VMEM((TOP_K, tm * (D // 2 // LANES), LANES), I32), pltpu.SemaphoreType.DMA(())],
        compiler_params=_cparams("arbitrary"),
        name="combine",
    )(dest.reshape(n_blk_all, 1, tm * TOP_K), h, gates, g2, fg, ys)


def _rope_tables(S):
    rows = S // GRID_W
    row = jnp.repeat(jnp.arange(rows, dtype=F32), GRID_W)
    col = jnp.tile(jnp.arange(GRID_W, dtype=F32), rows)
    n_freq = HEAD_DIM // 4
    inv = ROPE_THETA ** (-jnp.arange(n_freq, dtype=F32) / n_freq)
    ang = jnp.concatenate([row[:, None] * inv, col[:, None] * inv], axis=-1)
    cos = jnp.repeat(jnp.cos(ang), 2, axis=1)
    sin = jnp.repeat(jnp.sin(ang), 2, axis=1)
    even = (jnp.arange(HEAD_DIM) % 2) == 0
    return cos, jnp.where(even, -sin, 0.0), jnp.where(even, 0.0, sin)


def _block_diag(w):
    n, c, _ = w.shape
    eye = jnp.eye(n, dtype=w.dtype)
    return (eye[:, None, :, None] * w[:, :, None, :]).reshape(n * c, n * c)


def _slot_plan(ids, counts_row, n_tok):
    e = ids[:, 0:TOP_K]
    rank = ids[:, TOP_K:2 * TOP_K]
    counts = counts_row[:N_EXPERTS].astype(I32)
    padded = (counts + ROW_BLOCK - 1) // ROW_BLOCK * ROW_BLOCK
    pad_end = jnp.cumsum(padded)
    pad_start = pad_end - padded
    dest = (pad_start[e] + rank).reshape(n_tok * TOP_K)
    n_blocks = -(-(n_tok * TOP_K + N_EXPERTS * (ROW_BLOCK - 1)) // ROW_BLOCK)
    starts = jnp.arange(n_blocks, dtype=I32) * ROW_BLOCK
    blk_e = jnp.minimum(jnp.sum(pad_end[None, :] <= starts[:, None], axis=1), N_EXPERTS - 1).astype(I32)
    n_used = (pad_end[-1] // ROW_BLOCK).astype(I32).reshape(1)
    return dest, blk_e, n_used, n_blocks


def kernel(x, c, ctx, c_ctx, ada_w, ada_b, norm_mix_g, norm_ffn_g, w_in, attn_sink, gla_gate_w2, gla_gate_b, gla_norm_g, lru_conv_w, lru_conv_b, lru_wa, lru_ba, lru_wx, lru_bx, lru_lambda, w_out, router_g_w, router_g_b, router_e_w, router_e_b, moe_w1, moe_w3, moe_w2, final_norm_g):
    B, S, D = x.shape
    C = ctx.shape[1]
    L = ada_w.shape[0]
    assert S % max(ATTN_BLOCK, GLA_CHUNK, LRU_CHUNK, GRID_W) == 0 and C % max(ATTN_BLOCK, GLA_CHUNK, LRU_CHUNK) == 0
    assert B + 1 <= SUBLANES and D % (2 * LANES) == 0

    cond = jnp.concatenate([c, c_ctx[None], jnp.zeros((SUBLANES - B - 1, D), F32)], axis=0)
    mods = _ada_mods(cond, ada_w, ada_b).reshape(L, SUBLANES, 6, D)

    cos_l, se_l, so_l = _rope_tables(S)
    cos_c = jnp.ones((C, HEAD_DIM), F32)
    zero_c = jnp.zeros((C, HEAD_DIM), F32)
    e2_np, m2_np = _gla_constants()
    gla_e = jnp.asarray(e2_np, BF16)
    gla_m = jnp.asarray(m2_np, F32)

    h_lat = x.reshape(B * S, D)
    h_ctx = ctx.reshape(B * C, D)
    out = None
    for l in range(L):
        last = l == L - 1
        ml = mods[l, :B]
        mc = jnp.broadcast_to(mods[l, B][None], (B, 6, D))
        per_b = lambda m, j: m[:, j][:, None, :]
        a1_l, a1_c = [(1.0 + per_b(m, 1)) * norm_mix_g[l] for m in (ml, mc)]
        a2_l, a2_c = [(1.0 + per_b(m, 4)) * norm_ffn_g[l] for m in (ml, mc)]

        w = w_in[l]
        c_gz = ATTN_Q + 2 * ATTN_KV + 2 * GLA_QK + 2 * GLA_V
        w_packed = jnp.concatenate(
            [w[:, :c_gz], w[:, c_gz:c_gz + 2 * GLA_RANK], jnp.zeros((D, GZ_W - 2 * GLA_RANK), F32), w[:, c_gz + 2 * GLA_RANK:]],
            axis=1).astype(BF16)
        qkv_l, gla_l, gz_l, lru_l = _in_proj(h_lat, a1_l, per_b(ml, 0), cos_l, se_l, so_l, w_packed, S)
        qkv_c, gla_c, gz_c, lru_c = _in_proj(h_ctx, a1_c, per_b(mc, 0), cos_c, zero_c, zero_c, w_packed, C)

        sink_b = jnp.broadcast_to(attn_sink[l][:, None], (N_Q_HEADS, LANES)).astype(F32)
        attn_l = _attn_latent(qkv_l, qkv_c, sink_b, B, S, C)

        w2p = jnp.zeros((2, GZ_W, GLA_QK), F32)
        w2p = w2p.at[0, :GLA_RANK].set(gla_gate_w2[l, 0]).at[1, GLA_RANK:2 * GLA_RANK].set(gla_gate_w2[l, 1]).astype(BF16)
        gbias = gla_gate_b[l].reshape(2, 1, GLA_QK)
        s_zero = jnp.zeros((B, 2, GLA_QK, GLA_DV), F32)
        go_c, s_ctx = _gla(gla_c, gz_c, w2p, gbias, gla_e, gla_m, s_zero, B, C)
        go_l, _ = _gla(gla_l, gz_l, w2p, gbias, gla_e, gla_m, s_ctx, B, S)

        cw = jnp.concatenate([lru_conv_w[l], jnp.zeros((SUBLANES - CONV_W, LRU_WIDTH), F32)], axis=0)
        cb = lru_conv_b[l].reshape(1, LRU_WIDTH)
        wg = jnp.stack([jnp.concatenate([_block_diag(lru_wa[l, d]), _block_diag(lru_wx[l, d])], axis=1) for d in range(2)]).astype(BF16)
        bg = jnp.concatenate([lru_ba[l], lru_bx[l]], axis=1).reshape(2, 1, 2 * LRU_WIDTH)
        lam = lru_lambda[l].reshape(2, 1, LRU_WIDTH)
        h_zero = jnp.zeros((B, 2, 1, LRU_WIDTH), F32)
        lh_c, hs_ctx = _lru(lru_c, cw, cb, wg, bg, lam, h_zero, B, C)
        lh_l, _ = _lru(lru_l, cw, cb, wg, bg, lam, hs_ctx, B, S)

        wo = w_out[l].astype(BF16)
        gg = gla_norm_g[l].reshape(1, GLA_DV)
        wr = jnp.concatenate([router_g_w[l], router_e_w[l], jnp.zeros((D, LANES - N_GROUPS - N_EXPERTS), F32)], axis=1)
        wr_hi = wr.astype(BF16)
        wr = jnp.concatenate([wr_hi, (wr - wr_hi.astype(F32)).astype(BF16)], axis=1)
        br =jnp.concatenate([router_g_b[l], router_e_b[l], jnp.zeros((LANES - N_GROUPS - N_EXPERTS,), F32)]).reshape(1, LANES)
        n_tok = B * S if last else B * (S + C)
        tail = None
        if not last:
            attn_c = _attn_context(qkv_c, sink_b, B, C)
            h_ctx, v_c, lgt_c = _mixer_out(h_ctx, attn_c, go_c, gla_c, gg, lh_c, lru_c, wo, per_b(mc, 2), a2_c, per_b(mc, 3),
                                           wr, br, C, None)
            tail = (v_c, lgt_c)
        h_lat, v_all, lgt_all = _mixer_out(h_lat, attn_l, go_l, gla_l, gg, lh_l, lru_l, wo, per_b(ml, 2), a2_l, per_b(ml, 3),
                                           wr, br, S, tail)

        ids, gates, counts = _route(lgt_all)
        dest, blk_e, n_used, n_blocks = _slot_plan(ids, counts[0], n_tok)
        xs = _dispatch(v_all, dest, jnp.zeros((n_blocks * ROW_BLOCK * (D // 2 // LANES), LANES), I32), n_tok)
        ys = _moe_experts(xs, blk_e, n_used, moe_w1, moe_w3, moe_w2, l)
        fg = final_norm_g.reshape(1, D)
        h_lat = _combine(h_lat, dest, gates, per_b(ml, 5), fg, ys, S, 0, last)
        if not last:
            h_ctx = _combine(h_ctx, dest, gates, per_b(mc, 5), fg, ys, C, B * S, False)
        out = h_lat
    return out.reshape(B, S, D)
```

```python
import functools

import numpy as np
import jax
import jax.numpy as jnp
from jax import lax
from jax.experimental import pallas as pl
from jax.experimental.pallas import tpu as pltpu

F32 = jnp.float32
BF16 = jnp.bfloat16
I32 = jnp.int32

GRID_W = 64
RMS_EPS = 1e-6
N_Q_HEADS = 8
N_KV_HEADS = 2
HEAD_DIM = 128
WINDOW = 128
ATTN_BLOCK = 128
ROPE_THETA = 10000.0
GLA_HEADS = 4
GLA_DK = 64
GLA_DV = 128
GLA_RANK = 16
GLA_TAU = 16.0
LRU_WIDTH = 512
LRU_BLOCKS = 8
LRU_C = 8.0
CONV_W = 4
N_GROUPS = 4
EXPERTS_PER_GROUP = 8
N_EXPERTS = 32
TOP_K = 2
ATTN_Q = N_Q_HEADS * HEAD_DIM
ATTN_KV = N_KV_HEADS * HEAD_DIM
GLA_QK = GLA_HEADS * GLA_DK
GLA_V = GLA_HEADS * GLA_DV
D_MIX = ATTN_Q + GLA_V + LRU_WIDTH

LANES = 128
SUBLANES = 8
VMEM_LIMIT_BYTES = 56 * 1024 * 1024

QKV_W = ATTN_Q + 2 * ATTN_KV
GLA_W = 2 * GLA_QK + 2 * GLA_V
GZ_W = LANES
LRU_W = 2 * LRU_WIDTH
COL_GLA = QKV_W
COL_GZ = COL_GLA + GLA_W
COL_LRU = COL_GZ + GZ_W
W_IN_PACKED = COL_LRU + LRU_W

GLA_CHUNK = 128
GLA_LEVELS = 7
LRU_CHUNK = 128
ROW_BLOCK = 256


def _cparams(*sem):
    return pltpu.CompilerParams(dimension_semantics=sem, vmem_limit_bytes=VMEM_LIMIT_BYTES)


def _tile(n, pref):
    t = min(n, pref)
    while n % t:
        t -= SUBLANES
    return t


def _softplus(x):
    return jnp.maximum(x, 0.0) + jnp.log1p(jnp.exp(-jnp.abs(x)))


def _pack_bf16_pairs(x):
    n = x.shape[1] // 2
    lo = lax.bitcast_convert_type(x[:, :n].astype(BF16).astype(F32), I32)
    hi = lax.bitcast_convert_type(x[:, n:].astype(BF16).astype(F32), I32)
    return hi | lax.shift_right_logical(lo, jnp.full(lo.shape, 16, I32))


def _unpack_bf16_pairs(p):
    lo = lax.bitcast_convert_type(lax.shift_left(p, jnp.full(p.shape, 16, I32)), F32)
    hi = lax.bitcast_convert_type(p & jnp.int32(-65536), F32)
    return jnp.concatenate([lo, hi], axis=1)


def _store_token_tiles(ref, packed):
    m = packed.shape[0]
    sub = packed.shape[1] // LANES
    for s in range(sub):
        ref[pl.ds(s, m, stride=sub), :] = packed[:, s * LANES:(s + 1) * LANES]


def _load_token_tiles(ref, m):
    sub = ref.shape[0] // m
    return jnp.concatenate([ref[pl.ds(s, m, stride=sub), :] for s in range(sub)], axis=1)


def _ada_kernel(c_ref, w_ref, b_ref, o_ref):
    c = c_ref[...]
    s = c * jax.nn.sigmoid(c)
    o_ref[0] = jnp.dot(s.astype(BF16), w_ref[0].astype(BF16), preferred_element_type=F32) + b_ref[0]


def _ada_mods(cond, ada_w, ada_b):
    L, D, D6 = ada_w.shape
    tn = _tile(D6, 1536)
    while tn % LANES:
        tn -= SUBLANES
    return pl.pallas_call(
        _ada_kernel,
        grid=(L, D6 // tn),
        in_specs=[
            pl.BlockSpec((SUBLANES, D), lambda l, j: (0, 0)),
            pl.BlockSpec((1, D, tn), lambda l, j: (l, 0, j)),
            pl.BlockSpec((1, 1, tn), lambda l, j: (l, 0, j)),
        ],
        out_specs=pl.BlockSpec((1, SUBLANES, tn), lambda l, j: (l, 0, j)),
        out_shape=jax.ShapeDtypeStruct((L, SUBLANES, D6), F32),
        compiler_params=_cparams("parallel", "parallel"),
        name="ada_mods",
    )(cond, ada_w, ada_b.reshape(L, 1, D6))


def _in_kernel(x_ref, a_ref, s_ref, cos_ref, se_ref, so_ref, w_ref, qkv_ref, gla_ref, gz_ref, lru_ref):
    x = x_ref[...]
    ms = jnp.mean(x * x, axis=-1, keepdims=True)
    u = (x * lax.rsqrt(ms + RMS_EPS)) * a_ref[0] + s_ref[0]
    ub = u.astype(BF16)
    cos, se, so = cos_ref[...], se_ref[...], so_ref[...]
    n_rot = N_Q_HEADS + N_KV_HEADS
    for j in range(n_rot):
        zh = jnp.dot(ub, w_ref[:, j * HEAD_DIM:(j + 1) * HEAD_DIM], preferred_element_type=F32)
        rot = zh * cos + pltpu.roll(zh, HEAD_DIM - 1, 1) * se + pltpu.roll(zh, 1, 1) * so
        qkv_ref[:, j * HEAD_DIM:(j + 1) * HEAD_DIM] = rot.astype(BF16)
    c0 = n_rot * HEAD_DIM
    qkv_ref[:, c0:QKV_W] = jnp.dot(ub, w_ref[:, c0:QKV_W], preferred_element_type=F32).astype(BF16)
    gla_ref[...] = jnp.dot(ub, w_ref[:, COL_GLA:COL_GZ], preferred_element_type=F32)
    gz_ref[...] = jnp.dot(ub, w_ref[:, COL_GZ:COL_LRU], preferred_element_type=F32)
    lru_ref[...] = jnp.dot(ub, w_ref[:, COL_LRU:W_IN_PACKED], preferred_element_type=F32)


def _in_proj(h, a, s, cos, se, so, w, seq_len):
    rows, D = h.shape
    tm = _tile(seq_len, 512)
    per = seq_len // tm
    row = lambda i: (i, 0)
    bat = lambda i: (i // per, 0, 0)
    tab = lambda i: (i % per, 0)
    return pl.pallas_call(
        _in_kernel,
        grid=(rows // tm,),
        in_specs=[
            pl.BlockSpec((tm, D), row),
            pl.BlockSpec((1, 1, D), bat),
            pl.BlockSpec((1, 1, D), bat),
            pl.BlockSpec((tm, HEAD_DIM), tab),
            pl.BlockSpec((tm, HEAD_DIM), tab),
            pl.BlockSpec((tm, HEAD_DIM), tab),
            pl.BlockSpec((D, W_IN_PACKED), lambda i: (0, 0), pipeline_mode=pl.Buffered(1)),
        ],
        out_specs=[
            pl.BlockSpec((tm, QKV_W), row),
            pl.BlockSpec((tm, GLA_W), row),
            pl.BlockSpec((tm, GZ_W), row),
            pl.BlockSpec((tm, LRU_W), row),
        ],
        out_shape=[
            jax.ShapeDtypeStruct((rows, QKV_W), BF16),
            jax.ShapeDtypeStruct((rows, GLA_W), F32),
            jax.ShapeDtypeStruct((rows, GZ_W), F32),
            jax.ShapeDtypeStruct((rows, LRU_W), F32),
        ],
        compiler_params=_cparams("parallel"),
        name="in_proj",
    )(h, a, s, cos, se, so, w)


def _attn_heads(q, k_loc, v_loc, valid, k_ctx, v_ctx, sink_ref, o_ref):
    scale = HEAD_DIM ** -0.5
    group = N_Q_HEADS // N_KV_HEADS
    nt = (((1,), (1,)), ((), ()))
    for head in range(N_Q_HEADS):
        hk = head // group
        qh = q[:, head * HEAD_DIM:(head + 1) * HEAD_DIM]
        sink = sink_ref[head:head + 1, 0:1]
        s_ctx = lax.dot_general(qh, k_ctx[hk], nt, preferred_element_type=F32) * scale
        m = jnp.maximum(jnp.max(s_ctx, axis=-1, keepdims=True), sink)
        if k_loc is not None:
            s_loc = lax.dot_general(qh, k_loc[hk], nt, preferred_element_type=F32) * scale
            s_loc = jnp.where(valid, s_loc, -jnp.inf)
            m = jnp.maximum(m, jnp.max(s_loc, axis=-1, keepdims=True))
        p_ctx = jnp.exp(s_ctx - m)
        den = jnp.sum(p_ctx, axis=-1, keepdims=True) + jnp.exp(sink - m)
        o = jnp.dot(p_ctx.astype(BF16), v_ctx[hk], preferred_element_type=F32)
        if k_loc is not None:
            p_loc = jnp.exp(s_loc - m)
            den = den + jnp.sum(p_loc, axis=-1, keepdims=True)
            o = o + jnp.dot(p_loc.astype(BF16), v_loc[hk], preferred_element_type=F32)
        o_ref[:, head * HEAD_DIM:(head + 1) * HEAD_DIM] = (o / den).astype(o_ref.dtype)


def _split_kv(kv):
    ks = [kv[:, h * HEAD_DIM:(h + 1) * HEAD_DIM] for h in range(N_KV_HEADS)]
    vs = [kv[:, ATTN_KV + h * HEAD_DIM:ATTN_KV + (h + 1) * HEAD_DIM] for h in range(N_KV_HEADS)]
    return ks, vs


def _attn_lat_kernel(q_ref, kvp_ref, kvc_ref, kvn_ref, kvx_ref, sink_ref, o_ref, *, nb):
    n = pl.program_id(1)
    kp, vp = _split_kv(kvp_ref[...])
    kc, vc = _split_kv(kvc_ref[...])
    kn, vn = _split_kv(kvn_ref[...])
    k_ctx, v_ctx = _split_kv(kvx_ref[...])
    k_loc = [jnp.concatenate([kp[h], kc[h], kn[h]], axis=0) for h in range(N_KV_HEADS)]
    v_loc = [jnp.concatenate([vp[h], vc[h], vn[h]], axis=0) for h in range(N_KV_HEADS)]
    i = lax.broadcasted_iota(I32, (ATTN_BLOCK, 3 * ATTN_BLOCK), 0)
    j = lax.broadcasted_iota(I32, (ATTN_BLOCK, 3 * ATTN_BLOCK), 1)
    valid = (j >= i) & (j <= i + 2 * WINDOW)
    valid = valid & ((j >= ATTN_BLOCK) | (n > 0)) & ((j < 2 * ATTN_BLOCK) | (n < nb - 1))
    _attn_heads(q_ref[...], k_loc, v_loc, valid, k_ctx, v_ctx, sink_ref, o_ref)


def _attn_ctx_kernel(q_ref, kvx_ref, sink_ref, o_ref):
    k_ctx, v_ctx = _split_kv(kvx_ref[...])
    _attn_heads(q_ref[...], None, None, None, k_ctx, v_ctx, sink_ref, o_ref)


def _attn_latent(qkv_l, qkv_c, sink_b, B, S, C):
    nb = S // ATTN_BLOCK
    kvw = 2 * ATTN_KV
    kvcol = ATTN_Q // kvw
    return pl.pallas_call(
        functools.partial(_attn_lat_kernel, nb=nb),
        grid=(B, nb),
        in_specs=[
            pl.BlockSpec((ATTN_BLOCK, ATTN_Q), lambda b, n: (b * nb + n, 0)),
            pl.BlockSpec((ATTN_BLOCK, kvw), lambda b, n: (b * nb + jnp.maximum(n - 1, 0), kvcol)),
            pl.BlockSpec((ATTN_BLOCK, kvw), lambda b, n: (b * nb + n, kvcol)),
            pl.BlockSpec((ATTN_BLOCK, kvw), lambda b, n: (b * nb + jnp.minimum(n + 1, nb - 1), kvcol)),
            pl.BlockSpec((C, kvw), lambda b, n: (b, kvcol)),
            pl.BlockSpec((SUBLANES, LANES), lambda b, n: (0, 0)),
        ],
        out_specs=pl.BlockSpec((ATTN_BLOCK, ATTN_Q), lambda b, n: (b * nb + n, 0)),
        out_shape=jax.ShapeDtypeStruct((B * S, ATTN_Q), BF16),
        compiler_params=_cparams("parallel", "parallel"),
        name="attn_latent",
    )(qkv_l, qkv_l, qkv_l, qkv_l, qkv_c, sink_b)


def _attn_context(qkv_c, sink_b, B, C):
    nb = C // ATTN_BLOCK
    kvw = 2 * ATTN_KV
    kvcol = ATTN_Q // kvw
    return pl.pallas_call(
        _attn_ctx_kernel,
        grid=(B, nb),
        in_specs=[
            pl.BlockSpec((ATTN_BLOCK, ATTN_Q), lambda b, n: (b * nb + n, 0)),
            pl.BlockSpec((C, kvw), lambda b, n: (b, kvcol)),
            pl.BlockSpec((SUBLANES, LANES), lambda b, n: (0, 0)),
        ],
        out_specs=pl.BlockSpec((ATTN_BLOCK, ATTN_Q), lambda b, n: (b * nb + n, 0)),
        out_shape=jax.ShapeDtypeStruct((B * C, ATTN_Q), BF16),
        compiler_params=_cparams("parallel", "parallel"),
        name="attn_context",
    )(qkv_c, qkv_c, sink_b)


def _gla_constants():
    Lc = GLA_CHUNK
    e = np.zeros((GLA_LEVELS + 2, Lc, Lc), np.float32)
    msk = np.zeros((GLA_LEVELS + 1, Lc, Lc), np.float32)
    t = np.arange(Lc)
    for l in range(GLA_LEVELS):
        m = 1 << l
        blk = t // (2 * m)
        upper = (t % (2 * m)) >= m
        bnd = blk * 2 * m + m
        r = t[None, :]
        eq = upper[:, None] & (r >= bnd[:, None]) & (r <= t[:, None])
        ek = (~upper)[:, None] & (r > t[:, None]) & (r <= bnd[:, None] - 1)
        e[l] = (eq | ek).astype(np.float32)
        msk[l] = (upper[:, None] & (~upper)[None, :] & (blk[:, None] == blk[None, :])).astype(np.float32)
    e[GLA_LEVELS] = (t[None, :] <= t[:, None]).astype(np.float32)
    e[GLA_LEVELS + 1] = (t[None, :] > t[:, None]).astype(np.float32)
    msk[GLA_LEVELS] = np.eye(Lc, dtype=np.float32)
    e2 = np.stack([e, e[:, ::-1, ::-1]]).reshape(2, (GLA_LEVELS + 2) * Lc, Lc)
    m2 = np.stack([msk, msk[:, ::-1, ::-1]])
    return e2, m2


def _gla_kernel(x_ref, gz_ref, w2_ref, bias_ref, e_ref, m_ref, s0_ref, o_ref, sfin_ref, s_scr, *, nch):
    i = pl.program_id(2)
    Lc = GLA_CHUNK

    @pl.when(i == 0)
    def _():
        s_scr[...] = s0_ref[0, 0]

    z = jnp.dot(gz_ref[...].astype(BF16), w2_ref[0], preferred_element_type=F32) + bias_ref[0]
    la = (jnp.minimum(z, 0.0) - jnp.log1p(jnp.exp(-jnp.abs(z)))) * (1.0 / GLA_TAU)
    la_hi = la.astype(BF16)
    la_lo = (la - la_hi.astype(F32)).astype(BF16)
    la2 = jnp.concatenate([la_hi, la_lo], axis=1)
    ex = jnp.dot(e_ref[0], la2, preferred_element_type=F32)
    decay = jnp.exp(ex[:, :GLA_QK] + ex[:, GLA_QK:])
    tot = lax.dot_general(la2, jnp.ones((Lc, LANES), BF16), (((0,), (0,)), ((), ())), preferred_element_type=F32)
    a_tot = jnp.exp(tot[:GLA_QK] + tot[GLA_QK:])

    q = x_ref[:, 0:GLA_QK] * (GLA_DK ** -0.5)
    k = x_ref[:, GLA_QK:2 * GLA_QK]
    v = x_ref[:, 2 * GLA_QK:2 * GLA_QK + GLA_V].astype(BF16)
    lane_head = lax.broadcasted_iota(I32, (Lc, GLA_QK), 1) >> 6
    row_head = lax.broadcasted_iota(I32, (GLA_QK, GLA_DV), 0) >> 6
    nt = (((1,), (1,)), ((), ()))

    att = [jnp.zeros((Lc, Lc), F32) for _ in range(GLA_HEADS)]
    for l in range(GLA_LEVELS + 1):
        if l < GLA_LEVELS:
            dl = decay[l * Lc:(l + 1) * Lc]
            ql = (q * dl).astype(BF16)
            kl = (k * dl).astype(BF16)
        else:
            ql = q.astype(BF16)
            kl = k.astype(BF16)
        ml = m_ref[0, l]
        for h in range(GLA_HEADS):
            kh = jnp.where(lane_head == h, kl, jnp.zeros_like(kl))
            att[h] = att[h] + ml * lax.dot_general(ql, kh, nt, preferred_element_type=F32)

    s_old = s_scr[...]
    s_b = s_old.astype(BF16)
    q_in = (q * decay[GLA_LEVELS * Lc:(GLA_LEVELS + 1) * Lc]).astype(BF16)
    k_out = (k * decay[(GLA_LEVELS + 1) * Lc:(GLA_LEVELS + 2) * Lc]).astype(BF16)
    s_new = a_tot * s_old
    for h in range(GLA_HEADS):
        vh = v[:, h * GLA_DV:(h + 1) * GLA_DV]
        qh = jnp.where(lane_head == h, q_in, jnp.zeros_like(q_in))
        o_h = jnp.dot(att[h].astype(BF16), vh, preferred_element_type=F32)
        o_h = o_h + jnp.dot(qh, s_b, preferred_element_type=F32)
        o_ref[0, :, h * GLA_DV:(h + 1) * GLA_DV] = o_h
        c_h = lax.dot_general(k_out, vh, (((0,), (0,)), ((), ())), preferred_element_type=F32)
        s_new = s_new + jnp.where(row_head == h, c_h, 0.0)
    s_scr[...] = s_new

    @pl.when(i == nch - 1)
    def _():
        sfin_ref[0, 0] = s_new


def _gla(gla_arr, gz_arr, w2p, bias, e2, m2, s0, B, T):
    nch = T // GLA_CHUNK
    blk = lambda b, d, i: b * nch + jnp.where(d == 0, i, nch - 1 - i)
    st = lambda b, d, i: (b, d, 0, 0)
    return pl.pallas_call(
        functools.partial(_gla_kernel, nch=nch),
        grid=(B, 2, nch),
        in_specs=[
            pl.BlockSpec((GLA_CHUNK, 2 * GLA_QK + GLA_V), lambda b, d, i: (blk(b, d, i), 0)),
            pl.BlockSpec((GLA_CHUNK, GZ_W), lambda b, d, i: (blk(b, d, i), 0)),
            pl.BlockSpec((1, GZ_W, GLA_QK), lambda b, d, i: (d, 0, 0)),
            pl.BlockSpec((1, 1, GLA_QK), lambda b, d, i: (d, 0, 0)),
            pl.BlockSpec((1, (GLA_LEVELS + 2) * GLA_CHUNK, GLA_CHUNK), lambda b, d, i: (d, 0, 0)),
            pl.BlockSpec((1, GLA_LEVELS + 1, GLA_CHUNK, GLA_CHUNK), lambda b, d, i: (d, 0, 0, 0)),
            pl.BlockSpec((1, 1, GLA_QK, GLA_DV), st),
        ],
        out_specs=[
            pl.BlockSpec((1, GLA_CHUNK, GLA_V), lambda b, d, i: (d, blk(b, d, i), 0)),
            pl.BlockSpec((1, 1, GLA_QK, GLA_DV), st),
        ],
        out_shape=[
            jax.ShapeDtypeStruct((2, B * T, GLA_V), F32),
            jax.ShapeDtypeStruct((B, 2, GLA_QK, GLA_DV), F32),
        ],
        scratch_shapes=[pltpu.VMEM((GLA_QK, GLA_DV), F32)],
        compiler_params=_cparams("parallel", "parallel", "arbitrary"),
        name="gla_scan",
    )(gla_arr, gz_arr, w2p, bias, e2, m2, s0)


def _lru_kernel(x_ref, pv_ref, nx_ref, cw_ref, cb_ref, wg_ref, bg_ref, lam_ref, h0_ref, o_ref, hfin_ref, carry, *, nblk):
    d = pl.program_id(1)
    i = pl.program_id(2)
    T = LRU_CHUNK
    W = LRU_WIDTH

    @pl.when(i == 0)
    def _():
        carry[...] = jnp.broadcast_to(h0_ref[0, 0], (SUBLANES, W))

    li = jnp.where(d == 0, i, nblk - 1 - i)
    pv = jnp.where(li > 0, pv_ref[...], 0.0)
    nx = jnp.where(li < nblk - 1, nx_ref[...], 0.0)
    xe = jnp.concatenate([pv, x_ref[...], nx], axis=0)
    n_ext = T + 2 * SUBLANES
    win = lambda off: pltpu.roll(xe, n_ext - off, 0)[0:T]
    cw = cw_ref[...]
    xc = cb_ref[...] + win(6) * cw[0:1] + win(7) * cw[1:2] + xe[SUBLANES:SUBLANES + T] * cw[2:3] + win(9) * cw[3:4]

    g = jnp.dot(xc.astype(BF16), wg_ref[0], preferred_element_type=F32) + bg_ref[0]
    r = jax.nn.sigmoid(g[:, :W])
    gi = jax.nn.sigmoid(g[:, W:])
    log_a = (-LRU_C * _softplus(-lam_ref[0])) * r
    a = jnp.exp(log_a)
    u = jnp.sqrt(-jnp.tanh(log_a) * (a * a + 1.0)) * (gi * xc)
    sub_row = lax.broadcasted_iota(I32, (T, W), 0) & (SUBLANES - 1)
    groups = T // SUBLANES

    @pl.when(d == 0)
    def _():
        aa, uu = a, u
        sh = 1
        while sh < SUBLANES:
            ok = sub_row >= sh
            uu = uu + aa * jnp.where(ok, pltpu.roll(uu, sh, 0), 0.0)
            aa = aa * jnp.where(ok, pltpu.roll(aa, sh, 0), 1.0)
            sh *= 2
        c = carry[0:1]
        for g in range(groups):
            rows = slice(g * SUBLANES, (g + 1) * SUBLANES)
            hg = uu[rows] + aa[rows] * c
            o_ref[0, rows, :] = hg
            c = hg[SUBLANES - 1:SUBLANES]
        carry[...] = jnp.broadcast_to(c, (SUBLANES, W))

    @pl.when(d == 1)
    def _():
        aa, uu = a, u
        sh = 1
        while sh < SUBLANES:
            ok = sub_row < SUBLANES - sh
            uu = uu + aa * jnp.where(ok, pltpu.roll(uu, T - sh, 0), 0.0)
            aa = aa * jnp.where(ok, pltpu.roll(aa, T - sh, 0), 1.0)
            sh *= 2
        c = carry[0:1]
        for g in reversed(range(groups)):
            rows = slice(g * SUBLANES, (g + 1) * SUBLANES)
            hg = uu[rows] + aa[rows] * c
            o_ref[0, rows, :] = hg
            c = hg[0:1]
        carry[...] = jnp.broadcast_to(c, (SUBLANES, W))

    @pl.when(i == nblk - 1)
    def _():
        hfin_ref[0, 0] = carry[0:1]


def _lru(lru_arr, cw, cb, wg, bg, lam, h0, B, T):
    nblk = T // LRU_CHUNK
    per8 = LRU_CHUNK // SUBLANES
    n8 = B * T // SUBLANES
    blk = lambda b, d, i: b * nblk + jnp.where(d == 0, i, nblk - 1 - i)
    dirw = lambda b, d, i: (d, 0, 0)
    st = lambda b, d, i: (b, d, 0, 0)
    return pl.pallas_call(
        functools.partial(_lru_kernel, nblk=nblk),
        grid=(B, 2, nblk),
        in_specs=[
            pl.BlockSpec((LRU_CHUNK, LRU_WIDTH), lambda b, d, i: (blk(b, d, i), 0)),
            pl.BlockSpec((SUBLANES, LRU_WIDTH), lambda b, d, i: (jnp.maximum(blk(b, d, i) * per8 - 1, 0), 0)),
            pl.BlockSpec((SUBLANES, LRU_WIDTH), lambda b, d, i: (jnp.minimum((blk(b, d, i) + 1) * per8, n8 - 1), 0)),
            pl.BlockSpec((SUBLANES, LRU_WIDTH), lambda b, d, i: (0, 0)),
            pl.BlockSpec((1, LRU_WIDTH), lambda b, d, i: (0, 0)),
            pl.BlockSpec((1, LRU_WIDTH, 2 * LRU_WIDTH), dirw),
            pl.BlockSpec((1, 1, 2 * LRU_WIDTH), dirw),
            pl.BlockSpec((1, 1, LRU_WIDTH), dirw),
            pl.BlockSpec((1, 1, 1, LRU_WIDTH), st),
        ],
        out_specs=[
            pl.BlockSpec((1, LRU_CHUNK, LRU_WIDTH), lambda b, d, i: (d, blk(b, d, i), 0)),
            pl.BlockSpec((1, 1, 1, LRU_WIDTH), st),
        ],
        out_shape=[
            jax.ShapeDtypeStruct((2, B * T, LRU_WIDTH), F32),
            jax.ShapeDtypeStruct((B, 2, 1, LRU_WIDTH), F32),
        ],
        scratch_shapes=[pltpu.VMEM((SUBLANES, LRU_WIDTH), F32)],
        compiler_params=_cparams("parallel", "parallel", "arbitrary"),
        name="lru_scan",
    )(lru_arr, lru_arr, lru_arr, cw, cb, wg, bg, lam, h0)


def _out_kernel(h_ref, attn_ref, go_ref, gr_ref, gg_ref, lh_ref, lg_ref, w_ref, g1_ref, a2_ref, s2_ref, wr_ref, br_ref,
                vt_ref, lt_ref, hn_ref, v_ref, lgt_ref, *, n_main):
    i = pl.program_id(0)

    @pl.when(i < n_main)
    def _():
        _out_body(h_ref, attn_ref, go_ref, gr_ref, gg_ref, lh_ref, lg_ref, w_ref, g1_ref, a2_ref, s2_ref, wr_ref, br_ref,
                  hn_ref, v_ref, lgt_ref)

    @pl.when(i >= n_main)
    def _():
        v_ref[...] = vt_ref[...]
        lgt_ref[...] = lt_ref[...]


def _out_body(h_ref, attn_ref, go_ref, gr_ref, gg_ref, lh_ref, lg_ref, w_ref, g1_ref, a2_ref, s2_ref, wr_ref, br_ref,
              hn_ref, v_ref, lgt_ref):
    o = go_ref[0] + go_ref[1]
    gr = gr_ref[...]
    gate = gr * jax.nn.sigmoid(gr)
    parts = []
    for hh in range(GLA_HEADS):
        oh = o[:, hh * GLA_DV:(hh + 1) * GLA_DV]
        y = (oh * lax.rsqrt(jnp.mean(oh * oh, axis=-1, keepdims=True) + RMS_EPS)) * gg_ref[...]
        parts.append(y * gate[:, hh * GLA_DV:(hh + 1) * GLA_DV])
    gla = jnp.concatenate(parts, axis=1).astype(BF16)
    lg = lg_ref[...]
    gelu = lg * (0.5 * (1.0 + jnp.tanh(np.sqrt(2.0 / np.pi).astype(np.float32) * (lg + 0.044715 * (lg * lg * lg)))))
    lru = ((lh_ref[0] + lh_ref[1]) * gelu).astype(BF16)
    y = jnp.dot(attn_ref[...], w_ref[0:ATTN_Q], preferred_element_type=F32)
    y = y + jnp.dot(gla, w_ref[ATTN_Q:ATTN_Q + GLA_V], preferred_element_type=F32)
    y = y + jnp.dot(lru, w_ref[ATTN_Q + GLA_V:D_MIX], preferred_element_type=F32)
    hn = h_ref[...] + g1_ref[0] * y
    hn_ref[...] = hn
    v = (hn * lax.rsqrt(jnp.mean(hn * hn, axis=-1, keepdims=True) + RMS_EPS)) * a2_ref[0] + s2_ref[0]
    _store_token_tiles(v_ref, _pack_bf16_pairs(v))
    v_hi = v.astype(BF16)
    v_lo = (v - v_hi.astype(F32)).astype(BF16)
    t = jnp.dot(v_hi, wr_ref[...], preferred_element_type=F32)
    t_lo = jnp.dot(v_lo, wr_ref[:, 0:LANES], preferred_element_type=F32)
    lgt_ref[...] = (t[:, :LANES] + t[:, LANES:]) + t_lo + br_ref[...]


def _mixer_out(h, attn, go, gla_arr, gg, lh, lru_arr, w_out, g1, a2, s2, wr, br, seq_len, tail):
    rows, D = h.shape
    tm = _tile(seq_len, 256)
    per = seq_len // tm
    n_main = rows // tm
    tw = tm * (D // 2 // LANES)
    if tail is None:
        tail = (jnp.zeros((tw, LANES), I32), jnp.zeros((tm, LANES), F32))
        n_tail = 0
    else:
        assert tail[1].shape[0] % tm == 0
        n_tail = tail[1].shape[0] // tm
    n_tok = rows + n_tail * tm
    main = lambda i: jnp.minimum(i, n_main - 1)
    row = lambda i: (main(i), 0)
    row3 = lambda i: (0, main(i), 0)
    bat = lambda i: (main(i) // per, 0, 0)
    const = lambda i: (0, 0)
    tok = lambda i: (i, 0)
    trow = lambda i: (jnp.maximum(i - n_main, 0), 0)
    in_specs = [
        pl.BlockSpec((tm, D), row),
        pl.BlockSpec((tm, ATTN_Q), row),
        pl.BlockSpec((2, tm, GLA_V), row3),
        pl.BlockSpec((tm, GLA_V), lambda i: (main(i), 2)),
        pl.BlockSpec((1, GLA_DV), const),
        pl.BlockSpec((2, tm, LRU_WIDTH), row3),
        pl.BlockSpec((tm, LRU_WIDTH), lambda i: (main(i), 1)),
        pl.BlockSpec((D_MIX, D), const, pipeline_mode=pl.Buffered(1)),
        pl.BlockSpec((1, 1, D), bat),
        pl.BlockSpec((1, 1, D), bat),
        pl.BlockSpec((1, 1, D), bat),
        pl.BlockSpec((D, 2 * LANES), const),
        pl.BlockSpec((1, LANES), const),
        pl.BlockSpec((tw, LANES), trow),
        pl.BlockSpec((tm, LANES), trow),
    ]
    args = [h, attn, go, gla_arr, gg, lh, lru_arr, w_out, g1, a2, s2, wr, br, tail[0], tail[1]]
    return pl.pallas_call(
        functools.partial(_out_kernel, n_main=n_main),
        grid=(n_main + n_tail,),
        in_specs=in_specs,
        out_specs=[
            pl.BlockSpec((tm, D), row),
            pl.BlockSpec((tw, LANES), tok),
            pl.BlockSpec((tm, LANES), tok),
        ],
        out_shape=[
            jax.ShapeDtypeStruct((rows, D), F32),
            jax.ShapeDtypeStruct((n_tok * (tw // tm), LANES), I32),
            jax.ShapeDtypeStruct((n_tok, LANES), F32),
        ],
        compiler_params=_cparams("arbitrary"),
        name="mixer_out",
    )(*args)


def _route_kernel(lg_ref, oi_ref, of_ref, cnt_ref, carry):
    i = pl.program_id(0)

    @pl.when(i == 0)
    def _():
        carry[...] = jnp.zeros_like(carry)

    lg = lg_ref[...]
    tm = lg.shape[0]
    col = lax.broadcasted_iota(I32, lg.shape, 1)
    colf = col.astype(F32)
    big = float(LANES)
    is_g = col < N_GROUPS
    gm = jnp.max(jnp.where(is_g, lg, -jnp.inf), axis=-1, keepdims=True)
    eg = jnp.where(is_g, jnp.exp(lg - gm), 0.0)
    pg = eg / jnp.sum(eg, axis=-1, keepdims=True)
    p_grp = jnp.max(pg, axis=-1, keepdims=True)
    grp = jnp.min(jnp.where(is_g & (pg == p_grp), colf, big), axis=-1, keepdims=True).astype(I32)

    sel = (col >= N_GROUPS) & (col < N_GROUPS + N_EXPERTS) & (((col - N_GROUPS) >> 3) == grp)
    em = jnp.max(jnp.where(sel, lg, -jnp.inf), axis=-1, keepdims=True)
    ee = jnp.where(sel, jnp.exp(lg - em), 0.0)
    pe = ee / jnp.sum(ee, axis=-1, keepdims=True)
    p1 = jnp.max(jnp.where(sel, pe, -1.0), axis=-1, keepdims=True)
    c1 = jnp.min(jnp.where(sel & (pe == p1), colf, big), axis=-1, keepdims=True).astype(I32)
    rest = sel & (col != c1)
    p2 = jnp.max(jnp.where(rest, pe, -1.0), axis=-1, keepdims=True)
    c2 = jnp.min(jnp.where(rest & (pe == p2), colf, big), axis=-1, keepdims=True).astype(I32)
    e1 = c1 - N_GROUPS
    e2 = c2 - N_GROUPS
    den = p1 + p2
    g1 = p_grp * (p1 / den)
    g2 = p_grp * (p2 / den)

    hit1 = col == e1
    hit2 = col == e2
    oh = jnp.where(hit1 | hit2, 1.0, 0.0)
    r_i = lax.broadcasted_iota(I32, (tm, tm), 0)
    c_i = lax.broadcasted_iota(I32, (tm, tm), 1)
    tri = jnp.where(c_i < r_i, 1.0, 0.0).astype(BF16)
    before = jnp.dot(tri, oh.astype(BF16), preferred_element_type=F32) + carry[0:1]
    r1 = jnp.sum(jnp.where(hit1, before, 0.0), axis=-1, keepdims=True).astype(I32)
    r2 = jnp.sum(jnp.where(hit2, before, 0.0), axis=-1, keepdims=True).astype(I32)
    new = carry[0:1] + jnp.sum(oh, axis=0, keepdims=True)
    carry[...] = jnp.broadcast_to(new, carry.shape)
    cnt_ref[...] = jnp.broadcast_to(new, cnt_ref.shape)

    zero = jnp.zeros_like(col)
    oi_ref[...] = jnp.where(col == 0, e1, jnp.where(col == 1, e2, jnp.where(col == 2, r1, jnp.where(col == 3, r2, zero))))
    of_ref[...] = jnp.where(col == 0, g1, jnp.where(col == 1, g2, 0.0))


def _route(logits):
    n_tok = logits.shape[0]
    tm = _tile(n_tok, 256)
    row = lambda i: (i, 0)
    return pl.pallas_call(
        _route_kernel,
        grid=(n_tok // tm,),
        in_specs=[pl.BlockSpec((tm, LANES), row)],
        out_specs=[
            pl.BlockSpec((tm, LANES), row),
            pl.BlockSpec((tm, LANES), row),
            pl.BlockSpec((SUBLANES, LANES), lambda i: (0, 0)),
        ],
        out_shape=[
            jax.ShapeDtypeStruct((n_tok, LANES), I32),
            jax.ShapeDtypeStruct((n_tok, LANES), F32),
            jax.ShapeDtypeStruct((SUBLANES, LANES), F32),
        ],
        scratch_shapes=[pltpu.VMEM((SUBLANES, LANES), F32)],
        compiler_params=_cparams("arbitrary"),
        name="route",
    )(logits)


def _cast_rows(src, dst):
    step = _tile(dst.shape[0], 256)

    def body(r, c):
        rows = pl.ds(pl.multiple_of(r * step, step), step)
        dst[rows, :] = src[rows, :].astype(BF16)
        return c

    lax.fori_loop(0, dst.shape[0] // step, body, 0)


def _moe_ffn_kernel(be_ref, nu_ref, nx_ref, src_ref, srcn_ref, v_hbm, w1_hbm, w3_hbm, w2_hbm, o_ref,
                    xbuf, st1, st3, st2, w1s, w3s, w2s, sem_x, sem_w, *, layer, sub):
    i = pl.program_id(0)
    nu = nu_ref[0]
    rows = ROW_BLOCK * sub

    def gather(idx_ref, slot):
        for r in range(ROW_BLOCK):
            s = pl.multiple_of(idx_ref[0, 0, r] * sub, sub)
            pltpu.make_async_copy(v_hbm.at[pl.ds(s, sub)], xbuf.at[slot, pl.ds(r * sub, sub)], sem_x.at[slot]).start()

    def wait_rows(slot):
        pltpu.make_async_copy(v_hbm.at[pl.ds(0, rows)], xbuf.at[slot], sem_x.at[slot]).wait()

    def weight_copies(e):
        return (pltpu.make_async_copy(w1_hbm.at[layer, e], st1, sem_w.at[0]),
                pltpu.make_async_copy(w3_hbm.at[layer, e], st3, sem_w.at[1]),
                pltpu.make_async_copy(w2_hbm.at[layer, e], st2, sem_w.at[2]))

    @pl.when(i < nu)
    def _():
        e = be_ref[i]
        slot = i & 1

        @pl.when(i == 0)
        def _():
            for cp in weight_copies(e):
                cp.start()
            gather(src_ref, 0)

        @pl.when((i == 0) | (e != be_ref[jnp.maximum(i - 1, 0)]))
        def _():
            for cp in weight_copies(e):
                cp.wait()
            _cast_rows(st1, w1s)
            _cast_rows(st3, w3s)
            _cast_rows(st2, w2s)

            @pl.when(nx_ref[i] >= 0)
            def _():
                for cp in weight_copies(nx_ref[i]):
                    cp.start()

        wait_rows(slot)
        gather(srcn_ref, 1 - slot)
        xb = _unpack_bf16_pairs(_load_token_tiles(xbuf.at[slot], ROW_BLOCK)).astype(BF16)
        a = jnp.dot(xb, w1s[...], preferred_element_type=F32)
        b = jnp.dot(xb, w3s[...], preferred_element_type=F32)
        hmid = ((a * jax.nn.sigmoid(a)) * b).astype(BF16)
        _store_token_tiles(o_ref, _pack_bf16_pairs(jnp.dot(hmid, w2s[...], preferred_element_type=F32)))

        @pl.when(i == nu - 1)
        def _():
            wait_rows(1 - slot)

    @pl.when(i >= nu)
    def _():
        o_ref[...] = jnp.zeros_like(o_ref)


def _moe_experts(v_tiles, src, blk_e, n_used, nxt_e, w1, w3, w2, layer):
    D, FF = w1.shape[-2:]
    sub = D // 2 // LANES
    P = src.shape[0]
    nb = P // ROW_BLOCK
    tw = ROW_BLOCK * sub
    cur = lambda i, be, nu, nx: (jnp.minimum(i, nu[0] - 1), 0, 0)
    nxt = lambda i, be, nu, nx: (jnp.minimum(i + 1, nu[0] - 1), 0, 0)
    src3 = src.reshape(nb, 1, ROW_BLOCK)
    hbm = pl.BlockSpec(memory_space=pl.ANY)
    return pl.pallas_call(
        functools.partial(_moe_ffn_kernel, layer=layer, sub=sub),
        grid_spec=pltpu.PrefetchScalarGridSpec(
            num_scalar_prefetch=3,
            grid=(nb,),
            in_specs=[
                pl.BlockSpec((1, 1, ROW_BLOCK), cur, memory_space=pltpu.SMEM),
                pl.BlockSpec((1, 1, ROW_BLOCK), nxt, memory_space=pltpu.SMEM),
                hbm, hbm, hbm, hbm,
            ],
            out_specs=pl.BlockSpec((tw, LANES), lambda i, be, nu, nx: (i, 0)),
            scratch_shapes=[
                pltpu.VMEM((2, tw, LANES), I32),
                pltpu.VMEM((D, FF), F32), pltpu.VMEM((D, FF), F32), pltpu.VMEM((FF, D), F32),
                pltpu.VMEM((D, FF), BF16), pltpu.VMEM((D, FF), BF16), pltpu.VMEM((FF, D), BF16),
                pltpu.SemaphoreType.DMA((2,)), pltpu.SemaphoreType.DMA((3,)),
            ],
        ),
        out_shape=jax.ShapeDtypeStruct((P * sub, LANES), I32),
        compiler_params=_cparams("arbitrary"),
        name="moe_ffn",
    )(blk_e, n_used, nxt_e, src3, src3, v_tiles, w1, w3, w2)


def _combine_kernel(dest_ref, destn_ref, h_ref, gate_ref, g2_ref, fg_ref, ys_ref, o_ref, buf, sem, *, final, nsteps):
    i = pl.program_id(0)
    slot = i & 1
    tm = h_ref.shape[0]
    rows = buf.shape[2]
    sub = rows // tm

    def gather(idx_ref, sl):
        def start(j, c):
            src = pl.multiple_of(idx_ref[0, 0, j] * sub, sub)
            dst = pl.multiple_of((j >> 1) * sub, sub)
            pltpu.make_async_copy(ys_ref.at[pl.ds(src, sub)], buf.at[sl, j & 1, pl.ds(dst, sub)], sem.at[sl]).start()
            return c

        lax.fori_loop(0, tm * TOP_K, start, 0, unroll=8)

    def wait_rows(sl):
        for kk in range(TOP_K):
            pltpu.make_async_copy(ys_ref.at[pl.ds(0, rows)], buf.at[sl, kk], sem.at[sl]).wait()

    @pl.when(i == 0)
    def _():
        gather(dest_ref, 0)

    gather(destn_ref, 1 - slot)
    wait_rows(slot)
    gate = gate_ref[...]
    f = (_unpack_bf16_pairs(_load_token_tiles(buf.at[slot, 0], tm)) * gate[:, 0:1]
         + _unpack_bf16_pairs(_load_token_tiles(buf.at[slot, 1], tm)) * gate[:, 1:2])

    @pl.when(i == nsteps - 1)
    def _():
        wait_rows(1 - slot)

    hn = h_ref[...] + g2_ref[0] * f
    if final:
        hn = (hn * lax.rsqrt(jnp.mean(hn * hn, axis=-1, keepdims=True) + RMS_EPS)) * fg_ref[...]
    o_ref[...] = hn


def _combine(h, dest, gates, g2, fg, ys, seq_len, tok_off, final):
    rows, D = h.shape
    tm = _tile(seq_len, 256)
    per = seq_len // tm
    off = tok_off // tm
    n_blk_all = dest.shape[0] // (tm * TOP_K)
    nsteps = rows // tm
    dest3 = dest.reshape(n_blk_all, 1, tm * TOP_K)
    return pl.pallas_call(
        functools.partial(_combine_kernel, final=final, nsteps=nsteps),
        grid=(nsteps,),
        in_specs=[
            pl.BlockSpec((1, 1, tm * TOP_K), lambda i: (i + off, 0, 0), memory_space=pltpu.SMEM),
            pl.BlockSpec((1, 1, tm * TOP_K), lambda i: (jnp.minimum(i + 1, nsteps - 1) + off, 0, 0), memory_space=pltpu.SMEM),
            pl.BlockSpec((tm, D), lambda i: (i, 0)),
            pl.BlockSpec((tm, LANES), lambda i: (i + off, 0)),
            pl.BlockSpec((1, 1, D), lambda i: (i // per, 0, 0)),
            pl.BlockSpec((1, D), lambda i: (0, 0)),
            pl.BlockSpec(memory_space=pl.ANY),
        ],
        out_specs=pl.BlockSpec((tm, D), lambda i: (i, 0)),
        out_shape=jax.ShapeDtypeStruct((rows, D), F32),
        scratch_shapes=[pltpu.VMEM((2, TOP_K, tm * (D // 2 // LANES), LANES), I32), pltpu.SemaphoreType.DMA((2,))],
        compiler_params=_cparams("arbitrary"),
        name="combine",
    )(dest3, dest3, h, gates, g2, fg, ys)


def _rope_tables(S):
    rows = S // GRID_W
    row = jnp.repeat(jnp.arange(rows, dtype=F32), GRID_W)
    col = jnp.tile(jnp.arange(GRID_W, dtype=F32), rows)
    n_freq = HEAD_DIM // 4
    inv = ROPE_THETA ** (-jnp.arange(n_freq, dtype=F32) / n_freq)
    ang = jnp.concatenate([row[:, None] * inv, col[:, None] * inv], axis=-1)
    cos = jnp.repeat(jnp.cos(ang), 2, axis=1)
    sin = jnp.repeat(jnp.sin(ang), 2, axis=1)
    even = (jnp.arange(HEAD_DIM) % 2) == 0
    return cos, jnp.where(even, -sin, 0.0), jnp.where(even, 0.0, sin)


def _block_diag(w):
    n, c, _ = w.shape
    eye = jnp.eye(n, dtype=w.dtype)
    return (eye[:, None, :, None] * w[:, :, None, :]).reshape(n * c, n * c)


def _slot_plan(ids, counts_row, n_tok):
    e = ids[:, 0:TOP_K]
    rank = ids[:, TOP_K:2 * TOP_K]
    counts = counts_row[:N_EXPERTS].astype(I32)
    padded = (counts + ROW_BLOCK - 1) // ROW_BLOCK * ROW_BLOCK
    pad_end = jnp.cumsum(padded)
    pad_start = pad_end - padded
    dest = (pad_start[e] + rank).reshape(n_tok * TOP_K)
    n_blocks = -(-(n_tok * TOP_K + N_EXPERTS * (ROW_BLOCK - 1)) // ROW_BLOCK)
    starts = jnp.arange(n_blocks, dtype=I32) * ROW_BLOCK
    blk_e = jnp.minimum(jnp.sum(pad_end[None, :] <= starts[:, None], axis=1), N_EXPERTS - 1).astype(I32)
    n_used = (pad_end[-1] // ROW_BLOCK).astype(I32).reshape(1)
    run_next = pad_end[blk_e] // ROW_BLOCK
    nxt_e = jnp.where(run_next < n_used[0], blk_e[jnp.minimum(run_next, n_blocks - 1)], -1).astype(I32)
    src = jnp.zeros((n_blocks * ROW_BLOCK,), I32).at[dest].set(jnp.arange(n_tok * TOP_K, dtype=I32) // TOP_K)
    return dest, src, blk_e, n_used, nxt_e


def kernel(x, c, ctx, c_ctx, ada_w, ada_b, norm_mix_g, norm_ffn_g, w_in, attn_sink, gla_gate_w2, gla_gate_b, gla_norm_g, lru_conv_w, lru_conv_b, lru_wa, lru_ba, lru_wx, lru_bx, lru_lambda, w_out, router_g_w, router_g_b, router_e_w, router_e_b, moe_w1, moe_w3, moe_w2, final_norm_g):
    B, S, D = x.shape
    C = ctx.shape[1]
    L = ada_w.shape[0]
    assert S % max(ATTN_BLOCK, GLA_CHUNK, LRU_CHUNK, GRID_W) == 0 and C % max(ATTN_BLOCK, GLA_CHUNK, LRU_CHUNK) == 0
    assert B + 1 <= SUBLANES and D % (2 * LANES) == 0

    cond = jnp.concatenate([c, c_ctx[None], jnp.zeros((SUBLANES - B - 1, D), F32)], axis=0)
    mods = _ada_mods(cond, ada_w, ada_b).reshape(L, SUBLANES, 6, D)

    cos_l, se_l, so_l = _rope_tables(S)
    cos_c = jnp.ones((C, HEAD_DIM), F32)
    zero_c = jnp.zeros((C, HEAD_DIM), F32)
    e2_np, m2_np = _gla_constants()
    gla_e = jnp.asarray(e2_np, BF16)
    gla_m = jnp.asarray(m2_np, F32)

    h_lat = x.reshape(B * S, D)
    h_ctx = ctx.reshape(B * C, D)
    out = None
    for l in range(L):
        last = l == L - 1
        ml = mods[l, :B]
        mc = jnp.broadcast_to(mods[l, B][None], (B, 6, D))
        per_b = lambda m, j: m[:, j][:, None, :]
        a1_l, a1_c = [(1.0 + per_b(m, 1)) * norm_mix_g[l] for m in (ml, mc)]
        a2_l, a2_c = [(1.0 + per_b(m, 4)) * norm_ffn_g[l] for m in (ml, mc)]

        w = w_in[l]
        c_gz = ATTN_Q + 2 * ATTN_KV + 2 * GLA_QK + 2 * GLA_V
        w_packed = jnp.concatenate(
            [w[:, :c_gz], w[:, c_gz:c_gz + 2 * GLA_RANK], jnp.zeros((D, GZ_W - 2 * GLA_RANK), F32), w[:, c_gz + 2 * GLA_RANK:]],
            axis=1).astype(BF16)
        qkv_l, gla_l, gz_l, lru_l = _in_proj(h_lat, a1_l, per_b(ml, 0), cos_l, se_l, so_l, w_packed, S)
        qkv_c, gla_c, gz_c, lru_c = _in_proj(h_ctx, a1_c, per_b(mc, 0), cos_c, zero_c, zero_c, w_packed, C)

        sink_b = jnp.broadcast_to(attn_sink[l][:, None], (N_Q_HEADS, LANES)).astype(F32)
        attn_l = _attn_latent(qkv_l, qkv_c, sink_b, B, S, C)

        w2p = jnp.zeros((2, GZ_W, GLA_QK), F32)
        w2p = w2p.at[0, :GLA_RANK].set(gla_gate_w2[l, 0]).at[1, GLA_RANK:2 * GLA_RANK].set(gla_gate_w2[l, 1]).astype(BF16)
        gbias = gla_gate_b[l].reshape(2, 1, GLA_QK)
        s_zero = jnp.zeros((B, 2, GLA_QK, GLA_DV), F32)
        go_c, s_ctx = _gla(gla_c, gz_c, w2p, gbias, gla_e, gla_m, s_zero, B, C)
        go_l, _ = _gla(gla_l, gz_l, w2p, gbias, gla_e, gla_m, s_ctx, B, S)

        cw = jnp.concatenate([lru_conv_w[l], jnp.zeros((SUBLANES - CONV_W, LRU_WIDTH), F32)], axis=0)
        cb = lru_conv_b[l].reshape(1, LRU_WIDTH)
        wg = jnp.stack([jnp.concatenate([_block_diag(lru_wa[l, d]), _block_diag(lru_wx[l, d])], axis=1) for d in range(2)]).astype(BF16)
        bg = jnp.concatenate([lru_ba[l], lru_bx[l]], axis=1).reshape(2, 1, 2 * LRU_WIDTH)
        lam = lru_lambda[l].reshape(2, 1, LRU_WIDTH)
        h_zero = jnp.zeros((B, 2, 1, LRU_WIDTH), F32)
        lh_c, hs_ctx = _lru(lru_c, cw, cb, wg, bg, lam, h_zero, B, C)
        lh_l, _ = _lru(lru_l, cw, cb, wg, bg, lam, hs_ctx, B, S)

        wo = w_out[l].astype(BF16)
        gg = gla_norm_g[l].reshape(1, GLA_DV)
        wr = jnp.concatenate([router_g_w[l], router_e_w[l], jnp.zeros((D, LANES - N_GROUPS - N_EXPERTS), F32)], axis=1)
        wr_hi = wr.astype(BF16)
        wr = jnp.concatenate([wr_hi, (wr - wr_hi.astype(F32)).astype(BF16)], axis=1)
        br =jnp.concatenate([router_g_b[l], router_e_b[l], jnp.zeros((LANES - N_GROUPS - N_EXPERTS,), F32)]).reshape(1, LANES)
        n_tok = B * S if last else B * (S + C)
        tail = None
        if not last:
            attn_c = _attn_context(qkv_c, sink_b, B, C)
            h_ctx, v_c, lgt_c = _mixer_out(h_ctx, attn_c, go_c, gla_c, gg, lh_c, lru_c, wo, per_b(mc, 2), a2_c, per_b(mc, 3),
                                           wr, br, C, None)
            tail = (v_c, lgt_c)
        h_lat, v_all, lgt_all = _mixer_out(h_lat, attn_l, go_l, gla_l, gg, lh_l, lru_l, wo, per_b(ml, 2), a2_l, per_b(ml, 3),
                                           wr, br, S, tail)

        ids, gates, counts = _route(lgt_all)
        dest, src, blk_e, n_used, nxt_e = _slot_plan(ids, counts[0], n_tok)
        ys = _moe_experts(v_all, src, blk_e, n_used, nxt_e, moe_w1, moe_w3, moe_w2, l)
        fg = final_norm_g.reshape(1, D)
        h_lat = _combine(h_lat, dest, gates, per_b(ml, 5), fg, ys, S, 0, last)
        if not last:
            h_ctx = _combine(h_ctx, dest, gates, per_b(mc, 5), fg, ys, C, B * S, False)
        out = h_lat
    return out.reshape(B, S, D)
```

```python
import functools

import numpy as np
import jax
import jax.numpy as jnp
from jax import lax
from jax.experimental import pallas as pl
from jax.experimental.pallas import tpu as pltpu

F32 = jnp.float32
BF16 = jnp.bfloat16
I32 = jnp.int32

GRID_W = 64
RMS_EPS = 1e-6
N_Q_HEADS = 8
N_KV_HEADS = 2
HEAD_DIM = 128
WINDOW = 128
ATTN_BLOCK = 128
ROPE_THETA = 10000.0
GLA_HEADS = 4
GLA_DK = 64
GLA_DV = 128
GLA_RANK = 16
GLA_TAU = 16.0
LRU_WIDTH = 512
LRU_BLOCKS = 8
LRU_C = 8.0
CONV_W = 4
N_GROUPS = 4
EXPERTS_PER_GROUP = 8
N_EXPERTS = 32
TOP_K = 2
ATTN_Q = N_Q_HEADS * HEAD_DIM
ATTN_KV = N_KV_HEADS * HEAD_DIM
GLA_QK = GLA_HEADS * GLA_DK
GLA_V = GLA_HEADS * GLA_DV
D_MIX = ATTN_Q + GLA_V + LRU_WIDTH

LANES = 128
SUBLANES = 8
VMEM_LIMIT_BYTES = 56 * 1024 * 1024

QKV_W = ATTN_Q + 2 * ATTN_KV
GLA_W = 2 * GLA_QK + 2 * GLA_V
GZ_W = LANES
LRU_W = 2 * LRU_WIDTH
COL_GLA = QKV_W
COL_GZ = COL_GLA + GLA_W
COL_LRU = COL_GZ + GZ_W
W_IN_PACKED = COL_LRU + LRU_W

GLA_CHUNK = 128
GLA_LEVELS = 7
LRU_CHUNK = 128
ROW_BLOCK = 256


def _cparams(*sem):
    return pltpu.CompilerParams(dimension_semantics=sem, vmem_limit_bytes=VMEM_LIMIT_BYTES)


def _tile(n, pref):
    t = min(n, pref)
    while n % t:
        t -= SUBLANES
    return t


def _softplus(x):
    return jnp.maximum(x, 0.0) + jnp.log1p(jnp.exp(-jnp.abs(x)))


def _pack_bf16_pairs(x):
    n = x.shape[1] // 2
    lo = lax.bitcast_convert_type(x[:, :n].astype(BF16).astype(F32), I32)
    hi = lax.bitcast_convert_type(x[:, n:].astype(BF16).astype(F32), I32)
    return hi | lax.shift_right_logical(lo, jnp.full(lo.shape, 16, I32))


def _unpack_bf16_pairs(p):
    lo = lax.bitcast_convert_type(lax.shift_left(p, jnp.full(p.shape, 16, I32)), F32)
    hi = lax.bitcast_convert_type(p & jnp.int32(-65536), F32)
    return jnp.concatenate([lo, hi], axis=1)


def _store_token_tiles(ref, packed):
    m = packed.shape[0]
    sub = packed.shape[1] // LANES
    for s in range(sub):
        ref[pl.ds(s, m, stride=sub), :] = packed[:, s * LANES:(s + 1) * LANES]


def _load_token_tiles(ref, m):
    sub = ref.shape[0] // m
    return jnp.concatenate([ref[pl.ds(s, m, stride=sub), :] for s in range(sub)], axis=1)


def _ada_kernel(c_ref, w_ref, b_ref, o_ref):
    c = c_ref[...]
    s = c * jax.nn.sigmoid(c)
    o_ref[0] = jnp.dot(s.astype(BF16), w_ref[0].astype(BF16), preferred_element_type=F32) + b_ref[0]


def _ada_mods(cond, ada_w, ada_b):
    L, D, D6 = ada_w.shape
    tn = _tile(D6, 1536)
    while tn % LANES:
        tn -= SUBLANES
    return pl.pallas_call(
        _ada_kernel,
        grid=(L, D6 // tn),
        in_specs=[
            pl.BlockSpec((SUBLANES, D), lambda l, j: (0, 0)),
            pl.BlockSpec((1, D, tn), lambda l, j: (l, 0, j)),
            pl.BlockSpec((1, 1, tn), lambda l, j: (l, 0, j)),
        ],
        out_specs=pl.BlockSpec((1, SUBLANES, tn), lambda l, j: (l, 0, j)),
        out_shape=jax.ShapeDtypeStruct((L, SUBLANES, D6), F32),
        compiler_params=_cparams("parallel", "parallel"),
        name="ada_mods",
    )(cond, ada_w, ada_b.reshape(L, 1, D6))


def _in_kernel(x_ref, a_ref, s_ref, cos_ref, se_ref, so_ref, w_ref, qkv_ref, gla_ref, gz_ref, lru_ref):
    x = x_ref[...]
    ms = jnp.mean(x * x, axis=-1, keepdims=True)
    u = (x * lax.rsqrt(ms + RMS_EPS)) * a_ref[0] + s_ref[0]
    ub = u.astype(BF16)
    cos, se, so = cos_ref[...], se_ref[...], so_ref[...]
    n_rot = N_Q_HEADS + N_KV_HEADS
    for jp in range(n_rot // 2):
        z2 = jnp.dot(ub, w_ref[:, 2 * jp * HEAD_DIM:(2 * jp + 2) * HEAD_DIM], preferred_element_type=F32)
        for j in (2 * jp, 2 * jp + 1):
            zh = z2[:, (j - 2 * jp) * HEAD_DIM:(j - 2 * jp + 1) * HEAD_DIM]
            rot = zh * cos + pltpu.roll(zh, HEAD_DIM - 1, 1) * se + pltpu.roll(zh, 1, 1) * so
            qkv_ref[:, j * HEAD_DIM:(j + 1) * HEAD_DIM] = rot.astype(BF16)
    c0 = n_rot * HEAD_DIM
    qkv_ref[:, c0:QKV_W] = jnp.dot(ub, w_ref[:, c0:QKV_W], preferred_element_type=F32).astype(BF16)
    gla_ref[...] = jnp.dot(ub, w_ref[:, COL_GLA:COL_GZ], preferred_element_type=F32)
    gz_ref[...] = jnp.dot(ub, w_ref[:, COL_GZ:COL_LRU], preferred_element_type=F32)
    lru_ref[...] = jnp.dot(ub, w_ref[:, COL_LRU:W_IN_PACKED], preferred_element_type=F32)


def _in_proj(h, a, s, cos, se, so, w, seq_len):
    rows, D = h.shape
    tm = _tile(seq_len, 512)
    per = seq_len // tm
    row = lambda i: (i, 0)
    bat = lambda i: (i // per, 0, 0)
    tab = lambda i: (i % per, 0)
    return pl.pallas_call(
        _in_kernel,
        grid=(rows // tm,),
        in_specs=[
            pl.BlockSpec((tm, D), row),
            pl.BlockSpec((1, 1, D), bat),
            pl.BlockSpec((1, 1, D), bat),
            pl.BlockSpec((tm, HEAD_DIM), tab),
            pl.BlockSpec((tm, HEAD_DIM), tab),
            pl.BlockSpec((tm, HEAD_DIM), tab),
            pl.BlockSpec((D, W_IN_PACKED), lambda i: (0, 0), pipeline_mode=pl.Buffered(1)),
        ],
        out_specs=[
            pl.BlockSpec((tm, QKV_W), row),
            pl.BlockSpec((tm, GLA_W), row),
            pl.BlockSpec((tm, GZ_W), row),
            pl.BlockSpec((tm, LRU_W), row),
        ],
        out_shape=[
            jax.ShapeDtypeStruct((rows, QKV_W), BF16),
            jax.ShapeDtypeStruct((rows, GLA_W), F32),
            jax.ShapeDtypeStruct((rows, GZ_W), F32),
            jax.ShapeDtypeStruct((rows, LRU_W), F32),
        ],
        compiler_params=_cparams("parallel"),
        name="in_proj",
    )(h, a, s, cos, se, so, w)


def _attn_heads(q, k_loc, v_loc, valid, k_ctx, v_ctx, sink_ref, o_ref):
    scale = HEAD_DIM ** -0.5
    group = N_Q_HEADS // N_KV_HEADS
    nq = q.shape[0]
    nt = (((1,), (1,)), ((), ()))
    if valid is not None:
        valid = jnp.concatenate([valid] * group, axis=0)
    for hk in range(N_KV_HEADS):
        heads = range(hk * group, (hk + 1) * group)
        qg = jnp.concatenate([q[:, h * HEAD_DIM:(h + 1) * HEAD_DIM] for h in heads], axis=0)
        sink = jnp.concatenate([jnp.broadcast_to(sink_ref[h:h + 1, 0:1], (nq, 1)) for h in heads], axis=0)
        s_ctx = lax.dot_general(qg, k_ctx[hk], nt, preferred_element_type=F32) * scale
        m = jnp.maximum(jnp.max(s_ctx, axis=-1, keepdims=True), sink)
        if k_loc is not None:
            s_loc = lax.dot_general(qg, k_loc[hk], nt, preferred_element_type=F32) * scale
            s_loc = jnp.where(valid, s_loc, -jnp.inf)
            m = jnp.maximum(m, jnp.max(s_loc, axis=-1, keepdims=True))
        ov = jnp.dot(jnp.exp(s_ctx - m).astype(BF16), _with_ones(v_ctx[hk]), preferred_element_type=F32)
        if k_loc is not None:
            ov = ov + jnp.dot(jnp.exp(s_loc - m).astype(BF16), _with_ones(v_loc[hk]), preferred_element_type=F32)
        den = ov[:, HEAD_DIM:HEAD_DIM + 1] + jnp.exp(sink - m)
        o = (ov[:, :HEAD_DIM] * (1.0 / den)).astype(o_ref.dtype)
        for n, h in enumerate(heads):
            o_ref[:, h * HEAD_DIM:(h + 1) * HEAD_DIM] = o[n * nq:(n + 1) * nq]


def _with_ones(v):
    return jnp.concatenate([v, jnp.ones_like(v)], axis=1)


def _split_kv(kv):
    ks = [kv[:, h * HEAD_DIM:(h + 1) * HEAD_DIM] for h in range(N_KV_HEADS)]
    vs = [kv[:, ATTN_KV + h * HEAD_DIM:ATTN_KV + (h + 1) * HEAD_DIM] for h in range(N_KV_HEADS)]
    return ks, vs


def _attn_lat_kernel(q_ref, kvp_ref, kvc_ref, kvn_ref, kvx_ref, sink_ref, o_ref, *, nb):
    n = pl.program_id(1)
    kp, vp = _split_kv(kvp_ref[...])
    kc, vc = _split_kv(kvc_ref[...])
    kn, vn = _split_kv(kvn_ref[...])
    k_ctx, v_ctx = _split_kv(kvx_ref[...])
    k_loc = [jnp.concatenate([kp[h], kc[h], kn[h]], axis=0) for h in range(N_KV_HEADS)]
    v_loc = [jnp.concatenate([vp[h], vc[h], vn[h]], axis=0) for h in range(N_KV_HEADS)]
    i = lax.broadcasted_iota(I32, (ATTN_BLOCK, 3 * ATTN_BLOCK), 0)
    j = lax.broadcasted_iota(I32, (ATTN_BLOCK, 3 * ATTN_BLOCK), 1)
    valid = (j >= i) & (j <= i + 2 * WINDOW)
    valid = valid & ((j >= ATTN_BLOCK) | (n > 0)) & ((j < 2 * ATTN_BLOCK) | (n < nb - 1))
    _attn_heads(q_ref[...], k_loc, v_loc, valid, k_ctx, v_ctx, sink_ref, o_ref)


def _attn_ctx_kernel(q_ref, kvx_ref, sink_ref, o_ref):
    k_ctx, v_ctx = _split_kv(kvx_ref[...])
    _attn_heads(q_ref[...], None, None, None, k_ctx, v_ctx, sink_ref, o_ref)


def _attn_latent(qkv_l, qkv_c, sink_b, B, S, C):
    nb = S // ATTN_BLOCK
    kvw = 2 * ATTN_KV
    kvcol = ATTN_Q // kvw
    return pl.pallas_call(
        functools.partial(_attn_lat_kernel, nb=nb),
        grid=(B, nb),
        in_specs=[
            pl.BlockSpec((ATTN_BLOCK, ATTN_Q), lambda b, n: (b * nb + n, 0)),
            pl.BlockSpec((ATTN_BLOCK, kvw), lambda b, n: (b * nb + jnp.maximum(n - 1, 0), kvcol)),
            pl.BlockSpec((ATTN_BLOCK, kvw), lambda b, n: (b * nb + n, kvcol)),
            pl.BlockSpec((ATTN_BLOCK, kvw), lambda b, n: (b * nb + jnp.minimum(n + 1, nb - 1), kvcol)),
            pl.BlockSpec((C, kvw), lambda b, n: (b, kvcol)),
            pl.BlockSpec((SUBLANES, LANES), lambda b, n: (0, 0)),
        ],
        out_specs=pl.BlockSpec((ATTN_BLOCK, ATTN_Q), lambda b, n: (b * nb + n, 0)),
        out_shape=jax.ShapeDtypeStruct((B * S, ATTN_Q), BF16),
        compiler_params=_cparams("parallel", "parallel"),
        name="attn_latent",
    )(qkv_l, qkv_l, qkv_l, qkv_l, qkv_c, sink_b)


def _attn_context(qkv_c, sink_b, B, C):
    nb = C // ATTN_BLOCK
    kvw = 2 * ATTN_KV
    kvcol = ATTN_Q // kvw
    return pl.pallas_call(
        _attn_ctx_kernel,
        grid=(B, nb),
        in_specs=[
            pl.BlockSpec((ATTN_BLOCK, ATTN_Q), lambda b, n: (b * nb + n, 0)),
            pl.BlockSpec((C, kvw), lambda b, n: (b, kvcol)),
            pl.BlockSpec((SUBLANES, LANES), lambda b, n: (0, 0)),
        ],
        out_specs=pl.BlockSpec((ATTN_BLOCK, ATTN_Q), lambda b, n: (b * nb + n, 0)),
        out_shape=jax.ShapeDtypeStruct((B * C, ATTN_Q), BF16),
        compiler_params=_cparams("parallel", "parallel"),
        name="attn_context",
    )(qkv_c, qkv_c, sink_b)


def _gla_constants():
    Lc = GLA_CHUNK
    e = np.zeros((GLA_LEVELS + 2, Lc, Lc), np.float32)
    msk = np.zeros((GLA_LEVELS + 1, Lc, Lc), np.float32)
    t = np.arange(Lc)
    for l in range(GLA_LEVELS):
        m = 1 << l
        blk = t // (2 * m)
        upper = (t % (2 * m)) >= m
        bnd = blk * 2 * m + m
        r = t[None, :]
        eq = upper[:, None] & (r >= bnd[:, None]) & (r <= t[:, None])
        ek = (~upper)[:, None] & (r > t[:, None]) & (r <= bnd[:, None] - 1)
        e[l] = (eq | ek).astype(np.float32)
        msk[l] = (upper[:, None] & (~upper)[None, :] & (blk[:, None] == blk[None, :])).astype(np.float32)
    e[GLA_LEVELS] = (t[None, :] <= t[:, None]).astype(np.float32)
    e[GLA_LEVELS + 1] = (t[None, :] > t[:, None]).astype(np.float32)
    msk[GLA_LEVELS] = np.eye(Lc, dtype=np.float32)
    e2 = np.stack([e, e[:, ::-1, ::-1]]).reshape(2, (GLA_LEVELS + 2) * Lc, Lc)
    m2 = np.stack([msk, msk[:, ::-1, ::-1]])
    return e2, m2


def _gla_kernel(x_ref, gz_ref, w2_ref, bias_ref, e_ref, m_ref, s0_ref, o_ref, sfin_ref, s_scr, *, nch):
    i = pl.program_id(2)
    Lc = GLA_CHUNK

    @pl.when(i == 0)
    def _():
        s_scr[...] = s0_ref[0, 0]

    z = jnp.dot(gz_ref[...].astype(BF16), w2_ref[0], preferred_element_type=F32) + bias_ref[0]
    la = (jnp.minimum(z, 0.0) - jnp.log1p(jnp.exp(-jnp.abs(z)))) * (1.0 / GLA_TAU)
    la_hi = la.astype(BF16)
    la_lo = (la - la_hi.astype(F32)).astype(BF16)
    la2 = jnp.concatenate([la_hi, la_lo], axis=1)
    ex = jnp.dot(e_ref[0], la2, preferred_element_type=F32)
    decay = jnp.exp(ex[:, :GLA_QK] + ex[:, GLA_QK:])
    tot = lax.dot_general(la2, jnp.ones((Lc, LANES), BF16), (((0,), (0,)), ((), ())), preferred_element_type=F32)
    a_tot = jnp.exp(tot[:GLA_QK] + tot[GLA_QK:])

    q = x_ref[:, 0:GLA_QK] * (GLA_DK ** -0.5)
    k = x_ref[:, GLA_QK:2 * GLA_QK]
    v = x_ref[:, 2 * GLA_QK:2 * GLA_QK + GLA_V].astype(BF16)
    lane_head = lax.broadcasted_iota(I32, (Lc, GLA_QK), 1) >> 6
    row_head = lax.broadcasted_iota(I32, (GLA_QK, GLA_DV), 0) >> 6
    nt = (((1,), (1,)), ((), ()))

    att = [jnp.zeros((Lc, Lc), F32) for _ in range(GLA_HEADS)]
    for l in range(GLA_LEVELS + 1):
        if l < GLA_LEVELS:
            dl = decay[l * Lc:(l + 1) * Lc]
            ql = (q * dl).astype(BF16)
            kl = (k * dl).astype(BF16)
        else:
            ql = q.astype(BF16)
            kl = k.astype(BF16)
        ml = m_ref[0, l]
        for h in range(GLA_HEADS):
            kh = jnp.where(lane_head == h, kl, jnp.zeros_like(kl))
            att[h] = att[h] + ml * lax.dot_general(ql, kh, nt, preferred_element_type=F32)

    s_old = s_scr[...]
    s_b = s_old.astype(BF16)
    q_in = (q * decay[GLA_LEVELS * Lc:(GLA_LEVELS + 1) * Lc]).astype(BF16)
    k_out = (k * decay[(GLA_LEVELS + 1) * Lc:(GLA_LEVELS + 2) * Lc]).astype(BF16)
    s_new = a_tot * s_old
    for h in range(GLA_HEADS):
        vh = v[:, h * GLA_DV:(h + 1) * GLA_DV]
        qh = jnp.where(lane_head == h, q_in, jnp.zeros_like(q_in))
        o_h = jnp.dot(att[h].astype(BF16), vh, preferred_element_type=F32)
        o_h = o_h + jnp.dot(qh, s_b, preferred_element_type=F32)
        o_ref[0, :, h * GLA_DV:(h + 1) * GLA_DV] = o_h
        c_h = lax.dot_general(k_out, vh, (((0,), (0,)), ((), ())), preferred_element_type=F32)
        s_new = s_new + jnp.where(row_head == h, c_h, 0.0)
    s_scr[...] = s_new

    @pl.when(i == nch - 1)
    def _():
        sfin_ref[0, 0] = s_new


def _gla(gla_arr, gz_arr, w2p, bias, e2, m2, s0, B, T):
    nch = T // GLA_CHUNK
    blk = lambda b, d, i: b * nch + jnp.where(d == 0, i, nch - 1 - i)
    st = lambda b, d, i: (b, d, 0, 0)
    return pl.pallas_call(
        functools.partial(_gla_kernel, nch=nch),
        grid=(B, 2, nch),
        in_specs=[
            pl.BlockSpec((GLA_CHUNK, 2 * GLA_QK + GLA_V), lambda b, d, i: (blk(b, d, i), 0)),
            pl.BlockSpec((GLA_CHUNK, GZ_W), lambda b, d, i: (blk(b, d, i), 0)),
            pl.BlockSpec((1, GZ_W, GLA_QK), lambda b, d, i: (d, 0, 0)),
            pl.BlockSpec((1, 1, GLA_QK), lambda b, d, i: (d, 0, 0)),
            pl.BlockSpec((1, (GLA_LEVELS + 2) * GLA_CHUNK, GLA_CHUNK), lambda b, d, i: (d, 0, 0)),
            pl.BlockSpec((1, GLA_LEVELS + 1, GLA_CHUNK, GLA_CHUNK), lambda b, d, i: (d, 0, 0, 0)),
            pl.BlockSpec((1, 1, GLA_QK, GLA_DV), st),
        ],
        out_specs=[
            pl.BlockSpec((1, GLA_CHUNK, GLA_V), lambda b, d, i: (d, blk(b, d, i), 0)),
            pl.BlockSpec((1, 1, GLA_QK, GLA_DV), st),
        ],
        out_shape=[
            jax.ShapeDtypeStruct((2, B * T, GLA_V), F32),
            jax.ShapeDtypeStruct((B, 2, GLA_QK, GLA_DV), F32),
        ],
        scratch_shapes=[pltpu.VMEM((GLA_QK, GLA_DV), F32)],
        compiler_params=_cparams("parallel", "parallel", "arbitrary"),
        name="gla_scan",
    )(gla_arr, gz_arr, w2p, bias, e2, m2, s0)


def _lru_kernel(x_ref, pv_ref, nx_ref, cw_ref, cb_ref, wg_ref, bg_ref, lam_ref, h0_ref, o_ref, hfin_ref, carry, *, nblk):
    d = pl.program_id(1)
    i = pl.program_id(2)
    T = LRU_CHUNK
    W = LRU_WIDTH

    @pl.when(i == 0)
    def _():
        carry[...] = jnp.broadcast_to(h0_ref[0, 0], (SUBLANES, W))

    li = jnp.where(d == 0, i, nblk - 1 - i)
    pv = jnp.where(li > 0, pv_ref[...], 0.0)
    nx = jnp.where(li < nblk - 1, nx_ref[...], 0.0)
    xe = jnp.concatenate([pv, x_ref[...], nx], axis=0)
    n_ext = T + 2 * SUBLANES
    win = lambda off: pltpu.roll(xe, n_ext - off, 0)[0:T]
    cw = cw_ref[...]
    xc = cb_ref[...] + win(6) * cw[0:1] + win(7) * cw[1:2] + xe[SUBLANES:SUBLANES + T] * cw[2:3] + win(9) * cw[3:4]

    g = jnp.dot(xc.astype(BF16), wg_ref[0], preferred_element_type=F32) + bg_ref[0]
    r = jax.nn.sigmoid(g[:, :W])
    gi = jax.nn.sigmoid(g[:, W:])
    log_a = (-LRU_C * _softplus(-lam_ref[0])) * r
    a = jnp.exp(log_a)
    u = jnp.sqrt(-jnp.tanh(log_a) * (a * a + 1.0)) * (gi * xc)
    groups = T // SUBLANES
    a3 = a.reshape(groups, SUBLANES, W)
    u3 = u.reshape(groups, SUBLANES, W)
    sub_row = lax.broadcasted_iota(I32, (groups, SUBLANES, W), 1)

    @pl.when(d == 0)
    def _():
        aa, uu = a3, u3
        sh = 1
        while sh < SUBLANES:
            ok = sub_row >= sh
            uu = uu + aa * jnp.where(ok, pltpu.roll(uu, sh, 1), 0.0)
            aa = aa * jnp.where(ok, pltpu.roll(aa, sh, 1), 1.0)
            sh *= 2
        c = carry[0:1]
        for g in range(groups):
            hg = uu[g] + aa[g] * c
            o_ref[0, g * SUBLANES:(g + 1) * SUBLANES, :] = hg
            c = hg[SUBLANES - 1:SUBLANES]
        carry[...] = jnp.broadcast_to(c, (SUBLANES, W))

    @pl.when(d == 1)
    def _():
        aa, uu = a3, u3
        sh = 1
        while sh < SUBLANES:
            ok = sub_row < SUBLANES - sh
            uu = uu + aa * jnp.where(ok, pltpu.roll(uu, SUBLANES - sh, 1), 0.0)
            aa = aa * jnp.where(ok, pltpu.roll(aa, SUBLANES - sh, 1), 1.0)
            sh *= 2
        c = carry[0:1]
        for g in reversed(range(groups)):
            hg = uu[g] + aa[g] * c
            o_ref[0, g * SUBLANES:(g + 1) * SUBLANES, :] = hg
            c = hg[0:1]
        carry[...] = jnp.broadcast_to(c, (SUBLANES, W))

    @pl.when(i == nblk - 1)
    def _():
        hfin_ref[0, 0] = carry[0:1]


def _lru(lru_arr, cw, cb, wg, bg, lam, h0, B, T):
    nblk = T // LRU_CHUNK
    per8 = LRU_CHUNK // SUBLANES
    n8 = B * T // SUBLANES
    blk = lambda b, d, i: b * nblk + jnp.where(d == 0, i, nblk - 1 - i)
    dirw = lambda b, d, i: (d, 0, 0)
    st = lambda b, d, i: (b, d, 0, 0)
    return pl.pallas_call(
        functools.partial(_lru_kernel, nblk=nblk),
        grid=(B, 2, nblk),
        in_specs=[
            pl.BlockSpec((LRU_CHUNK, LRU_WIDTH), lambda b, d, i: (blk(b, d, i), 0)),
            pl.BlockSpec((SUBLANES, LRU_WIDTH), lambda b, d, i: (jnp.maximum(blk(b, d, i) * per8 - 1, 0), 0)),
            pl.BlockSpec((SUBLANES, LRU_WIDTH), lambda b, d, i: (jnp.minimum((blk(b, d, i) + 1) * per8, n8 - 1), 0)),
            pl.BlockSpec((SUBLANES, LRU_WIDTH), lambda b, d, i: (0, 0)),
            pl.BlockSpec((1, LRU_WIDTH), lambda b, d, i: (0, 0)),
            pl.BlockSpec((1, LRU_WIDTH, 2 * LRU_WIDTH), dirw),
            pl.BlockSpec((1, 1, 2 * LRU_WIDTH), dirw),
            pl.BlockSpec((1, 1, LRU_WIDTH), dirw),
            pl.BlockSpec((1, 1, 1, LRU_WIDTH), st),
        ],
        out_specs=[
            pl.BlockSpec((1, LRU_CHUNK, LRU_WIDTH), lambda b, d, i: (d, blk(b, d, i), 0)),
            pl.BlockSpec((1, 1, 1, LRU_WIDTH), st),
        ],
        out_shape=[
            jax.ShapeDtypeStruct((2, B * T, LRU_WIDTH), F32),
            jax.ShapeDtypeStruct((B, 2, 1, LRU_WIDTH), F32),
        ],
        scratch_shapes=[pltpu.VMEM((SUBLANES, LRU_WIDTH), F32)],
        compiler_params=_cparams("parallel", "parallel", "arbitrary"),
        name="lru_scan",
    )(lru_arr, lru_arr, lru_arr, cw, cb, wg, bg, lam, h0)


def _out_kernel(h_ref, attn_ref, go_ref, gr_ref, gg_ref, lh_ref, lg_ref, w_ref, g1_ref, a2_ref, s2_ref, wr_ref, br_ref,
                vt_ref, lt_ref, hn_ref, v_ref, lgt_ref, *, n_main):
    i = pl.program_id(0)

    @pl.when(i < n_main)
    def _():
        _out_body(h_ref, attn_ref, go_ref, gr_ref, gg_ref, lh_ref, lg_ref, w_ref, g1_ref, a2_ref, s2_ref, wr_ref, br_ref,
                  hn_ref, v_ref, lgt_ref)

    @pl.when(i >= n_main)
    def _():
        v_ref[...] = vt_ref[...]
        lgt_ref[...] = lt_ref[...]


def _out_body(h_ref, attn_ref, go_ref, gr_ref, gg_ref, lh_ref, lg_ref, w_ref, g1_ref, a2_ref, s2_ref, wr_ref, br_ref,
              hn_ref, v_ref, lgt_ref):
    o = go_ref[0] + go_ref[1]
    gr = gr_ref[...]
    gate = gr * jax.nn.sigmoid(gr)
    parts = []
    for hh in range(GLA_HEADS):
        oh = o[:, hh * GLA_DV:(hh + 1) * GLA_DV]
        y = (oh * lax.rsqrt(jnp.mean(oh * oh, axis=-1, keepdims=True) + RMS_EPS)) * gg_ref[...]
        parts.append(y * gate[:, hh * GLA_DV:(hh + 1) * GLA_DV])
    gla = jnp.concatenate(parts, axis=1).astype(BF16)
    lg = lg_ref[...]
    gelu = lg * (0.5 * (1.0 + jnp.tanh(np.sqrt(2.0 / np.pi).astype(np.float32) * (lg + 0.044715 * (lg * lg * lg)))))
    lru = ((lh_ref[0] + lh_ref[1]) * gelu).astype(BF16)
    y = jnp.dot(attn_ref[...], w_ref[0:ATTN_Q], preferred_element_type=F32)
    y = y + jnp.dot(gla, w_ref[ATTN_Q:ATTN_Q + GLA_V], preferred_element_type=F32)
    y = y + jnp.dot(lru, w_ref[ATTN_Q + GLA_V:D_MIX], preferred_element_type=F32)
    hn = h_ref[...] + g1_ref[0] * y
    hn_ref[...] = hn
    v = (hn * lax.rsqrt(jnp.mean(hn * hn, axis=-1, keepdims=True) + RMS_EPS)) * a2_ref[0] + s2_ref[0]
    _store_token_tiles(v_ref, _pack_bf16_pairs(v))
    v_hi = v.astype(BF16)
    v_lo = (v - v_hi.astype(F32)).astype(BF16)
    t = jnp.dot(v_hi, wr_ref[...], preferred_element_type=F32)
    t_lo = jnp.dot(v_lo, wr_ref[:, 0:LANES], preferred_element_type=F32)
    lgt_ref[...] = (t[:, :LANES] + t[:, LANES:]) + t_lo + br_ref[...]


def _mixer_out(h, attn, go, gla_arr, gg, lh, lru_arr, w_out, g1, a2, s2, wr, br, seq_len, tail):
    rows, D = h.shape
    tm = _tile(seq_len, 256)
    per = seq_len // tm
    n_main = rows // tm
    tw = tm * (D // 2 // LANES)
    if tail is None:
        tail = (jnp.zeros((tw, LANES), I32), jnp.zeros((tm, LANES), F32))
        n_tail = 0
    else:
        assert tail[1].shape[0] % tm == 0
        n_tail = tail[1].shape[0] // tm
    n_tok = rows + n_tail * tm
    main = lambda i: jnp.minimum(i, n_main - 1)
    row = lambda i: (main(i), 0)
    row3 = lambda i: (0, main(i), 0)
    bat = lambda i: (main(i) // per, 0, 0)
    const = lambda i: (0, 0)
    tok = lambda i: (i, 0)
    trow = lambda i: (jnp.maximum(i - n_main, 0), 0)
    in_specs = [
        pl.BlockSpec((tm, D), row),
        pl.BlockSpec((tm, ATTN_Q), row),
        pl.BlockSpec((2, tm, GLA_V), row3),
        pl.BlockSpec((tm, GLA_V), lambda i: (main(i), 2)),
        pl.BlockSpec((1, GLA_DV), const),
        pl.BlockSpec((2, tm, LRU_WIDTH), row3),
        pl.BlockSpec((tm, LRU_WIDTH), lambda i: (main(i), 1)),
        pl.BlockSpec((D_MIX, D), const, pipeline_mode=pl.Buffered(1)),
        pl.BlockSpec((1, 1, D), bat),
        pl.BlockSpec((1, 1, D), bat),
        pl.BlockSpec((1, 1, D), bat),
        pl.BlockSpec((D, 2 * LANES), const),
        pl.BlockSpec((1, LANES), const),
        pl.BlockSpec((tw, LANES), trow),
        pl.BlockSpec((tm, LANES), trow),
    ]
    args = [h, attn, go, gla_arr, gg, lh, lru_arr, w_out, g1, a2, s2, wr, br, tail[0], tail[1]]
    return pl.pallas_call(
        functools.partial(_out_kernel, n_main=n_main),
        grid=(n_main + n_tail,),
        in_specs=in_specs,
        out_specs=[
            pl.BlockSpec((tm, D), row),
            pl.BlockSpec((tw, LANES), tok),
            pl.BlockSpec((tm, LANES), tok),
        ],
        out_shape=[
            jax.ShapeDtypeStruct((rows, D), F32),
            jax.ShapeDtypeStruct((n_tok * (tw // tm), LANES), I32),
            jax.ShapeDtypeStruct((n_tok, LANES), F32),
        ],
        compiler_params=_cparams("arbitrary"),
        name="mixer_out",
    )(*args)


def _route_kernel(lg_ref, tri_ref, oi_ref, of_ref, cnt_ref, carry):
    i = pl.program_id(0)

    @pl.when(i == 0)
    def _():
        carry[...] = jnp.zeros_like(carry)

    lg = lg_ref[...]
    col = lax.broadcasted_iota(I32, lg.shape, 1)
    colf = col.astype(F32)
    big = float(LANES)
    is_g = col < N_GROUPS
    gm = jnp.max(jnp.where(is_g, lg, -jnp.inf), axis=-1, keepdims=True)
    eg = jnp.where(is_g, jnp.exp(lg - gm), 0.0)
    pg = eg / jnp.sum(eg, axis=-1, keepdims=True)
    p_grp = jnp.max(pg, axis=-1, keepdims=True)
    grp = jnp.min(jnp.where(is_g & (pg == p_grp), colf, big), axis=-1, keepdims=True).astype(I32)

    sel = (col >= N_GROUPS) & (col < N_GROUPS + N_EXPERTS) & (((col - N_GROUPS) >> 3) == grp)
    em = jnp.max(jnp.where(sel, lg, -jnp.inf), axis=-1, keepdims=True)
    ee = jnp.where(sel, jnp.exp(lg - em), 0.0)
    pe = ee / jnp.sum(ee, axis=-1, keepdims=True)
    p1 = jnp.max(jnp.where(sel, pe, -1.0), axis=-1, keepdims=True)
    c1 = jnp.min(jnp.where(sel & (pe == p1), colf, big), axis=-1, keepdims=True).astype(I32)
    rest = sel & (col != c1)
    p2 = jnp.max(jnp.where(rest, pe, -1.0), axis=-1, keepdims=True)
    c2 = jnp.min(jnp.where(rest & (pe == p2), colf, big), axis=-1, keepdims=True).astype(I32)
    e1 = c1 - N_GROUPS
    e2 = c2 - N_GROUPS
    den = p1 + p2
    g1 = p_grp * (p1 / den)
    g2 = p_grp * (p2 / den)

    hit1 = col == e1
    hit2 = col == e2
    oh = jnp.where(hit1 | hit2, 1.0, 0.0)
    before = jnp.dot(tri_ref[...], oh.astype(BF16), preferred_element_type=F32) + carry[0:1]
    r1 = jnp.sum(jnp.where(hit1, before, 0.0), axis=-1, keepdims=True).astype(I32)
    r2 = jnp.sum(jnp.where(hit2, before, 0.0), axis=-1, keepdims=True).astype(I32)
    new = carry[0:1] + jnp.sum(oh, axis=0, keepdims=True)
    carry[...] = jnp.broadcast_to(new, carry.shape)
    cnt_ref[...] = jnp.broadcast_to(new, cnt_ref.shape)

    zero = jnp.zeros_like(col)
    oi_ref[...] = jnp.where(col == 0, e1, jnp.where(col == 1, e2, jnp.where(col == 2, r1, jnp.where(col == 3, r2, zero))))
    of_ref[...] = jnp.where(col == 0, g1, jnp.where(col == 1, g2, 0.0))


def _route(logits):
    n_tok = logits.shape[0]
    tm = _tile(n_tok, 1024)
    row = lambda i: (i, 0)
    tri = jnp.tril(jnp.ones((tm, tm), BF16), -1)
    return pl.pallas_call(
        _route_kernel,
        grid=(n_tok // tm,),
        in_specs=[pl.BlockSpec((tm, LANES), row), pl.BlockSpec((tm, tm), lambda i: (0, 0))],
        out_specs=[
            pl.BlockSpec((tm, LANES), row),
            pl.BlockSpec((tm, LANES), row),
            pl.BlockSpec((SUBLANES, LANES), lambda i: (0, 0)),
        ],
        out_shape=[
            jax.ShapeDtypeStruct((n_tok, LANES), I32),
            jax.ShapeDtypeStruct((n_tok, LANES), F32),
            jax.ShapeDtypeStruct((SUBLANES, LANES), F32),
        ],
        scratch_shapes=[pltpu.VMEM((SUBLANES, LANES), F32)],
        compiler_params=_cparams("arbitrary"),
        name="route",
    )(logits, tri)


def _cast_rows(src, dst):
    step = _tile(dst.shape[0], 256)

    def body(r, c):
        rows = pl.ds(pl.multiple_of(r * step, step), step)
        dst[rows, :] = src[rows, :].astype(BF16)
        return c

    lax.fori_loop(0, dst.shape[0] // step, body, 0)


def _moe_ffn_kernel(be_ref, nu_ref, nx_ref, src_ref, srcn_ref, v_hbm, w1_hbm, w3_hbm, w2_hbm, o_ref,
                    xbuf, st1, st3, st2, w1s, w3s, w2s, sem_x, sem_w, *, layer, sub):
    i = pl.program_id(0)
    nu = nu_ref[0]
    rows = ROW_BLOCK * sub

    def gather(idx_ref, slot):
        for r in range(ROW_BLOCK):
            s = pl.multiple_of(idx_ref[0, 0, r] * sub, sub)
            pltpu.make_async_copy(v_hbm.at[pl.ds(s, sub)], xbuf.at[slot, pl.ds(r * sub, sub)], sem_x.at[slot]).start()

    def wait_rows(slot):
        pltpu.make_async_copy(v_hbm.at[pl.ds(0, rows)], xbuf.at[slot], sem_x.at[slot]).wait()

    def weight_copies(e):
        return (pltpu.make_async_copy(w1_hbm.at[layer, e], st1, sem_w.at[0]),
                pltpu.make_async_copy(w3_hbm.at[layer, e], st3, sem_w.at[1]),
                pltpu.make_async_copy(w2_hbm.at[layer, e], st2, sem_w.at[2]))

    @pl.when(i < nu)
    def _():
        e = be_ref[i]
        slot = i & 1

        @pl.when(i == 0)
        def _():
            for cp in weight_copies(e):
                cp.start(priority=1)
            gather(src_ref, 0)

        @pl.when((i == 0) | (e != be_ref[jnp.maximum(i - 1, 0)]))
        def _():
            for cp in weight_copies(e):
                cp.wait()
            _cast_rows(st1, w1s)
            _cast_rows(st3, w3s)
            _cast_rows(st2, w2s)

            @pl.when(nx_ref[i] >= 0)
            def _():
                for cp in weight_copies(nx_ref[i]):
                    cp.start(priority=1)

        wait_rows(slot)
        gather(srcn_ref, 1 - slot)
        xb = _unpack_bf16_pairs(_load_token_tiles(xbuf.at[slot], ROW_BLOCK)).astype(BF16)
        a = jnp.dot(xb, w1s[...], preferred_element_type=F32)
        b = jnp.dot(xb, w3s[...], preferred_element_type=F32)
        hmid = ((a * jax.nn.sigmoid(a)) * b).astype(BF16)
        _store_token_tiles(o_ref, _pack_bf16_pairs(jnp.dot(hmid, w2s[...], preferred_element_type=F32)))

        @pl.when(i == nu - 1)
        def _():
            wait_rows(1 - slot)

    @pl.when(i >= nu)
    def _():
        o_ref[...] = jnp.zeros_like(o_ref)


def _moe_experts(v_tiles, src, blk_e, n_used, nxt_e, w1, w3, w2, layer):
    D, FF = w1.shape[-2:]
    sub = D // 2 // LANES
    P = src.shape[0]
    nb = P // ROW_BLOCK
    tw = ROW_BLOCK * sub
    cur = lambda i, be, nu, nx: (jnp.minimum(i, nu[0] - 1), 0, 0)
    nxt = lambda i, be, nu, nx: (jnp.minimum(i + 1, nu[0] - 1), 0, 0)
    src3 = src.reshape(nb, 1, ROW_BLOCK)
    hbm = pl.BlockSpec(memory_space=pl.ANY)
    return pl.pallas_call(
        functools.partial(_moe_ffn_kernel, layer=layer, sub=sub),
        grid_spec=pltpu.PrefetchScalarGridSpec(
            num_scalar_prefetch=3,
            grid=(nb,),
            in_specs=[
                pl.BlockSpec((1, 1, ROW_BLOCK), cur, memory_space=pltpu.SMEM),
                pl.BlockSpec((1, 1, ROW_BLOCK), nxt, memory_space=pltpu.SMEM),
                hbm, hbm, hbm, hbm,
            ],
            out_specs=pl.BlockSpec((tw, LANES), lambda i, be, nu, nx: (i, 0)),
            scratch_shapes=[
                pltpu.VMEM((2, tw, LANES), I32),
                pltpu.VMEM((D, FF), F32), pltpu.VMEM((D, FF), F32), pltpu.VMEM((FF, D), F32),
                pltpu.VMEM((D, FF), BF16), pltpu.VMEM((D, FF), BF16), pltpu.VMEM((FF, D), BF16),
                pltpu.SemaphoreType.DMA((2,)), pltpu.SemaphoreType.DMA((3,)),
            ],
        ),
        out_shape=jax.ShapeDtypeStruct((P * sub, LANES), I32),
        compiler_params=_cparams("arbitrary"),
        name="moe_ffn",
    )(blk_e, n_used, nxt_e, src3, src3, v_tiles, w1, w3, w2)


def _combine_kernel(dest_ref, destn_ref, h_ref, gate_ref, g2_ref, fg_ref, ys_ref, o_ref, buf, sem, *, final, nsteps):
    i = pl.program_id(0)
    slot = i & 1
    tm = h_ref.shape[0]
    rows = buf.shape[2]
    sub = rows // tm

    def gather(idx_ref, sl):
        def start(j, c):
            src = pl.multiple_of(idx_ref[0, 0, j] * sub, sub)
            dst = pl.multiple_of((j >> 1) * sub, sub)
            pltpu.make_async_copy(ys_ref.at[pl.ds(src, sub)], buf.at[sl, j & 1, pl.ds(dst, sub)], sem.at[sl]).start()
            return c

        lax.fori_loop(0, tm * TOP_K, start, 0, unroll=8)

    def wait_rows(sl):
        for kk in range(TOP_K):
            pltpu.make_async_copy(ys_ref.at[pl.ds(0, rows)], buf.at[sl, kk], sem.at[sl]).wait()

    @pl.when(i == 0)
    def _():
        gather(dest_ref, 0)

    gather(destn_ref, 1 - slot)
    wait_rows(slot)
    gate = gate_ref[...]
    f = (_unpack_bf16_pairs(_load_token_tiles(buf.at[slot, 0], tm)) * gate[:, 0:1]
         + _unpack_bf16_pairs(_load_token_tiles(buf.at[slot, 1], tm)) * gate[:, 1:2])

    @pl.when(i == nsteps - 1)
    def _():
        wait_rows(1 - slot)

    hn = h_ref[...] + g2_ref[0] * f
    if final:
        hn = (hn * lax.rsqrt(jnp.mean(hn * hn, axis=-1, keepdims=True) + RMS_EPS)) * fg_ref[...]
    o_ref[...] = hn


def _combine(h, dest, gates, g2, fg, ys, seq_len, tok_off, final):
    rows, D = h.shape
    tm = _tile(seq_len, 256)
    per = seq_len // tm
    off = tok_off // tm
    n_blk_all = dest.shape[0] // (tm * TOP_K)
    nsteps = rows // tm
    dest3 = dest.reshape(n_blk_all, 1, tm * TOP_K)
    return pl.pallas_call(
        functools.partial(_combine_kernel, final=final, nsteps=nsteps),
        grid=(nsteps,),
        in_specs=[
            pl.BlockSpec((1, 1, tm * TOP_K), lambda i: (i + off, 0, 0), memory_space=pltpu.SMEM),
            pl.BlockSpec((1, 1, tm * TOP_K), lambda i: (jnp.minimum(i + 1, nsteps - 1) + off, 0, 0), memory_space=pltpu.SMEM),
            pl.BlockSpec((tm, D), lambda i: (i, 0)),
            pl.BlockSpec((tm, LANES), lambda i: (i + off, 0)),
            pl.BlockSpec((1, 1, D), lambda i: (i // per, 0, 0)),
            pl.BlockSpec((1, D), lambda i: (0, 0)),
            pl.BlockSpec(memory_space=pl.ANY),
        ],
        out_specs=pl.BlockSpec((tm, D), lambda i: (i, 0)),
        out_shape=jax.ShapeDtypeStruct((rows, D), F32),
        scratch_shapes=[pltpu.VMEM((2, TOP_K, tm * (D // 2 // LANES), LANES), I32), pltpu.SemaphoreType.DMA((2,))],
        compiler_params=_cparams("arbitrary"),
        name="combine",
    )(dest3, dest3, h, gates, g2, fg, ys)


def _rope_tables(S):
    rows = S // GRID_W
    row = jnp.repeat(jnp.arange(rows, dtype=F32), GRID_W)
    col = jnp.tile(jnp.arange(GRID_W, dtype=F32), rows)
    n_freq = HEAD_DIM // 4
    inv = ROPE_THETA ** (-jnp.arange(n_freq, dtype=F32) / n_freq)
    ang = jnp.concatenate([row[:, None] * inv, col[:, None] * inv], axis=-1)
    cos = jnp.repeat(jnp.cos(ang), 2, axis=1)
    sin = jnp.repeat(jnp.sin(ang), 2, axis=1)
    even = (jnp.arange(HEAD_DIM) % 2) == 0
    return cos, jnp.where(even, -sin, 0.0), jnp.where(even, 0.0, sin)


def _block_diag(w):
    n, c, _ = w.shape
    eye = jnp.eye(n, dtype=w.dtype)
    return (eye[:, None, :, None] * w[:, :, None, :]).reshape(n * c, n * c)


def _slot_plan(ids, counts_row, n_tok):
    e = ids[:, 0:TOP_K]
    rank = ids[:, TOP_K:2 * TOP_K]
    counts = counts_row[:N_EXPERTS].astype(I32)
    padded = (counts + ROW_BLOCK - 1) // ROW_BLOCK * ROW_BLOCK
    pad_end = jnp.cumsum(padded)
    pad_start = pad_end - padded
    dest = (pad_start[e] + rank).reshape(n_tok * TOP_K)
    n_blocks = -(-(n_tok * TOP_K + N_EXPERTS * (ROW_BLOCK - 1)) // ROW_BLOCK)
    starts = jnp.arange(n_blocks, dtype=I32) * ROW_BLOCK
    blk_e = jnp.minimum(jnp.sum(pad_end[None, :] <= starts[:, None], axis=1), N_EXPERTS - 1).astype(I32)
    n_used = (pad_end[-1] // ROW_BLOCK).astype(I32).reshape(1)
    run_next = pad_end[blk_e] // ROW_BLOCK
    nxt_e = jnp.where(run_next < n_used[0], blk_e[jnp.minimum(run_next, n_blocks - 1)], -1).astype(I32)
    src = jnp.zeros((n_blocks * ROW_BLOCK,), I32).at[dest].set(jnp.arange(n_tok * TOP_K, dtype=I32) // TOP_K)
    return dest, src, blk_e, n_used, nxt_e


def kernel(x, c, ctx, c_ctx, ada_w, ada_b, norm_mix_g, norm_ffn_g, w_in, attn_sink, gla_gate_w2, gla_gate_b, gla_norm_g, lru_conv_w, lru_conv_b, lru_wa, lru_ba, lru_wx, lru_bx, lru_lambda, w_out, router_g_w, router_g_b, router_e_w, router_e_b, moe_w1, moe_w3, moe_w2, final_norm_g):
    B, S, D = x.shape
    C = ctx.shape[1]
    L = ada_w.shape[0]
    assert S % max(ATTN_BLOCK, GLA_CHUNK, LRU_CHUNK, GRID_W) == 0 and C % max(ATTN_BLOCK, GLA_CHUNK, LRU_CHUNK) == 0
    assert B + 1 <= SUBLANES and D % (2 * LANES) == 0

    cond = jnp.concatenate([c, c_ctx[None], jnp.zeros((SUBLANES - B - 1, D), F32)], axis=0)
    mods = _ada_mods(cond, ada_w, ada_b).reshape(L, SUBLANES, 6, D)

    cos_l, se_l, so_l = _rope_tables(S)
    cos_c = jnp.ones((C, HEAD_DIM), F32)
    zero_c = jnp.zeros((C, HEAD_DIM), F32)
    e2_np, m2_np = _gla_constants()
    gla_e = jnp.asarray(e2_np, BF16)
    gla_m = jnp.asarray(m2_np, F32)

    h_lat = x.reshape(B * S, D)
    h_ctx = ctx.reshape(B * C, D)
    out = None
    for l in range(L):
        last = l == L - 1
        ml = mods[l, :B]
        mc = jnp.broadcast_to(mods[l, B][None], (B, 6, D))
        per_b = lambda m, j: m[:, j][:, None, :]
        a1_l, a1_c = [(1.0 + per_b(m, 1)) * norm_mix_g[l] for m in (ml, mc)]
        a2_l, a2_c = [(1.0 + per_b(m, 4)) * norm_ffn_g[l] for m in (ml, mc)]

        w = w_in[l]
        c_gz = ATTN_Q + 2 * ATTN_KV + 2 * GLA_QK + 2 * GLA_V
        w_packed = jnp.concatenate(
            [w[:, :c_gz], w[:, c_gz:c_gz + 2 * GLA_RANK], jnp.zeros((D, GZ_W - 2 * GLA_RANK), F32), w[:, c_gz + 2 * GLA_RANK:]],
            axis=1).astype(BF16)
        qkv_l, gla_l, gz_l, lru_l = _in_proj(h_lat, a1_l, per_b(ml, 0), cos_l, se_l, so_l, w_packed, S)
        qkv_c, gla_c, gz_c, lru_c = _in_proj(h_ctx, a1_c, per_b(mc, 0), cos_c, zero_c, zero_c, w_packed, C)

        sink_b = jnp.broadcast_to(attn_sink[l][:, None], (N_Q_HEADS, LANES)).astype(F32)
        attn_l = _attn_latent(qkv_l, qkv_c, sink_b, B, S, C)

        w2p = jnp.zeros((2, GZ_W, GLA_QK), F32)
        w2p = w2p.at[0, :GLA_RANK].set(gla_gate_w2[l, 0]).at[1, GLA_RANK:2 * GLA_RANK].set(gla_gate_w2[l, 1]).astype(BF16)
        gbias = gla_gate_b[l].reshape(2, 1, GLA_QK)
        s_zero = jnp.zeros((B, 2, GLA_QK, GLA_DV), F32)
        go_c, s_ctx = _gla(gla_c, gz_c, w2p, gbias, gla_e, gla_m, s_zero, B, C)
        go_l, _ = _gla(gla_l, gz_l, w2p, gbias, gla_e, gla_m, s_ctx, B, S)

        cw = jnp.concatenate([lru_conv_w[l], jnp.zeros((SUBLANES - CONV_W, LRU_WIDTH), F32)], axis=0)
        cb = lru_conv_b[l].reshape(1, LRU_WIDTH)
        wg = jnp.stack([jnp.concatenate([_block_diag(lru_wa[l, d]), _block_diag(lru_wx[l, d])], axis=1) for d in range(2)]).astype(BF16)
        bg = jnp.concatenate([lru_ba[l], lru_bx[l]], axis=1).reshape(2, 1, 2 * LRU_WIDTH)
        lam = lru_lambda[l].reshape(2, 1, LRU_WIDTH)
        h_zero = jnp.zeros((B, 2, 1, LRU_WIDTH), F32)
        lh_c, hs_ctx = _lru(lru_c, cw, cb, wg, bg, lam, h_zero, B, C)
        lh_l, _ = _lru(lru_l, cw, cb, wg, bg, lam, hs_ctx, B, S)

        wo = w_out[l].astype(BF16)
        gg = gla_norm_g[l].reshape(1, GLA_DV)
        wr = jnp.concatenate([router_g_w[l], router_e_w[l], jnp.zeros((D, LANES - N_GROUPS - N_EXPERTS), F32)], axis=1)
        wr_hi = wr.astype(BF16)
        wr = jnp.concatenate([wr_hi, (wr - wr_hi.astype(F32)).astype(BF16)], axis=1)
        br =jnp.concatenate([router_g_b[l], router_e_b[l], jnp.zeros((LANES - N_GROUPS - N_EXPERTS,), F32)]).reshape(1, LANES)
        n_tok = B * S if last else B * (S + C)
        tail = None
        if not last:
            attn_c = _attn_context(qkv_c, sink_b, B, C)
            h_ctx, v_c, lgt_c = _mixer_out(h_ctx, attn_c, go_c, gla_c, gg, lh_c, lru_c, wo, per_b(mc, 2), a2_c, per_b(mc, 3),
                                           wr, br, C, None)
            tail = (v_c, lgt_c)
        h_lat, v_all, lgt_all = _mixer_out(h_lat, attn_l, go_l, gla_l, gg, lh_l, lru_l, wo, per_b(ml, 2), a2_l, per_b(ml, 3),
                                           wr, br, S, tail)

        ids, gates, counts = _route(lgt_all)
        dest, src, blk_e, n_used, nxt_e = _slot_plan(ids, counts[0], n_tok)
        ys = _moe_experts(v_all, src, blk_e, n_used, nxt_e, moe_w1, moe_w3, moe_w2, l)
        fg = final_norm_g.reshape(1, D)
        h_lat = _combine(h_lat, dest, gates, per_b(ml, 5), fg, ys, S, 0, last)
        if not last:
            h_ctx = _combine(h_ctx, dest, gates, per_b(mc, 5), fg, ys, C, B * S, False)
        out = h_lat
    return out.reshape(B, S, D)
```

```python
import functools

import numpy as np
import jax
import jax.numpy as jnp
from jax import lax
from jax.experimental import pallas as pl
from jax.experimental.pallas import tpu as pltpu

F32 = jnp.float32
BF16 = jnp.bfloat16
I32 = jnp.int32

GRID_W = 64
RMS_EPS = 1e-6
N_Q_HEADS = 8
N_KV_HEADS = 2
HEAD_DIM = 128
WINDOW = 128
ATTN_BLOCK = 128
ROPE_THETA = 10000.0
GLA_HEADS = 4
GLA_DK = 64
GLA_DV = 128
GLA_RANK = 16
GLA_TAU = 16.0
LRU_WIDTH = 512
LRU_BLOCKS = 8
LRU_C = 8.0
CONV_W = 4
N_GROUPS = 4
EXPERTS_PER_GROUP = 8
N_EXPERTS = 32
TOP_K = 2
ATTN_Q = N_Q_HEADS * HEAD_DIM
ATTN_KV = N_KV_HEADS * HEAD_DIM
GLA_QK = GLA_HEADS * GLA_DK
GLA_V = GLA_HEADS * GLA_DV
D_MIX = ATTN_Q + GLA_V + LRU_WIDTH

LANES = 128
SUBLANES = 8
VMEM_LIMIT_BYTES = 56 * 1024 * 1024

QKV_W = ATTN_Q + 2 * ATTN_KV
GLA_W = 2 * GLA_QK + 2 * GLA_V
GZ_W = LANES
LRU_W = 2 * LRU_WIDTH
COL_GLA = QKV_W
COL_GZ = COL_GLA + GLA_W
COL_LRU = COL_GZ + GZ_W
W_IN_PACKED = COL_LRU + LRU_W

GLA_CHUNK = 128
GLA_LEVELS = 7
LRU_CHUNK = 128
ROW_BLOCK = 256


def _cparams(*sem):
    return pltpu.CompilerParams(dimension_semantics=sem, vmem_limit_bytes=VMEM_LIMIT_BYTES)


def _tile(n, pref):
    t = min(n, pref)
    while n % t:
        t -= SUBLANES
    return t


def _softplus(x):
    return jnp.maximum(x, 0.0) + jnp.log1p(jnp.exp(-jnp.abs(x)))


def _pack_bf16_pairs(x):
    n = x.shape[1] // 2
    lo = lax.bitcast_convert_type(x[:, :n].astype(BF16).astype(F32), I32)
    hi = lax.bitcast_convert_type(x[:, n:].astype(BF16).astype(F32), I32)
    return hi | lax.shift_right_logical(lo, jnp.full(lo.shape, 16, I32))


def _unpack_bf16_pairs(p):
    lo = lax.bitcast_convert_type(lax.shift_left(p, jnp.full(p.shape, 16, I32)), F32)
    hi = lax.bitcast_convert_type(p & jnp.int32(-65536), F32)
    return jnp.concatenate([lo, hi], axis=1)


def _store_token_tiles(ref, packed):
    m = packed.shape[0]
    sub = packed.shape[1] // LANES
    for s in range(sub):
        ref[pl.ds(s, m, stride=sub), :] = packed[:, s * LANES:(s + 1) * LANES]


def _load_token_tiles(ref, m):
    sub = ref.shape[0] // m
    return jnp.concatenate([ref[pl.ds(s, m, stride=sub), :] for s in range(sub)], axis=1)


def _ada_kernel(c_ref, w_ref, b_ref, o_ref):
    c = c_ref[...]
    s = c * jax.nn.sigmoid(c)
    o_ref[0] = jnp.dot(s.astype(BF16), w_ref[0].astype(BF16), preferred_element_type=F32) + b_ref[0]


def _ada_mods(cond, ada_w, ada_b):
    L, D, D6 = ada_w.shape
    tn = _tile(D6, 1536)
    while tn % LANES:
        tn -= SUBLANES
    return pl.pallas_call(
        _ada_kernel,
        grid=(L, D6 // tn),
        in_specs=[
            pl.BlockSpec((SUBLANES, D), lambda l, j: (0, 0)),
            pl.BlockSpec((1, D, tn), lambda l, j: (l, 0, j)),
            pl.BlockSpec((1, 1, tn), lambda l, j: (l, 0, j)),
        ],
        out_specs=pl.BlockSpec((1, SUBLANES, tn), lambda l, j: (l, 0, j)),
        out_shape=jax.ShapeDtypeStruct((L, SUBLANES, D6), F32),
        compiler_params=_cparams("parallel", "parallel"),
        name="ada_mods",
    )(cond, ada_w, ada_b.reshape(L, 1, D6))


def _in_kernel(x_ref, a_ref, s_ref, cos_ref, se_ref, so_ref, w_ref, qkv_ref, gla_ref, gz_ref, lru_ref):
    x = x_ref[...]
    ms = jnp.mean(x * x, axis=-1, keepdims=True)
    u = (x * lax.rsqrt(ms + RMS_EPS)) * a_ref[0] + s_ref[0]
    ub = u.astype(BF16)
    cos, se, so = cos_ref[...], se_ref[...], so_ref[...]
    n_rot = N_Q_HEADS + N_KV_HEADS
    for jp in range(n_rot // 2):
        z2 = jnp.dot(ub, w_ref[:, 2 * jp * HEAD_DIM:(2 * jp + 2) * HEAD_DIM], preferred_element_type=F32)
        for j in (2 * jp, 2 * jp + 1):
            zh = z2[:, (j - 2 * jp) * HEAD_DIM:(j - 2 * jp + 1) * HEAD_DIM]
            rot = zh * cos + pltpu.roll(zh, HEAD_DIM - 1, 1) * se + pltpu.roll(zh, 1, 1) * so
            qkv_ref[:, j * HEAD_DIM:(j + 1) * HEAD_DIM] = rot.astype(BF16)
    c0 = n_rot * HEAD_DIM
    qkv_ref[:, c0:QKV_W] = jnp.dot(ub, w_ref[:, c0:QKV_W], preferred_element_type=F32).astype(BF16)
    gla_ref[...] = jnp.dot(ub, w_ref[:, COL_GLA:COL_GZ], preferred_element_type=F32)
    gz_ref[...] = jnp.dot(ub, w_ref[:, COL_GZ:COL_LRU], preferred_element_type=F32)
    lru_ref[...] = jnp.dot(ub, w_ref[:, COL_LRU:W_IN_PACKED], preferred_element_type=F32)


def _in_proj(h, a, s, cos, se, so, w, seq_len):
    rows, D = h.shape
    tm = _tile(seq_len, 512)
    per = seq_len // tm
    row = lambda i: (i, 0)
    bat = lambda i: (i // per, 0, 0)
    tab = lambda i: (i % per, 0)
    return pl.pallas_call(
        _in_kernel,
        grid=(rows // tm,),
        in_specs=[
            pl.BlockSpec((tm, D), row),
            pl.BlockSpec((1, 1, D), bat),
            pl.BlockSpec((1, 1, D), bat),
            pl.BlockSpec((tm, HEAD_DIM), tab),
            pl.BlockSpec((tm, HEAD_DIM), tab),
            pl.BlockSpec((tm, HEAD_DIM), tab),
            pl.BlockSpec((D, W_IN_PACKED), lambda i: (0, 0), pipeline_mode=pl.Buffered(1)),
        ],
        out_specs=[
            pl.BlockSpec((tm, QKV_W), row),
            pl.BlockSpec((tm, GLA_W), row),
            pl.BlockSpec((tm, GZ_W), row),
            pl.BlockSpec((tm, LRU_W), row),
        ],
        out_shape=[
            jax.ShapeDtypeStruct((rows, QKV_W), BF16),
            jax.ShapeDtypeStruct((rows, GLA_W), F32),
            jax.ShapeDtypeStruct((rows, GZ_W), F32),
            jax.ShapeDtypeStruct((rows, LRU_W), F32),
        ],
        compiler_params=_cparams("parallel"),
        name="in_proj",
    )(h, a, s, cos, se, so, w)


def _attn_heads(q, k_loc, v_loc, valid, k_ctx, v_ctx, sink_ref, o_ref):
    scale = HEAD_DIM ** -0.5
    group = N_Q_HEADS // N_KV_HEADS
    nq = q.shape[0]
    nt = (((1,), (1,)), ((), ()))
    if valid is not None:
        valid = jnp.concatenate([valid] * group, axis=0)
    for hk in range(N_KV_HEADS):
        heads = range(hk * group, (hk + 1) * group)
        qg = jnp.concatenate([q[:, h * HEAD_DIM:(h + 1) * HEAD_DIM] for h in heads], axis=0)
        sink = jnp.concatenate([jnp.broadcast_to(sink_ref[h:h + 1, 0:1], (nq, 1)) for h in heads], axis=0)
        s_ctx = lax.dot_general(qg, k_ctx[hk], nt, preferred_element_type=F32) * scale
        m = jnp.maximum(jnp.max(s_ctx, axis=-1, keepdims=True), sink)
        if k_loc is not None:
            s_loc = lax.dot_general(qg, k_loc[hk], nt, preferred_element_type=F32) * scale
            s_loc = jnp.where(valid, s_loc, -jnp.inf)
            m = jnp.maximum(m, jnp.max(s_loc, axis=-1, keepdims=True))
        ov = jnp.dot(jnp.exp(s_ctx - m).astype(BF16), _with_ones(v_ctx[hk]), preferred_element_type=F32)
        if k_loc is not None:
            ov = ov + jnp.dot(jnp.exp(s_loc - m).astype(BF16), _with_ones(v_loc[hk]), preferred_element_type=F32)
        den = ov[:, HEAD_DIM:HEAD_DIM + 1] + jnp.exp(sink - m)
        o = (ov[:, :HEAD_DIM] * (1.0 / den)).astype(o_ref.dtype)
        for n, h in enumerate(heads):
            o_ref[:, h * HEAD_DIM:(h + 1) * HEAD_DIM] = o[n * nq:(n + 1) * nq]


def _with_ones(v):
    return jnp.concatenate([v, jnp.ones_like(v)], axis=1)


def _split_kv(kv):
    ks = [kv[:, h * HEAD_DIM:(h + 1) * HEAD_DIM] for h in range(N_KV_HEADS)]
    vs = [kv[:, ATTN_KV + h * HEAD_DIM:ATTN_KV + (h + 1) * HEAD_DIM] for h in range(N_KV_HEADS)]
    return ks, vs


def _attn_lat_kernel(q_ref, kvp_ref, kvc_ref, kvn_ref, kvx_ref, sink_ref, o_ref, *, nb):
    n = pl.program_id(1)
    kp, vp = _split_kv(kvp_ref[...])
    kc, vc = _split_kv(kvc_ref[...])
    kn, vn = _split_kv(kvn_ref[...])
    k_ctx, v_ctx = _split_kv(kvx_ref[...])
    k_loc = [jnp.concatenate([kp[h], kc[h], kn[h]], axis=0) for h in range(N_KV_HEADS)]
    v_loc = [jnp.concatenate([vp[h], vc[h], vn[h]], axis=0) for h in range(N_KV_HEADS)]
    i = lax.broadcasted_iota(I32, (ATTN_BLOCK, 3 * ATTN_BLOCK), 0)
    j = lax.broadcasted_iota(I32, (ATTN_BLOCK, 3 * ATTN_BLOCK), 1)
    valid = (j >= i) & (j <= i + 2 * WINDOW)
    valid = valid & ((j >= ATTN_BLOCK) | (n > 0)) & ((j < 2 * ATTN_BLOCK) | (n < nb - 1))
    _attn_heads(q_ref[...], k_loc, v_loc, valid, k_ctx, v_ctx, sink_ref, o_ref)


def _attn_ctx_kernel(q_ref, kvx_ref, sink_ref, o_ref):
    k_ctx, v_ctx = _split_kv(kvx_ref[...])
    _attn_heads(q_ref[...], None, None, None, k_ctx, v_ctx, sink_ref, o_ref)


def _attn_latent(qkv_l, qkv_c, sink_b, B, S, C):
    nb = S // ATTN_BLOCK
    kvw = 2 * ATTN_KV
    kvcol = ATTN_Q // kvw
    return pl.pallas_call(
        functools.partial(_attn_lat_kernel, nb=nb),
        grid=(B, nb),
        in_specs=[
            pl.BlockSpec((ATTN_BLOCK, ATTN_Q), lambda b, n: (b * nb + n, 0)),
            pl.BlockSpec((ATTN_BLOCK, kvw), lambda b, n: (b * nb + jnp.maximum(n - 1, 0), kvcol)),
            pl.BlockSpec((ATTN_BLOCK, kvw), lambda b, n: (b * nb + n, kvcol)),
            pl.BlockSpec((ATTN_BLOCK, kvw), lambda b, n: (b * nb + jnp.minimum(n + 1, nb - 1), kvcol)),
            pl.BlockSpec((C, kvw), lambda b, n: (b, kvcol)),
            pl.BlockSpec((SUBLANES, LANES), lambda b, n: (0, 0)),
        ],
        out_specs=pl.BlockSpec((ATTN_BLOCK, ATTN_Q), lambda b, n: (b * nb + n, 0)),
        out_shape=jax.ShapeDtypeStruct((B * S, ATTN_Q), BF16),
        compiler_params=_cparams("parallel", "parallel"),
        name="attn_latent",
    )(qkv_l, qkv_l, qkv_l, qkv_l, qkv_c, sink_b)


def _attn_context(qkv_c, sink_b, B, C):
    nb = C // ATTN_BLOCK
    kvw = 2 * ATTN_KV
    kvcol = ATTN_Q // kvw
    return pl.pallas_call(
        _attn_ctx_kernel,
        grid=(B, nb),
        in_specs=[
            pl.BlockSpec((ATTN_BLOCK, ATTN_Q), lambda b, n: (b * nb + n, 0)),
            pl.BlockSpec((C, kvw), lambda b, n: (b, kvcol)),
            pl.BlockSpec((SUBLANES, LANES), lambda b, n: (0, 0)),
        ],
        out_specs=pl.BlockSpec((ATTN_BLOCK, ATTN_Q), lambda b, n: (b * nb + n, 0)),
        out_shape=jax.ShapeDtypeStruct((B * C, ATTN_Q), BF16),
        compiler_params=_cparams("parallel", "parallel"),
        name="attn_context",
    )(qkv_c, qkv_c, sink_b)


def _gla_constants():
    Lc = GLA_CHUNK
    e = np.zeros((GLA_LEVELS + 2, Lc, Lc), np.float32)
    msk = np.zeros((GLA_LEVELS + 1, Lc, Lc), np.float32)
    t = np.arange(Lc)
    for l in range(GLA_LEVELS):
        m = 1 << l
        blk = t // (2 * m)
        upper = (t % (2 * m)) >= m
        bnd = blk * 2 * m + m
        r = t[None, :]
        eq = upper[:, None] & (r >= bnd[:, None]) & (r <= t[:, None])
        ek = (~upper)[:, None] & (r > t[:, None]) & (r <= bnd[:, None] - 1)
        e[l] = (eq | ek).astype(np.float32)
        msk[l] = (upper[:, None] & (~upper)[None, :] & (blk[:, None] == blk[None, :])).astype(np.float32)
    e[GLA_LEVELS] = (t[None, :] <= t[:, None]).astype(np.float32)
    e[GLA_LEVELS + 1] = (t[None, :] > t[:, None]).astype(np.float32)
    msk[GLA_LEVELS] = np.eye(Lc, dtype=np.float32)
    e2 = np.stack([e, e[:, ::-1, ::-1]]).reshape(2, (GLA_LEVELS + 2) * Lc, Lc)
    m2 = np.stack([msk, msk[:, ::-1, ::-1]])
    return e2, m2


def _gla_kernel(x_ref, gz_ref, w2_ref, bias_ref, e_ref, m_ref, s0_ref, o_ref, sfin_ref, s_scr, *, nch):
    i = pl.program_id(2)

    @pl.when(i == 0)
    def _():
        s_scr[...] = s0_ref[:, 0]

    for bb in range(x_ref.shape[0]):
        _gla_chunk(x_ref.at[bb], gz_ref.at[bb], w2_ref, bias_ref, e_ref, m_ref, o_ref.at[0, bb], s_scr.at[bb])

    @pl.when(i == nch - 1)
    def _():
        sfin_ref[:, 0] = s_scr[...]


def _gla_chunk(x_ref, gz_ref, w2_ref, bias_ref, e_ref, m_ref, o_ref, s_scr):
    Lc = GLA_CHUNK
    z = jnp.dot(gz_ref[...].astype(BF16), w2_ref[0], preferred_element_type=F32) + bias_ref[0]
    la = (jnp.minimum(z, 0.0) - jnp.log1p(jnp.exp(-jnp.abs(z)))) * (1.0 / GLA_TAU)
    la_hi = la.astype(BF16)
    la_lo = (la - la_hi.astype(F32)).astype(BF16)
    la2 = jnp.concatenate([la_hi, la_lo], axis=1)
    ex = jnp.dot(e_ref[0], la2, preferred_element_type=F32)
    decay = jnp.exp(ex[:, :GLA_QK] + ex[:, GLA_QK:])
    tot = lax.dot_general(la2, jnp.ones((Lc, LANES), BF16), (((0,), (0,)), ((), ())), preferred_element_type=F32)
    a_tot = jnp.exp(tot[:GLA_QK] + tot[GLA_QK:])

    q = x_ref[:, 0:GLA_QK] * (GLA_DK ** -0.5)
    k = x_ref[:, GLA_QK:2 * GLA_QK]
    v = x_ref[:, 2 * GLA_QK:2 * GLA_QK + GLA_V].astype(BF16)
    row_head = lax.broadcasted_iota(I32, (GLA_QK, GLA_DV), 0) >> 6
    first_of_pair = (lax.broadcasted_iota(I32, (Lc, LANES), 1) >> 6) == 0
    nt = (((1,), (1,)), ((), ()))

    att = [jnp.zeros((Lc, Lc), F32) for _ in range(GLA_HEADS)]
    for l in range(GLA_LEVELS + 1):
        if l < GLA_LEVELS:
            dl = decay[l * Lc:(l + 1) * Lc]
            ql = (q * dl).astype(BF16)
            kl = (k * dl).astype(BF16)
        else:
            ql = q.astype(BF16)
            kl = k.astype(BF16)
        ml = m_ref[0, l]
        for pair in range(GLA_HEADS // 2):
            qp = ql[:, pair * LANES:(pair + 1) * LANES]
            kp = kl[:, pair * LANES:(pair + 1) * LANES]
            zero = jnp.zeros_like(kp)
            kk = jnp.concatenate([jnp.where(first_of_pair, kp, zero), jnp.where(first_of_pair, zero, kp)], axis=0)
            sc = lax.dot_general(qp, kk, nt, preferred_element_type=F32)
            att[2 * pair] = att[2 * pair] + ml * sc[:, :Lc]
            att[2 * pair + 1] = att[2 * pair + 1] + ml * sc[:, Lc:]

    s_old = s_scr[...]
    s_b = s_old.astype(BF16)
    zero_s = jnp.zeros_like(s_b)
    s_bd = jnp.concatenate([jnp.where(row_head == h, s_b, zero_s) for h in range(GLA_HEADS)], axis=1)
    q_in = (q * decay[GLA_LEVELS * Lc:(GLA_LEVELS + 1) * Lc]).astype(BF16)
    k_out = (k * decay[(GLA_LEVELS + 1) * Lc:(GLA_LEVELS + 2) * Lc]).astype(BF16)
    o_inter = jnp.dot(q_in, s_bd, preferred_element_type=F32)
    contrib = lax.dot_general(k_out, v, (((0,), (0,)), ((), ())), preferred_element_type=F32)
    s_new = a_tot * s_old
    for h in range(GLA_HEADS):
        cols = slice(h * GLA_DV, (h + 1) * GLA_DV)
        o_ref[:, cols] = jnp.dot(att[h].astype(BF16), v[:, cols], preferred_element_type=F32) + o_inter[:, cols]
        s_new = s_new + jnp.where(row_head == h, contrib[:, cols], 0.0)
    s_scr[...] = s_new


def _gla(gla_arr, gz_arr, w2p, bias, e2, m2, s0, B, T):
    nch = T // GLA_CHUNK
    nseq = 2 if B % 2 == 0 else 1
    chunk = lambda d, i: jnp.where(d == 0, i, nch - 1 - i)
    st = lambda b, d, i: (b, d, 0, 0)
    o, s_fin = pl.pallas_call(
        functools.partial(_gla_kernel, nch=nch),
        grid=(B // nseq, 2, nch),
        in_specs=[
            pl.BlockSpec((nseq, GLA_CHUNK, 2 * GLA_QK + GLA_V), lambda b, d, i: (b, chunk(d, i), 0)),
            pl.BlockSpec((nseq, GLA_CHUNK, GZ_W), lambda b, d, i: (b, chunk(d, i), 0)),
            pl.BlockSpec((1, GZ_W, GLA_QK), lambda b, d, i: (d, 0, 0)),
            pl.BlockSpec((1, 1, GLA_QK), lambda b, d, i: (d, 0, 0)),
            pl.BlockSpec((1, (GLA_LEVELS + 2) * GLA_CHUNK, GLA_CHUNK), lambda b, d, i: (d, 0, 0)),
            pl.BlockSpec((1, GLA_LEVELS + 1, GLA_CHUNK, GLA_CHUNK), lambda b, d, i: (d, 0, 0, 0)),
            pl.BlockSpec((nseq, 1, GLA_QK, GLA_DV), st),
        ],
        out_specs=[
            pl.BlockSpec((1, nseq, GLA_CHUNK, GLA_V), lambda b, d, i: (d, b, chunk(d, i), 0)),
            pl.BlockSpec((nseq, 1, GLA_QK, GLA_DV), st),
        ],
        out_shape=[
            jax.ShapeDtypeStruct((2, B, T, GLA_V), F32),
            jax.ShapeDtypeStruct((B, 2, GLA_QK, GLA_DV), F32),
        ],
        scratch_shapes=[pltpu.VMEM((nseq, GLA_QK, GLA_DV), F32)],
        compiler_params=_cparams("parallel", "parallel", "arbitrary"),
        name="gla_scan",
    )(gla_arr.reshape(B, T, GLA_W), gz_arr.reshape(B, T, GZ_W), w2p, bias, e2, m2, s0)
    return o.reshape(2, B * T, GLA_V), s_fin


def _lru_kernel(x_ref, pv_ref, nx_ref, cw_ref, cb_ref, wg_ref, bg_ref, lam_ref, h0_ref, o_ref, hfin_ref, carry, *, nblk):
    d = pl.program_id(1)
    i = pl.program_id(2)
    T = LRU_CHUNK
    W = LRU_WIDTH

    @pl.when(i == 0)
    def _():
        carry[...] = jnp.broadcast_to(h0_ref[0, 0], (SUBLANES, W))

    li = jnp.where(d == 0, i, nblk - 1 - i)
    pv = jnp.where(li > 0, pv_ref[...], 0.0)
    nx = jnp.where(li < nblk - 1, nx_ref[...], 0.0)
    xe = jnp.concatenate([pv, x_ref[...], nx], axis=0)
    n_ext = T + 2 * SUBLANES
    win = lambda off: pltpu.roll(xe, n_ext - off, 0)[0:T]
    cw = cw_ref[...]
    xc = cb_ref[...] + win(6) * cw[0:1] + win(7) * cw[1:2] + xe[SUBLANES:SUBLANES + T] * cw[2:3] + win(9) * cw[3:4]

    g = jnp.dot(xc.astype(BF16), wg_ref[0], preferred_element_type=F32) + bg_ref[0]
    r = jax.nn.sigmoid(g[:, :W])
    gi = jax.nn.sigmoid(g[:, W:])
    log_a = (-LRU_C * _softplus(-lam_ref[0])) * r
    a = jnp.exp(log_a)
    u = jnp.sqrt(-jnp.tanh(log_a) * (a * a + 1.0)) * (gi * xc)
    groups = T // SUBLANES
    a3 = a.reshape(groups, SUBLANES, W)
    u3 = u.reshape(groups, SUBLANES, W)
    sub_row = lax.broadcasted_iota(I32, (groups, SUBLANES, W), 1)

    @pl.when(d == 0)
    def _():
        aa, uu = a3, u3
        sh = 1
        while sh < SUBLANES:
            ok = sub_row >= sh
            uu = uu + aa * jnp.where(ok, pltpu.roll(uu, sh, 1), 0.0)
            aa = aa * jnp.where(ok, pltpu.roll(aa, sh, 1), 1.0)
            sh *= 2
        c = carry[0:1]
        for g in range(groups):
            hg = uu[g] + aa[g] * c
            o_ref[0, g * SUBLANES:(g + 1) * SUBLANES, :] = hg
            c = hg[SUBLANES - 1:SUBLANES]
        carry[...] = jnp.broadcast_to(c, (SUBLANES, W))

    @pl.when(d == 1)
    def _():
        aa, uu = a3, u3
        sh = 1
        while sh < SUBLANES:
            ok = sub_row < SUBLANES - sh
            uu = uu + aa * jnp.where(ok, pltpu.roll(uu, SUBLANES - sh, 1), 0.0)
            aa = aa * jnp.where(ok, pltpu.roll(aa, SUBLANES - sh, 1), 1.0)
            sh *= 2
        c = carry[0:1]
        for g in reversed(range(groups)):
            hg = uu[g] + aa[g] * c
            o_ref[0, g * SUBLANES:(g + 1) * SUBLANES, :] = hg
            c = hg[0:1]
        carry[...] = jnp.broadcast_to(c, (SUBLANES, W))

    @pl.when(i == nblk - 1)
    def _():
        hfin_ref[0, 0] = carry[0:1]


def _lru(lru_arr, cw, cb, wg, bg, lam, h0, B, T):
    nblk = T // LRU_CHUNK
    per8 = LRU_CHUNK // SUBLANES
    n8 = B * T // SUBLANES
    blk = lambda b, d, i: b * nblk + jnp.where(d == 0, i, nblk - 1 - i)
    dirw = lambda b, d, i: (d, 0, 0)
    st = lambda b, d, i: (b, d, 0, 0)
    return pl.pallas_call(
        functools.partial(_lru_kernel, nblk=nblk),
        grid=(B, 2, nblk),
        in_specs=[
            pl.BlockSpec((LRU_CHUNK, LRU_WIDTH), lambda b, d, i: (blk(b, d, i), 0)),
            pl.BlockSpec((SUBLANES, LRU_WIDTH), lambda b, d, i: (jnp.maximum(blk(b, d, i) * per8 - 1, 0), 0)),
            pl.BlockSpec((SUBLANES, LRU_WIDTH), lambda b, d, i: (jnp.minimum((blk(b, d, i) + 1) * per8, n8 - 1), 0)),
            pl.BlockSpec((SUBLANES, LRU_WIDTH), lambda b, d, i: (0, 0)),
            pl.BlockSpec((1, LRU_WIDTH), lambda b, d, i: (0, 0)),
            pl.BlockSpec((1, LRU_WIDTH, 2 * LRU_WIDTH), dirw),
            pl.BlockSpec((1, 1, 2 * LRU_WIDTH), dirw),
            pl.BlockSpec((1, 1, LRU_WIDTH), dirw),
            pl.BlockSpec((1, 1, 1, LRU_WIDTH), st),
        ],
        out_specs=[
            pl.BlockSpec((1, LRU_CHUNK, LRU_WIDTH), lambda b, d, i: (d, blk(b, d, i), 0)),
            pl.BlockSpec((1, 1, 1, LRU_WIDTH), st),
        ],
        out_shape=[
            jax.ShapeDtypeStruct((2, B * T, LRU_WIDTH), F32),
            jax.ShapeDtypeStruct((B, 2, 1, LRU_WIDTH), F32),
        ],
        scratch_shapes=[pltpu.VMEM((SUBLANES, LRU_WIDTH), F32)],
        compiler_params=_cparams("parallel", "parallel", "arbitrary"),
        name="lru_scan",
    )(lru_arr, lru_arr, lru_arr, cw, cb, wg, bg, lam, h0)


def _out_kernel(h_ref, attn_ref, go_ref, gr_ref, gg_ref, lh_ref, lg_ref, w_ref, g1_ref, a2_ref, s2_ref, wr_ref, br_ref,
                vt_ref, lt_ref, hn_ref, v_ref, lgt_ref, *, n_main):
    i = pl.program_id(0)

    @pl.when(i < n_main)
    def _():
        _out_body(h_ref, attn_ref, go_ref, gr_ref, gg_ref, lh_ref, lg_ref, w_ref, g1_ref, a2_ref, s2_ref, wr_ref, br_ref,
                  hn_ref, v_ref, lgt_ref)

    @pl.when(i >= n_main)
    def _():
        v_ref[...] = vt_ref[...]
        lgt_ref[...] = lt_ref[...]


def _out_body(h_ref, attn_ref, go_ref, gr_ref, gg_ref, lh_ref, lg_ref, w_ref, g1_ref, a2_ref, s2_ref, wr_ref, br_ref,
              hn_ref, v_ref, lgt_ref):
    o = go_ref[0] + go_ref[1]
    gr = gr_ref[...]
    gate = gr * jax.nn.sigmoid(gr)
    parts = []
    for hh in range(GLA_HEADS):
        oh = o[:, hh * GLA_DV:(hh + 1) * GLA_DV]
        y = (oh * lax.rsqrt(jnp.mean(oh * oh, axis=-1, keepdims=True) + RMS_EPS)) * gg_ref[...]
        parts.append(y * gate[:, hh * GLA_DV:(hh + 1) * GLA_DV])
    gla = jnp.concatenate(parts, axis=1).astype(BF16)
    lg = lg_ref[...]
    gelu = lg * (0.5 * (1.0 + jnp.tanh(np.sqrt(2.0 / np.pi).astype(np.float32) * (lg + 0.044715 * (lg * lg * lg)))))
    lru = ((lh_ref[0] + lh_ref[1]) * gelu).astype(BF16)
    y = jnp.dot(attn_ref[...], w_ref[0:ATTN_Q], preferred_element_type=F32)
    y = y + jnp.dot(gla, w_ref[ATTN_Q:ATTN_Q + GLA_V], preferred_element_type=F32)
    y = y + jnp.dot(lru, w_ref[ATTN_Q + GLA_V:D_MIX], preferred_element_type=F32)
    hn = h_ref[...] + g1_ref[0] * y
    hn_ref[...] = hn
    v = (hn * lax.rsqrt(jnp.mean(hn * hn, axis=-1, keepdims=True) + RMS_EPS)) * a2_ref[0] + s2_ref[0]
    _store_token_tiles(v_ref, _pack_bf16_pairs(v))
    v_hi = v.astype(BF16)
    v_lo = (v - v_hi.astype(F32)).astype(BF16)
    t = jnp.dot(v_hi, wr_ref[...], preferred_element_type=F32)
    t_lo = jnp.dot(v_lo, wr_ref[:, 0:LANES], preferred_element_type=F32)
    lgt_ref[...] = (t[:, :LANES] + t[:, LANES:]) + t_lo + br_ref[...]


def _mixer_out(h, attn, go, gla_arr, gg, lh, lru_arr, w_out, g1, a2, s2, wr, br, seq_len, tail):
    rows, D = h.shape
    tm = _tile(seq_len, 256)
    per = seq_len // tm
    n_main = rows // tm
    tw = tm * (D // 2 // LANES)
    if tail is None:
        tail = (jnp.zeros((tw, LANES), I32), jnp.zeros((tm, LANES), F32))
        n_tail = 0
    else:
        assert tail[1].shape[0] % tm == 0
        n_tail = tail[1].shape[0] // tm
    n_tok = rows + n_tail * tm
    main = lambda i: jnp.minimum(i, n_main - 1)
    row = lambda i: (main(i), 0)
    row3 = lambda i: (0, main(i), 0)
    bat = lambda i: (main(i) // per, 0, 0)
    const = lambda i: (0, 0)
    tok = lambda i: (i, 0)
    trow = lambda i: (jnp.maximum(i - n_main, 0), 0)
    in_specs = [
        pl.BlockSpec((tm, D), row),
        pl.BlockSpec((tm, ATTN_Q), row),
        pl.BlockSpec((2, tm, GLA_V), row3),
        pl.BlockSpec((tm, GLA_V), lambda i: (main(i), 2)),
        pl.BlockSpec((1, GLA_DV), const),
        pl.BlockSpec((2, tm, LRU_WIDTH), row3),
        pl.BlockSpec((tm, LRU_WIDTH), lambda i: (main(i), 1)),
        pl.BlockSpec((D_MIX, D), const, pipeline_mode=pl.Buffered(1)),
        pl.BlockSpec((1, 1, D), bat),
        pl.BlockSpec((1, 1, D), bat),
        pl.BlockSpec((1, 1, D), bat),
        pl.BlockSpec((D, 2 * LANES), const),
        pl.BlockSpec((1, LANES), const),
        pl.BlockSpec((tw, LANES), trow),
        pl.BlockSpec((tm, LANES), trow),
    ]
    args = [h, attn, go, gla_arr, gg, lh, lru_arr, w_out, g1, a2, s2, wr, br, tail[0], tail[1]]
    return pl.pallas_call(
        functools.partial(_out_kernel, n_main=n_main),
        grid=(n_main + n_tail,),
        in_specs=in_specs,
        out_specs=[
            pl.BlockSpec((tm, D), row),
            pl.BlockSpec((tw, LANES), tok),
            pl.BlockSpec((tm, LANES), tok),
        ],
        out_shape=[
            jax.ShapeDtypeStruct((rows, D), F32),
            jax.ShapeDtypeStruct((n_tok * (tw // tm), LANES), I32),
            jax.ShapeDtypeStruct((n_tok, LANES), F32),
        ],
        compiler_params=_cparams("arbitrary"),
        name="mixer_out",
    )(*args)


def _route_kernel(lg_ref, tri_ref, oi_ref, of_ref, cnt_ref, carry):
    i = pl.program_id(0)

    @pl.when(i == 0)
    def _():
        carry[...] = jnp.zeros_like(carry)

    lg = lg_ref[...]
    col = lax.broadcasted_iota(I32, lg.shape, 1)
    colf = col.astype(F32)
    big = float(LANES)
    is_g = col < N_GROUPS
    gm = jnp.max(jnp.where(is_g, lg, -jnp.inf), axis=-1, keepdims=True)
    eg = jnp.where(is_g, jnp.exp(lg - gm), 0.0)
    pg = eg / jnp.sum(eg, axis=-1, keepdims=True)
    p_grp = jnp.max(pg, axis=-1, keepdims=True)
    grp = jnp.min(jnp.where(is_g & (pg == p_grp), colf, big), axis=-1, keepdims=True).astype(I32)

    sel = (col >= N_GROUPS) & (col < N_GROUPS + N_EXPERTS) & (((col - N_GROUPS) >> 3) == grp)
    em = jnp.max(jnp.where(sel, lg, -jnp.inf), axis=-1, keepdims=True)
    ee = jnp.where(sel, jnp.exp(lg - em), 0.0)
    pe = ee / jnp.sum(ee, axis=-1, keepdims=True)
    p1 = jnp.max(jnp.where(sel, pe, -1.0), axis=-1, keepdims=True)
    c1 = jnp.min(jnp.where(sel & (pe == p1), colf, big), axis=-1, keepdims=True).astype(I32)
    rest = sel & (col != c1)
    p2 = jnp.max(jnp.where(rest, pe, -1.0), axis=-1, keepdims=True)
    c2 = jnp.min(jnp.where(rest & (pe == p2), colf, big), axis=-1, keepdims=True).astype(I32)
    e1 = c1 - N_GROUPS
    e2 = c2 - N_GROUPS
    den = p1 + p2
    g1 = p_grp * (p1 / den)
    g2 = p_grp * (p2 / den)

    hit1 = col == e1
    hit2 = col == e2
    oh = jnp.where(hit1 | hit2, 1.0, 0.0)
    before = jnp.dot(tri_ref[...], oh.astype(BF16), preferred_element_type=F32) + carry[0:1]
    r1 = jnp.sum(jnp.where(hit1, before, 0.0), axis=-1, keepdims=True).astype(I32)
    r2 = jnp.sum(jnp.where(hit2, before, 0.0), axis=-1, keepdims=True).astype(I32)
    new = carry[0:1] + jnp.sum(oh, axis=0, keepdims=True)
    carry[...] = jnp.broadcast_to(new, carry.shape)
    cnt_ref[...] = jnp.broadcast_to(new, cnt_ref.shape)

    zero = jnp.zeros_like(col)
    oi_ref[...] = jnp.where(col == 0, e1, jnp.where(col == 1, e2, jnp.where(col == 2, r1, jnp.where(col == 3, r2, zero))))
    of_ref[...] = jnp.where(col == 0, g1, jnp.where(col == 1, g2, 0.0))


def _route(logits):
    n_tok = logits.shape[0]
    tm = _tile(n_tok, 1024)
    row = lambda i: (i, 0)
    tri = jnp.tril(jnp.ones((tm, tm), BF16), -1)
    return pl.pallas_call(
        _route_kernel,
        grid=(n_tok // tm,),
        in_specs=[pl.BlockSpec((tm, LANES), row), pl.BlockSpec((tm, tm), lambda i: (0, 0))],
        out_specs=[
            pl.BlockSpec((tm, LANES), row),
            pl.BlockSpec((tm, LANES), row),
            pl.BlockSpec((SUBLANES, LANES), lambda i: (0, 0)),
        ],
        out_shape=[
            jax.ShapeDtypeStruct((n_tok, LANES), I32),
            jax.ShapeDtypeStruct((n_tok, LANES), F32),
            jax.ShapeDtypeStruct((SUBLANES, LANES), F32),
        ],
        scratch_shapes=[pltpu.VMEM((SUBLANES, LANES), F32)],
        compiler_params=_cparams("arbitrary"),
        name="route",
    )(logits, tri)


def _cast_rows(src, dst):
    step = _tile(dst.shape[0], 256)

    def body(r, c):
        rows = pl.ds(pl.multiple_of(r * step, step), step)
        dst[rows, :] = src[rows, :].astype(BF16)
        return c

    lax.fori_loop(0, dst.shape[0] // step, body, 0)


def _moe_ffn_kernel(be_ref, nu_ref, nx_ref, src_ref, srcn_ref, v_hbm, w1_hbm, w3_hbm, w2_hbm, o_ref,
                    xbuf, st1, st3, st2, w1s, w3s, w2s, sem_x, sem_w, *, layer, sub):
    i = pl.program_id(0)
    nu = nu_ref[0]
    rows = ROW_BLOCK * sub

    def gather(idx_ref, slot):
        for r in range(ROW_BLOCK):
            s = pl.multiple_of(idx_ref[0, 0, r] * sub, sub)
            pltpu.make_async_copy(v_hbm.at[pl.ds(s, sub)], xbuf.at[slot, pl.ds(r * sub, sub)], sem_x.at[slot]).start()

    def wait_rows(slot):
        pltpu.make_async_copy(v_hbm.at[pl.ds(0, rows)], xbuf.at[slot], sem_x.at[slot]).wait()

    def weight_copies(e):
        return (pltpu.make_async_copy(w1_hbm.at[layer, e], st1, sem_w.at[0]),
                pltpu.make_async_copy(w3_hbm.at[layer, e], st3, sem_w.at[1]),
                pltpu.make_async_copy(w2_hbm.at[layer, e], st2, sem_w.at[2]))

    @pl.when(i < nu)
    def _():
        e = be_ref[i]
        slot = i & 1

        @pl.when(i == 0)
        def _():
            for cp in weight_copies(e):
                cp.start(priority=1)
            gather(src_ref, 0)

        @pl.when((i == 0) | (e != be_ref[jnp.maximum(i - 1, 0)]))
        def _():
            for cp in weight_copies(e):
                cp.wait()
            _cast_rows(st1, w1s)
            _cast_rows(st3, w3s)
            _cast_rows(st2, w2s)

            @pl.when(nx_ref[i] >= 0)
            def _():
                for cp in weight_copies(nx_ref[i]):
                    cp.start(priority=1)

        wait_rows(slot)
        gather(srcn_ref, 1 - slot)
        xb = _unpack_bf16_pairs(_load_token_tiles(xbuf.at[slot], ROW_BLOCK)).astype(BF16)
        a = jnp.dot(xb, w1s[...], preferred_element_type=F32)
        b = jnp.dot(xb, w3s[...], preferred_element_type=F32)
        hmid = ((a * jax.nn.sigmoid(a)) * b).astype(BF16)
        _store_token_tiles(o_ref, _pack_bf16_pairs(jnp.dot(hmid, w2s[...], preferred_element_type=F32)))

        @pl.when(i == nu - 1)
        def _():
            wait_rows(1 - slot)

    @pl.when(i >= nu)
    def _():
        o_ref[...] = jnp.zeros_like(o_ref)


def _moe_experts(v_tiles, src, blk_e, n_used, nxt_e, w1, w3, w2, layer):
    D, FF = w1.shape[-2:]
    sub = D // 2 // LANES
    P = src.shape[0]
    nb = P // ROW_BLOCK
    tw = ROW_BLOCK * sub
    cur = lambda i, be, nu, nx: (jnp.minimum(i, nu[0] - 1), 0, 0)
    nxt = lambda i, be, nu, nx: (jnp.minimum(i + 1, nu[0] - 1), 0, 0)
    src3 = src.reshape(nb, 1, ROW_BLOCK)
    hbm = pl.BlockSpec(memory_space=pl.ANY)
    return pl.pallas_call(
        functools.partial(_moe_ffn_kernel, layer=layer, sub=sub),
        grid_spec=pltpu.PrefetchScalarGridSpec(
            num_scalar_prefetch=3,
            grid=(nb,),
            in_specs=[
                pl.BlockSpec((1, 1, ROW_BLOCK), cur, memory_space=pltpu.SMEM),
                pl.BlockSpec((1, 1, ROW_BLOCK), nxt, memory_space=pltpu.SMEM),
                hbm, hbm, hbm, hbm,
            ],
            out_specs=pl.BlockSpec((tw, LANES), lambda i, be, nu, nx: (i, 0)),
            scratch_shapes=[
                pltpu.VMEM((2, tw, LANES), I32),
                pltpu.VMEM((D, FF), F32), pltpu.VMEM((D, FF), F32), pltpu.VMEM((FF, D), F32),
                pltpu.VMEM((D, FF), BF16), pltpu.VMEM((D, FF), BF16), pltpu.VMEM((FF, D), BF16),
                pltpu.SemaphoreType.DMA((2,)), pltpu.SemaphoreType.DMA((3,)),
            ],
        ),
        out_shape=jax.ShapeDtypeStruct((P * sub, LANES), I32),
        compiler_params=_cparams("arbitrary"),
        name="moe_ffn",
    )(blk_e, n_used, nxt_e, src3, src3, v_tiles, w1, w3, w2)


def _combine_kernel(dest_ref, destn_ref, h_ref, gate_ref, g2_ref, fg_ref, ys_ref, o_ref, buf, sem, *, final, nsteps):
    i = pl.program_id(0)
    slot = i & 1
    tm = h_ref.shape[0]
    rows = buf.shape[2]
    sub = rows // tm

    def gather(idx_ref, sl):
        unroll = 8

        def start(it, c):
            for u in range(unroll):
                j = it * unroll + u
                src = pl.multiple_of(idx_ref[0, 0, j] * sub, sub)
                dst = pl.multiple_of((j >> 1) * sub, sub)
                cp = pltpu.make_async_copy(ys_ref.at[pl.ds(src, sub)], buf.at[sl, u % TOP_K, pl.ds(dst, sub)], sem.at[sl])
                cp.start(priority=u % 2)
            return c

        lax.fori_loop(0, tm * TOP_K // unroll, start, 0)

    def wait_rows(sl):
        for kk in range(TOP_K):
            pltpu.make_async_copy(ys_ref.at[pl.ds(0, rows)], buf.at[sl, kk], sem.at[sl]).wait()

    @pl.when(i == 0)
    def _():
        gather(dest_ref, 0)

    gather(destn_ref, 1 - slot)
    wait_rows(slot)
    gate = gate_ref[...]
    f = (_unpack_bf16_pairs(_load_token_tiles(buf.at[slot, 0], tm)) * gate[:, 0:1]
         + _unpack_bf16_pairs(_load_token_tiles(buf.at[slot, 1], tm)) * gate[:, 1:2])

    @pl.when(i == nsteps - 1)
    def _():
        wait_rows(1 - slot)

    hn = h_ref[...] + g2_ref[0] * f
    if final:
        hn = (hn * lax.rsqrt(jnp.mean(hn * hn, axis=-1, keepdims=True) + RMS_EPS)) * fg_ref[...]
    o_ref[...] = hn


def _combine(h, dest, gates, g2, fg, ys, seq_len, tok_off, final):
    rows, D = h.shape
    tm = _tile(seq_len, 256)
    per = seq_len // tm
    off = tok_off // tm
    n_blk_all = dest.shape[0] // (tm * TOP_K)
    nsteps = rows // tm
    dest3 = dest.reshape(n_blk_all, 1, tm * TOP_K)
    return pl.pallas_call(
        functools.partial(_combine_kernel, final=final, nsteps=nsteps),
        grid=(nsteps,),
        in_specs=[
            pl.BlockSpec((1, 1, tm * TOP_K), lambda i: (i + off, 0, 0), memory_space=pltpu.SMEM),
            pl.BlockSpec((1, 1, tm * TOP_K), lambda i: (jnp.minimum(i + 1, nsteps - 1) + off, 0, 0), memory_space=pltpu.SMEM),
            pl.BlockSpec((tm, D), lambda i: (i, 0)),
            pl.BlockSpec((tm, LANES), lambda i: (i + off, 0)),
            pl.BlockSpec((1, 1, D), lambda i: (i // per, 0, 0)),
            pl.BlockSpec((1, D), lambda i: (0, 0)),
            pl.BlockSpec(memory_space=pl.ANY),
        ],
        out_specs=pl.BlockSpec((tm, D), lambda i: (i, 0)),
        out_shape=jax.ShapeDtypeStruct((rows, D), F32),
        scratch_shapes=[pltpu.VMEM((2, TOP_K, tm * (D // 2 // LANES), LANES), I32), pltpu.SemaphoreType.DMA((2,))],
        compiler_params=_cparams("arbitrary"),
        name="combine",
    )(dest3, dest3, h, gates, g2, fg, ys)


def _rope_tables(S):
    rows = S // GRID_W
    row = jnp.repeat(jnp.arange(rows, dtype=F32), GRID_W)
    col = jnp.tile(jnp.arange(GRID_W, dtype=F32), rows)
    n_freq = HEAD_DIM // 4
    inv = ROPE_THETA ** (-jnp.arange(n_freq, dtype=F32) / n_freq)
    ang = jnp.concatenate([row[:, None] * inv, col[:, None] * inv], axis=-1)
    cos = jnp.repeat(jnp.cos(ang), 2, axis=1)
    sin = jnp.repeat(jnp.sin(ang), 2, axis=1)
    even = (jnp.arange(HEAD_DIM) % 2) == 0
    return cos, jnp.where(even, -sin, 0.0), jnp.where(even, 0.0, sin)


def _block_diag(w):
    n, c, _ = w.shape
    eye = jnp.eye(n, dtype=w.dtype)
    return (eye[:, None, :, None] * w[:, :, None, :]).reshape(n * c, n * c)


def _slot_plan(ids, counts_row, n_tok):
    e = ids[:, 0:TOP_K]
    rank = ids[:, TOP_K:2 * TOP_K]
    counts = counts_row[:N_EXPERTS].astype(I32)
    padded = (counts + ROW_BLOCK - 1) // ROW_BLOCK * ROW_BLOCK
    pad_end = jnp.cumsum(padded)
    pad_start = pad_end - padded
    dest = (pad_start[e] + rank).reshape(n_tok * TOP_K)
    n_blocks = -(-(n_tok * TOP_K + N_EXPERTS * (ROW_BLOCK - 1)) // ROW_BLOCK)
    starts = jnp.arange(n_blocks, dtype=I32) * ROW_BLOCK
    blk_e = jnp.minimum(jnp.sum(pad_end[None, :] <= starts[:, None], axis=1), N_EXPERTS - 1).astype(I32)
    n_used = (pad_end[-1] // ROW_BLOCK).astype(I32).reshape(1)
    run_next = pad_end[blk_e] // ROW_BLOCK
    nxt_e = jnp.where(run_next < n_used[0], blk_e[jnp.minimum(run_next, n_blocks - 1)], -1).astype(I32)
    src = jnp.zeros((n_blocks * ROW_BLOCK,), I32).at[dest].set(jnp.arange(n_tok * TOP_K, dtype=I32) // TOP_K,
                                                                 unique_indices=True)
    return dest, src, blk_e, n_used, nxt_e


def kernel(x, c, ctx, c_ctx, ada_w, ada_b, norm_mix_g, norm_ffn_g, w_in, attn_sink, gla_gate_w2, gla_gate_b, gla_norm_g, lru_conv_w, lru_conv_b, lru_wa, lru_ba, lru_wx, lru_bx, lru_lambda, w_out, router_g_w, router_g_b, router_e_w, router_e_b, moe_w1, moe_w3, moe_w2, final_norm_g):
    B, S, D = x.shape
    C = ctx.shape[1]
    L = ada_w.shape[0]
    assert S % max(ATTN_BLOCK, GLA_CHUNK, LRU_CHUNK, GRID_W) == 0 and C % max(ATTN_BLOCK, GLA_CHUNK, LRU_CHUNK) == 0
    assert B + 1 <= SUBLANES and D % (2 * LANES) == 0

    cond = jnp.concatenate([c, c_ctx[None], jnp.zeros((SUBLANES - B - 1, D), F32)], axis=0)
    mods = _ada_mods(cond, ada_w, ada_b).reshape(L, SUBLANES, 6, D)

    cos_l, se_l, so_l = _rope_tables(S)
    cos_c = jnp.ones((C, HEAD_DIM), F32)
    zero_c = jnp.zeros((C, HEAD_DIM), F32)
    e2_np, m2_np = _gla_constants()
    gla_e = jnp.asarray(e2_np, BF16)
    gla_m = jnp.asarray(m2_np, F32)

    h_lat = x.reshape(B * S, D)
    h_ctx = ctx.reshape(B * C, D)
    out = None
    for l in range(L):
        last = l == L - 1
        ml = mods[l, :B]
        mc = jnp.broadcast_to(mods[l, B][None], (B, 6, D))
        per_b = lambda m, j: m[:, j][:, None, :]
        a1_l, a1_c = [(1.0 + per_b(m, 1)) * norm_mix_g[l] for m in (ml, mc)]
        a2_l, a2_c = [(1.0 + per_b(m, 4)) * norm_ffn_g[l] for m in (ml, mc)]

        w = w_in[l]
        c_gz = ATTN_Q + 2 * ATTN_KV + 2 * GLA_QK + 2 * GLA_V
        w_packed = jnp.concatenate(
            [w[:, :c_gz], w[:, c_gz:c_gz + 2 * GLA_RANK], jnp.zeros((D, GZ_W - 2 * GLA_RANK), F32), w[:, c_gz + 2 * GLA_RANK:]],
            axis=1).astype(BF16)
        qkv_l, gla_l, gz_l, lru_l = _in_proj(h_lat, a1_l, per_b(ml, 0), cos_l, se_l, so_l, w_packed, S)
        qkv_c, gla_c, gz_c, lru_c = _in_proj(h_ctx, a1_c, per_b(mc, 0), cos_c, zero_c, zero_c, w_packed, C)

        sink_b = jnp.broadcast_to(attn_sink[l][:, None], (N_Q_HEADS, LANES)).astype(F32)
        attn_l = _attn_latent(qkv_l, qkv_c, sink_b, B, S, C)

        w2p = jnp.zeros((2, GZ_W, GLA_QK), F32)
        w2p = w2p.at[0, :GLA_RANK].set(gla_gate_w2[l, 0]).at[1, GLA_RANK:2 * GLA_RANK].set(gla_gate_w2[l, 1]).astype(BF16)
        gbias = gla_gate_b[l].reshape(2, 1, GLA_QK)
        s_zero = jnp.zeros((B, 2, GLA_QK, GLA_DV), F32)
        go_c, s_ctx = _gla(gla_c, gz_c, w2p, gbias, gla_e, gla_m, s_zero, B, C)
        go_l, _ = _gla(gla_l, gz_l, w2p, gbias, gla_e, gla_m, s_ctx, B, S)

        cw = jnp.concatenate([lru_conv_w[l], jnp.zeros((SUBLANES - CONV_W, LRU_WIDTH), F32)], axis=0)
        cb = lru_conv_b[l].reshape(1, LRU_WIDTH)
        wg = jnp.stack([jnp.concatenate([_block_diag(lru_wa[l, d]), _block_diag(lru_wx[l, d])], axis=1) for d in range(2)]).astype(BF16)
        bg = jnp.concatenate([lru_ba[l], lru_bx[l]], axis=1).reshape(2, 1, 2 * LRU_WIDTH)
        lam = lru_lambda[l].reshape(2, 1, LRU_WIDTH)
        h_zero = jnp.zeros((B, 2, 1, LRU_WIDTH), F32)
        lh_c, hs_ctx = _lru(lru_c, cw, cb, wg, bg, lam, h_zero, B, C)
        lh_l, _ = _lru(lru_l, cw, cb, wg, bg, lam, hs_ctx, B, S)

        wo = w_out[l].astype(BF16)
        gg = gla_norm_g[l].reshape(1, GLA_DV)
        wr = jnp.concatenate([router_g_w[l], router_e_w[l], jnp.zeros((D, LANES - N_GROUPS - N_EXPERTS), F32)], axis=1)
        wr_hi = wr.astype(BF16)
        wr = jnp.concatenate([wr_hi, (wr - wr_hi.astype(F32)).astype(BF16)], axis=1)
        br =jnp.concatenate([router_g_b[l], router_e_b[l], jnp.zeros((LANES - N_GROUPS - N_EXPERTS,), F32)]).reshape(1, LANES)
        n_tok = B * S if last else B * (S + C)
        tail = None
        if not last:
            attn_c = _attn_context(qkv_c, sink_b, B, C)
            h_ctx, v_c, lgt_c = _mixer_out(h_ctx, attn_c, go_c, gla_c, gg, lh_c, lru_c, wo, per_b(mc, 2), a2_c, per_b(mc, 3),
                                           wr, br, C, None)
            tail = (v_c, lgt_c)
        h_lat, v_all, lgt_all = _mixer_out(h_lat, attn_l, go_l, gla_l, gg, lh_l, lru_l, wo, per_b(ml, 2), a2_l, per_b(ml, 3),
                                           wr, br, S, tail)

        ids, gates, counts = _route(lgt_all)
        dest, src, blk_e, n_used, nxt_e = _slot_plan(ids, counts[0], n_tok)
        ys = _moe_experts(v_all, src, blk_e, n_used, nxt_e, moe_w1, moe_w3, moe_w2, l)
        fg = final_norm_g.reshape(1, D)
        h_lat = _combine(h_lat, dest, gates, per_b(ml, 5), fg, ys, S, 0, last)
        if not last:
            h_ctx = _combine(h_ctx, dest, gates, per_b(mc, 5), fg, ys, C, B * S, False)
        out = h_lat
    return out.reshape(B, S, D)
```

```python
import functools

import numpy as np
import jax
import jax.numpy as jnp
from jax import lax
from jax.experimental import pallas as pl
from jax.experimental.pallas import tpu as pltpu

F32 = jnp.float32
BF16 = jnp.bfloat16
I32 = jnp.int32

GRID_W = 64
RMS_EPS = 1e-6
N_Q_HEADS = 8
N_KV_HEADS = 2
HEAD_DIM = 128
WINDOW = 128
ATTN_BLOCK = 128
ROPE_THETA = 10000.0
GLA_HEADS = 4
GLA_DK = 64
GLA_DV = 128
GLA_RANK = 16
GLA_TAU = 16.0
LRU_WIDTH = 512
LRU_BLOCKS = 8
LRU_C = 8.0
CONV_W = 4
N_GROUPS = 4
EXPERTS_PER_GROUP = 8
N_EXPERTS = 32
TOP_K = 2
ATTN_Q = N_Q_HEADS * HEAD_DIM
ATTN_KV = N_KV_HEADS * HEAD_DIM
GLA_QK = GLA_HEADS * GLA_DK
GLA_V = GLA_HEADS * GLA_DV
D_MIX = ATTN_Q + GLA_V + LRU_WIDTH

LANES = 128
SUBLANES = 8
VMEM_LIMIT_BYTES = 56 * 1024 * 1024

QKV_W = ATTN_Q + 2 * ATTN_KV
GLA_W = 2 * GLA_QK + 2 * GLA_V
GZ_W = LANES
LRU_W = 2 * LRU_WIDTH
COL_GLA = QKV_W
COL_GZ = COL_GLA + GLA_W
COL_LRU = COL_GZ + GZ_W
W_IN_PACKED = COL_LRU + LRU_W

GLA_CHUNK = 128
GLA_LEVELS = 7
LRU_CHUNK = 128
ROW_BLOCK = 256


def _cparams(*sem):
    return pltpu.CompilerParams(dimension_semantics=sem, vmem_limit_bytes=VMEM_LIMIT_BYTES)


def _tile(n, pref):
    t = min(n, pref)
    while n % t:
        t -= SUBLANES
    return t


def _sigmoid(x):
    return 0.5 * (jnp.tanh(0.5 * x) + 1.0)


def _softplus(x):
    return jnp.maximum(x, 0.0) + jnp.log1p(jnp.exp(-jnp.abs(x)))


def _pack_bf16_pairs(x):
    n = x.shape[1] // 2
    lo = lax.bitcast_convert_type(x[:, :n].astype(BF16).astype(F32), I32)
    hi = lax.bitcast_convert_type(x[:, n:].astype(BF16).astype(F32), I32)
    return hi | lax.shift_right_logical(lo, jnp.full(lo.shape, 16, I32))


def _unpack_bf16_pairs(p):
    lo = lax.bitcast_convert_type(lax.shift_left(p, jnp.full(p.shape, 16, I32)), F32)
    hi = lax.bitcast_convert_type(p & jnp.int32(-65536), F32)
    return jnp.concatenate([lo, hi], axis=1)


def _store_token_tiles(ref, packed):
    m = packed.shape[0]
    sub = packed.shape[1] // LANES
    for s in range(sub):
        ref[pl.ds(s, m, stride=sub), :] = packed[:, s * LANES:(s + 1) * LANES]


def _load_token_tiles(ref, m):
    sub = ref.shape[0] // m
    return jnp.concatenate([ref[pl.ds(s, m, stride=sub), :] for s in range(sub)], axis=1)


def _ada_kernel(c_ref, w_ref, b_ref, o_ref):
    c = c_ref[...]
    s = c * _sigmoid(c)
    o_ref[0] = jnp.dot(s.astype(BF16), w_ref[0].astype(BF16), preferred_element_type=F32) + b_ref[0]


def _ada_mods(cond, ada_w, ada_b):
    L, D, D6 = ada_w.shape
    tn = _tile(D6, 1536)
    while tn % LANES:
        tn -= SUBLANES
    return pl.pallas_call(
        _ada_kernel,
        grid=(L, D6 // tn),
        in_specs=[
            pl.BlockSpec((SUBLANES, D), lambda l, j: (0, 0)),
            pl.BlockSpec((1, D, tn), lambda l, j: (l, 0, j)),
            pl.BlockSpec((1, 1, tn), lambda l, j: (l, 0, j)),
        ],
        out_specs=pl.BlockSpec((1, SUBLANES, tn), lambda l, j: (l, 0, j)),
        out_shape=jax.ShapeDtypeStruct((L, SUBLANES, D6), F32),
        compiler_params=_cparams("parallel", "parallel"),
        name="ada_mods",
    )(cond, ada_w, ada_b.reshape(L, 1, D6))


def _in_kernel(x_ref, a_ref, s_ref, cos_ref, se_ref, so_ref, w_ref, qkv_ref, gla_ref, gz_ref, lru_ref):
    x = x_ref[...]
    ms = jnp.mean(x * x, axis=-1, keepdims=True)
    u = (x * lax.rsqrt(ms + RMS_EPS)) * a_ref[0] + s_ref[0]
    ub = u.astype(BF16)
    cos, se, so = cos_ref[...], se_ref[...], so_ref[...]
    n_rot = N_Q_HEADS + N_KV_HEADS
    for jp in range(n_rot // 2):
        z2 = jnp.dot(ub, w_ref[:, 2 * jp * HEAD_DIM:(2 * jp + 2) * HEAD_DIM], preferred_element_type=F32)
        for j in (2 * jp, 2 * jp + 1):
            zh = z2[:, (j - 2 * jp) * HEAD_DIM:(j - 2 * jp + 1) * HEAD_DIM]
            rot = zh * cos + pltpu.roll(zh, HEAD_DIM - 1, 1) * se + pltpu.roll(zh, 1, 1) * so
            qkv_ref[:, j * HEAD_DIM:(j + 1) * HEAD_DIM] = rot.astype(BF16)
    c0 = n_rot * HEAD_DIM
    qkv_ref[:, c0:QKV_W] = jnp.dot(ub, w_ref[:, c0:QKV_W], preferred_element_type=F32).astype(BF16)
    gla_ref[...] = jnp.dot(ub, w_ref[:, COL_GLA:COL_GZ], preferred_element_type=F32)
    gz_ref[...] = jnp.dot(ub, w_ref[:, COL_GZ:COL_LRU], preferred_element_type=F32)
    lru_ref[...] = jnp.dot(ub, w_ref[:, COL_LRU:W_IN_PACKED], preferred_element_type=F32)


def _in_proj(h, a, s, cos, se, so, w, seq_len):
    rows, D = h.shape
    tm = _tile(seq_len, 512)
    per = seq_len // tm
    row = lambda i: (i, 0)
    bat = lambda i: (i // per, 0, 0)
    tab = lambda i: (i % per, 0)
    return pl.pallas_call(
        _in_kernel,
        grid=(rows // tm,),
        in_specs=[
            pl.BlockSpec((tm, D), row),
            pl.BlockSpec((1, 1, D), bat),
            pl.BlockSpec((1, 1, D), bat),
            pl.BlockSpec((tm, HEAD_DIM), tab),
            pl.BlockSpec((tm, HEAD_DIM), tab),
            pl.BlockSpec((tm, HEAD_DIM), tab),
            pl.BlockSpec((D, W_IN_PACKED), lambda i: (0, 0), pipeline_mode=pl.Buffered(1)),
        ],
        out_specs=[
            pl.BlockSpec((tm, QKV_W), row),
            pl.BlockSpec((tm, GLA_W), row),
            pl.BlockSpec((tm, GZ_W), row),
            pl.BlockSpec((tm, LRU_W), row),
        ],
        out_shape=[
            jax.ShapeDtypeStruct((rows, QKV_W), BF16),
            jax.ShapeDtypeStruct((rows, GLA_W), F32),
            jax.ShapeDtypeStruct((rows, GZ_W), F32),
            jax.ShapeDtypeStruct((rows, LRU_W), F32),
        ],
        compiler_params=_cparams("parallel"),
        name="in_proj",
    )(h, a, s, cos, se, so, w)


def _attn_heads(q, k_loc, v_loc, valid, k_ctx, v_ctx, sink_ref, o_ref):
    scale = HEAD_DIM ** -0.5
    group = N_Q_HEADS // N_KV_HEADS
    nq = q.shape[0]
    nt = (((1,), (1,)), ((), ()))
    if valid is not None:
        valid = jnp.concatenate([valid] * group, axis=0)
    for hk in range(N_KV_HEADS):
        heads = range(hk * group, (hk + 1) * group)
        qg = jnp.concatenate([q[:, h * HEAD_DIM:(h + 1) * HEAD_DIM] for h in heads], axis=0)
        sink = jnp.concatenate([jnp.broadcast_to(sink_ref[h:h + 1, 0:1], (nq, 1)) for h in heads], axis=0)
        s_ctx = lax.dot_general(qg, k_ctx[hk], nt, preferred_element_type=F32) * scale
        m = jnp.maximum(jnp.max(s_ctx, axis=-1, keepdims=True), sink)
        if k_loc is not None:
            s_loc = lax.dot_general(qg, k_loc[hk], nt, preferred_element_type=F32) * scale
            s_loc = jnp.where(valid, s_loc, -jnp.inf)
            m = jnp.maximum(m, jnp.max(s_loc, axis=-1, keepdims=True))
        ov = jnp.dot(jnp.exp(s_ctx - m).astype(BF16), _with_ones(v_ctx[hk]), preferred_element_type=F32)
        if k_loc is not None:
            ov = ov + jnp.dot(jnp.exp(s_loc - m).astype(BF16), _with_ones(v_loc[hk]), preferred_element_type=F32)
        den = ov[:, HEAD_DIM:HEAD_DIM + 1] + jnp.exp(sink - m)
        o = (ov[:, :HEAD_DIM] * (1.0 / den)).astype(o_ref.dtype)
        for n, h in enumerate(heads):
            o_ref[:, h * HEAD_DIM:(h + 1) * HEAD_DIM] = o[n * nq:(n + 1) * nq]


def _with_ones(v):
    return jnp.concatenate([v, jnp.ones_like(v)], axis=1)


def _split_kv(kv):
    ks = [kv[:, h * HEAD_DIM:(h + 1) * HEAD_DIM] for h in range(N_KV_HEADS)]
    vs = [kv[:, ATTN_KV + h * HEAD_DIM:ATTN_KV + (h + 1) * HEAD_DIM] for h in range(N_KV_HEADS)]
    return ks, vs


def _attn_lat_kernel(q_ref, kvp_ref, kvc_ref, kvn_ref, kvx_ref, sink_ref, o_ref, *, nb):
    n = pl.program_id(1)
    kp, vp = _split_kv(kvp_ref[...])
    kc, vc = _split_kv(kvc_ref[...])
    kn, vn = _split_kv(kvn_ref[...])
    k_ctx, v_ctx = _split_kv(kvx_ref[...])
    k_loc = [jnp.concatenate([kp[h], kc[h], kn[h]], axis=0) for h in range(N_KV_HEADS)]
    v_loc = [jnp.concatenate([vp[h], vc[h], vn[h]], axis=0) for h in range(N_KV_HEADS)]
    i = lax.broadcasted_iota(I32, (ATTN_BLOCK, 3 * ATTN_BLOCK), 0)
    j = lax.broadcasted_iota(I32, (ATTN_BLOCK, 3 * ATTN_BLOCK), 1)
    valid = (j >= i) & (j <= i + 2 * WINDOW)
    valid = valid & ((j >= ATTN_BLOCK) | (n > 0)) & ((j < 2 * ATTN_BLOCK) | (n < nb - 1))
    _attn_heads(q_ref[...], k_loc, v_loc, valid, k_ctx, v_ctx, sink_ref, o_ref)


def _attn_ctx_kernel(q_ref, kvx_ref, sink_ref, o_ref):
    k_ctx, v_ctx = _split_kv(kvx_ref[...])
    _attn_heads(q_ref[...], None, None, None, k_ctx, v_ctx, sink_ref, o_ref)


def _attn_latent(qkv_l, qkv_c, sink_b, B, S, C):
    nb = S // ATTN_BLOCK
    kvw = 2 * ATTN_KV
    kvcol = ATTN_Q // kvw
    return pl.pallas_call(
        functools.partial(_attn_lat_kernel, nb=nb),
        grid=(B, nb),
        in_specs=[
            pl.BlockSpec((ATTN_BLOCK, ATTN_Q), lambda b, n: (b * nb + n, 0)),
            pl.BlockSpec((ATTN_BLOCK, kvw), lambda b, n: (b * nb + jnp.maximum(n - 1, 0), kvcol)),
            pl.BlockSpec((ATTN_BLOCK, kvw), lambda b, n: (b * nb + n, kvcol)),
            pl.BlockSpec((ATTN_BLOCK, kvw), lambda b, n: (b * nb + jnp.minimum(n + 1, nb - 1), kvcol)),
            pl.BlockSpec((C, kvw), lambda b, n: (b, kvcol)),
            pl.BlockSpec((SUBLANES, LANES), lambda b, n: (0, 0)),
        ],
        out_specs=pl.BlockSpec((ATTN_BLOCK, ATTN_Q), lambda b, n: (b * nb + n, 0)),
        out_shape=jax.ShapeDtypeStruct((B * S, ATTN_Q), BF16),
        compiler_params=_cparams("parallel", "parallel"),
        name="attn_latent",
    )(qkv_l, qkv_l, qkv_l, qkv_l, qkv_c, sink_b)


def _attn_context(qkv_c, sink_b, B, C):
    nb = C // ATTN_BLOCK
    kvw = 2 * ATTN_KV
    kvcol = ATTN_Q // kvw
    return pl.pallas_call(
        _attn_ctx_kernel,
        grid=(B, nb),
        in_specs=[
            pl.BlockSpec((ATTN_BLOCK, ATTN_Q), lambda b, n: (b * nb + n, 0)),
            pl.BlockSpec((C, kvw), lambda b, n: (b, kvcol)),
            pl.BlockSpec((SUBLANES, LANES), lambda b, n: (0, 0)),
        ],
        out_specs=pl.BlockSpec((ATTN_BLOCK, ATTN_Q), lambda b, n: (b * nb + n, 0)),
        out_shape=jax.ShapeDtypeStruct((B * C, ATTN_Q), BF16),
        compiler_params=_cparams("parallel", "parallel"),
        name="attn_context",
    )(qkv_c, qkv_c, sink_b)


def _gla_constants():
    Lc = GLA_CHUNK
    e = np.zeros((GLA_LEVELS + 2, Lc, Lc), np.float32)
    msk = np.zeros((GLA_LEVELS + 1, Lc, Lc), np.float32)
    t = np.arange(Lc)
    for l in range(GLA_LEVELS):
        m = 1 << l
        blk = t // (2 * m)
        upper = (t % (2 * m)) >= m
        bnd = blk * 2 * m + m
        r = t[None, :]
        eq = upper[:, None] & (r >= bnd[:, None]) & (r <= t[:, None])
        ek = (~upper)[:, None] & (r > t[:, None]) & (r <= bnd[:, None] - 1)
        e[l] = (eq | ek).astype(np.float32)
        msk[l] = (upper[:, None] & (~upper)[None, :] & (blk[:, None] == blk[None, :])).astype(np.float32)
    e[GLA_LEVELS] = (t[None, :] <= t[:, None]).astype(np.float32)
    e[GLA_LEVELS + 1] = (t[None, :] > t[:, None]).astype(np.float32)
    msk[GLA_LEVELS] = np.eye(Lc, dtype=np.float32)
    e2 = np.stack([e, e[:, ::-1, ::-1]]).reshape(2, (GLA_LEVELS + 2) * Lc, Lc)
    m2 = np.stack([msk, msk[:, ::-1, ::-1]])
    return e2, m2


def _gla_kernel(x_ref, gz_ref, w2_ref, bias_ref, e_ref, m_ref, s0_ref, o_ref, sfin_ref, s_scr, *, nch):
    i = pl.program_id(2)

    @pl.when(i == 0)
    def _():
        s_scr[...] = s0_ref[:, 0]

    for bb in range(x_ref.shape[0]):
        _gla_chunk(x_ref.at[bb], gz_ref.at[bb], w2_ref, bias_ref, e_ref, m_ref, o_ref.at[0, bb], s_scr.at[bb])

    @pl.when(i == nch - 1)
    def _():
        sfin_ref[:, 0] = s_scr[...]


def _gla_chunk(x_ref, gz_ref, w2_ref, bias_ref, e_ref, m_ref, o_ref, s_scr):
    Lc = GLA_CHUNK
    z = jnp.dot(gz_ref[...].astype(BF16), w2_ref[0], preferred_element_type=F32) + bias_ref[0]
    la = (jnp.minimum(z, 0.0) - jnp.log1p(jnp.exp(-jnp.abs(z)))) * (1.0 / GLA_TAU)
    la_hi = la.astype(BF16)
    la_lo = (la - la_hi.astype(F32)).astype(BF16)
    la2 = jnp.concatenate([la_hi, la_lo], axis=1)
    ex = jnp.dot(e_ref[0], la2, preferred_element_type=F32)
    decay = jnp.exp(ex[:, :GLA_QK] + ex[:, GLA_QK:])
    tot = lax.dot_general(la2, jnp.ones((Lc, LANES), BF16), (((0,), (0,)), ((), ())), preferred_element_type=F32)
    a_tot = jnp.exp(tot[:GLA_QK] + tot[GLA_QK:])

    q = x_ref[:, 0:GLA_QK] * (GLA_DK ** -0.5)
    k = x_ref[:, GLA_QK:2 * GLA_QK]
    v = x_ref[:, 2 * GLA_QK:2 * GLA_QK + GLA_V].astype(BF16)
    row_head = lax.broadcasted_iota(I32, (GLA_QK, GLA_DV), 0) >> 6
    first_of_pair = (lax.broadcasted_iota(I32, (Lc, LANES), 1) >> 6) == 0
    nt = (((1,), (1,)), ((), ()))

    att = [jnp.zeros((Lc, Lc), F32) for _ in range(GLA_HEADS)]
    for l in range(GLA_LEVELS + 1):
        if l < GLA_LEVELS:
            dl = decay[l * Lc:(l + 1) * Lc]
            ql = (q * dl).astype(BF16)
            kl = (k * dl).astype(BF16)
        else:
            ql = q.astype(BF16)
            kl = k.astype(BF16)
        ml = m_ref[0, l]
        for pair in range(GLA_HEADS // 2):
            qp = ql[:, pair * LANES:(pair + 1) * LANES]
            kp = kl[:, pair * LANES:(pair + 1) * LANES]
            zero = jnp.zeros_like(kp)
            kk = jnp.concatenate([jnp.where(first_of_pair, kp, zero), jnp.where(first_of_pair, zero, kp)], axis=0)
            sc = lax.dot_general(qp, kk, nt, preferred_element_type=F32)
            att[2 * pair] = att[2 * pair] + ml * sc[:, :Lc]
            att[2 * pair + 1] = att[2 * pair + 1] + ml * sc[:, Lc:]

    s_old = s_scr[...]
    s_b = s_old.astype(BF16)
    zero_s = jnp.zeros_like(s_b)
    s_bd = jnp.concatenate([jnp.where(row_head == h, s_b, zero_s) for h in range(GLA_HEADS)], axis=1)
    q_in = (q * decay[GLA_LEVELS * Lc:(GLA_LEVELS + 1) * Lc]).astype(BF16)
    k_out = (k * decay[(GLA_LEVELS + 1) * Lc:(GLA_LEVELS + 2) * Lc]).astype(BF16)
    o_inter = jnp.dot(q_in, s_bd, preferred_element_type=F32)
    contrib = lax.dot_general(k_out, v, (((0,), (0,)), ((), ())), preferred_element_type=F32)
    s_new = a_tot * s_old
    for h in range(GLA_HEADS):
        cols = slice(h * GLA_DV, (h + 1) * GLA_DV)
        o_ref[:, cols] = jnp.dot(att[h].astype(BF16), v[:, cols], preferred_element_type=F32) + o_inter[:, cols]
        s_new = s_new + jnp.where(row_head == h, contrib[:, cols], 0.0)
    s_scr[...] = s_new


def _gla(gla_arr, gz_arr, w2p, bias, e2, m2, s0, B, T):
    nch = T // GLA_CHUNK
    nseq = 2 if B % 2 == 0 else 1
    chunk = lambda d, i: jnp.where(d == 0, i, nch - 1 - i)
    st = lambda b, d, i: (b, d, 0, 0)
    o, s_fin = pl.pallas_call(
        functools.partial(_gla_kernel, nch=nch),
        grid=(B // nseq, 2, nch),
        in_specs=[
            pl.BlockSpec((nseq, GLA_CHUNK, 2 * GLA_QK + GLA_V), lambda b, d, i: (b, chunk(d, i), 0)),
            pl.BlockSpec((nseq, GLA_CHUNK, GZ_W), lambda b, d, i: (b, chunk(d, i), 0)),
            pl.BlockSpec((1, GZ_W, GLA_QK), lambda b, d, i: (d, 0, 0)),
            pl.BlockSpec((1, 1, GLA_QK), lambda b, d, i: (d, 0, 0)),
            pl.BlockSpec((1, (GLA_LEVELS + 2) * GLA_CHUNK, GLA_CHUNK), lambda b, d, i: (d, 0, 0)),
            pl.BlockSpec((1, GLA_LEVELS + 1, GLA_CHUNK, GLA_CHUNK), lambda b, d, i: (d, 0, 0, 0)),
            pl.BlockSpec((nseq, 1, GLA_QK, GLA_DV), st),
        ],
        out_specs=[
            pl.BlockSpec((1, nseq, GLA_CHUNK, GLA_V), lambda b, d, i: (d, b, chunk(d, i), 0)),
            pl.BlockSpec((nseq, 1, GLA_QK, GLA_DV), st),
        ],
        out_shape=[
            jax.ShapeDtypeStruct((2, B, T, GLA_V), F32),
            jax.ShapeDtypeStruct((B, 2, GLA_QK, GLA_DV), F32),
        ],
        scratch_shapes=[pltpu.VMEM((nseq, GLA_QK, GLA_DV), F32)],
        compiler_params=_cparams("parallel", "parallel", "arbitrary"),
        name="gla_scan",
    )(gla_arr.reshape(B, T, GLA_W), gz_arr.reshape(B, T, GZ_W), w2p, bias, e2, m2, s0)
    return o.reshape(2, B * T, GLA_V), s_fin


def _lru_kernel(x_ref, pv_ref, nx_ref, cw_ref, cb_ref, wg_ref, bg_ref, lam_ref, h0_ref, o_ref, hfin_ref, carry, a_s, u_s,
                h_s, *, nblk):
    d = pl.program_id(1)
    i = pl.program_id(2)
    T = LRU_CHUNK
    W = LRU_WIDTH

    @pl.when(i == 0)
    def _():
        carry[...] = jnp.broadcast_to(h0_ref[0, 0], (SUBLANES, W))

    li = jnp.where(d == 0, i, nblk - 1 - i)
    pv = jnp.where(li > 0, pv_ref[...], 0.0)
    nx = jnp.where(li < nblk - 1, nx_ref[...], 0.0)
    xe = jnp.concatenate([pv, x_ref[...], nx], axis=0)
    n_ext = T + 2 * SUBLANES
    win = lambda off: pltpu.roll(xe, n_ext - off, 0)[0:T]
    cw = cw_ref[...]
    xc = cb_ref[...] + win(6) * cw[0:1] + win(7) * cw[1:2] + xe[SUBLANES:SUBLANES + T] * cw[2:3] + win(9) * cw[3:4]

    g = jnp.dot(xc.astype(BF16), wg_ref[0], preferred_element_type=F32) + bg_ref[0]
    r = _sigmoid(g[:, :W])
    gi = _sigmoid(g[:, W:])
    log_a = (-LRU_C * _softplus(-lam_ref[0])) * r
    a = jnp.exp(log_a)
    u = jnp.sqrt(-jnp.tanh(log_a) * (a * a + 1.0)) * (gi * xc)
    panels = W // LANES
    for c in range(panels):
        a_s[c] = a[:, c * LANES:(c + 1) * LANES]
        u_s[c] = u[:, c * LANES:(c + 1) * LANES]

    groups = T // SUBLANES
    slab = lambda ref, j: jnp.concatenate([ref[c, pl.ds(j, groups, stride=SUBLANES), :] for c in range(panels)], axis=1)
    g_row = lax.broadcasted_iota(I32, (groups, W), 0)

    def scan(order, shift, edge):
        us = {order[0]: slab(u_s, order[0])}
        ps = {order[0]: slab(a_s, order[0])}
        for prev, j in zip(order[:-1], order[1:]):
            aj = slab(a_s, j)
            us[j] = slab(u_s, j) + aj * us[prev]
            ps[j] = aj * ps[prev]
        cu, cp = us[order[-1]], ps[order[-1]]
        sh = 1
        while sh < groups:
            ok = (g_row >= sh) if shift == 1 else (g_row < groups - sh)
            amt = sh if shift == 1 else groups - sh
            cu = cu + cp * jnp.where(ok, pltpu.roll(cu, amt, 0), 0.0)
            cp = cp * jnp.where(ok, pltpu.roll(cp, amt, 0), 1.0)
            sh *= 2
        c_out = cu + cp * carry[0:1]
        c_in = jnp.where(g_row == edge, carry[0:1], pltpu.roll(c_out, shift if shift == 1 else groups - 1, 0))
        for j in order:
            hj = us[j] + ps[j] * c_in
            for c in range(panels):
                h_s[c, pl.ds(j, groups, stride=SUBLANES), :] = hj[:, c * LANES:(c + 1) * LANES]
        o_ref[0] = jnp.concatenate([h_s[c] for c in range(panels)], axis=1)
        last = groups - 1 - edge
        carry[...] = jnp.broadcast_to(c_out[last:last + 1], (SUBLANES, W))

    @pl.when(d == 0)
    def _():
        scan(list(range(SUBLANES)), 1, 0)

    @pl.when(d == 1)
    def _():
        scan(list(reversed(range(SUBLANES))), -1, groups - 1)

    @pl.when(i == nblk - 1)
    def _():
        hfin_ref[0, 0] = carry[0:1]


def _lru(lru_arr, cw, cb, wg, bg, lam, h0, B, T):
    nblk = T // LRU_CHUNK
    per8 = LRU_CHUNK // SUBLANES
    n8 = B * T // SUBLANES
    blk = lambda b, d, i: b * nblk + jnp.where(d == 0, i, nblk - 1 - i)
    dirw = lambda b, d, i: (d, 0, 0)
    st = lambda b, d, i: (b, d, 0, 0)
    return pl.pallas_call(
        functools.partial(_lru_kernel, nblk=nblk),
        grid=(B, 2, nblk),
        in_specs=[
            pl.BlockSpec((LRU_CHUNK, LRU_WIDTH), lambda b, d, i: (blk(b, d, i), 0)),
            pl.BlockSpec((SUBLANES, LRU_WIDTH), lambda b, d, i: (jnp.maximum(blk(b, d, i) * per8 - 1, 0), 0)),
            pl.BlockSpec((SUBLANES, LRU_WIDTH), lambda b, d, i: (jnp.minimum((blk(b, d, i) + 1) * per8, n8 - 1), 0)),
            pl.BlockSpec((SUBLANES, LRU_WIDTH), lambda b, d, i: (0, 0)),
            pl.BlockSpec((1, LRU_WIDTH), lambda b, d, i: (0, 0)),
            pl.BlockSpec((1, LRU_WIDTH, 2 * LRU_WIDTH), dirw),
            pl.BlockSpec((1, 1, 2 * LRU_WIDTH), dirw),
            pl.BlockSpec((1, 1, LRU_WIDTH), dirw),
            pl.BlockSpec((1, 1, 1, LRU_WIDTH), st),
        ],
        out_specs=[
            pl.BlockSpec((1, LRU_CHUNK, LRU_WIDTH), lambda b, d, i: (d, blk(b, d, i), 0)),
            pl.BlockSpec((1, 1, 1, LRU_WIDTH), st),
        ],
        out_shape=[
            jax.ShapeDtypeStruct((2, B * T, LRU_WIDTH), F32),
            jax.ShapeDtypeStruct((B, 2, 1, LRU_WIDTH), F32),
        ],
        scratch_shapes=[pltpu.VMEM((SUBLANES, LRU_WIDTH), F32)]
        + [pltpu.VMEM((LRU_WIDTH // LANES, LRU_CHUNK, LANES), F32) for _ in range(3)],
        compiler_params=_cparams("parallel", "parallel", "arbitrary"),
        name="lru_scan",
    )(lru_arr, lru_arr, lru_arr, cw, cb, wg, bg, lam, h0)


def _out_kernel(h_ref, attn_ref, go_ref, gr_ref, gg_ref, lh_ref, lg_ref, w_ref, g1_ref, a2_ref, s2_ref, wr_ref, br_ref,
                vt_ref, lt_ref, hn_ref, v_ref, lgt_ref, *, n_main):
    i = pl.program_id(0)

    @pl.when(i < n_main)
    def _():
        _out_body(h_ref, attn_ref, go_ref, gr_ref, gg_ref, lh_ref, lg_ref, w_ref, g1_ref, a2_ref, s2_ref, wr_ref, br_ref,
                  hn_ref, v_ref, lgt_ref)

    @pl.when(i >= n_main)
    def _():
        v_ref[...] = vt_ref[...]
        lgt_ref[...] = lt_ref[...]


def _out_body(h_ref, attn_ref, go_ref, gr_ref, gg_ref, lh_ref, lg_ref, w_ref, g1_ref, a2_ref, s2_ref, wr_ref, br_ref,
              hn_ref, v_ref, lgt_ref):
    o = go_ref[0] + go_ref[1]
    gr = gr_ref[...]
    gate = gr * _sigmoid(gr)
    parts = []
    for hh in range(GLA_HEADS):
        oh = o[:, hh * GLA_DV:(hh + 1) * GLA_DV]
        y = (oh * lax.rsqrt(jnp.mean(oh * oh, axis=-1, keepdims=True) + RMS_EPS)) * gg_ref[...]
        parts.append(y * gate[:, hh * GLA_DV:(hh + 1) * GLA_DV])
    gla = jnp.concatenate(parts, axis=1).astype(BF16)
    lg = lg_ref[...]
    gelu = lg * (0.5 * (1.0 + jnp.tanh(np.sqrt(2.0 / np.pi).astype(np.float32) * (lg + 0.044715 * (lg * lg * lg)))))
    lru = ((lh_ref[0] + lh_ref[1]) * gelu).astype(BF16)
    y = jnp.dot(attn_ref[...], w_ref[0:ATTN_Q], preferred_element_type=F32)
    y = y + jnp.dot(gla, w_ref[ATTN_Q:ATTN_Q + GLA_V], preferred_element_type=F32)
    y = y + jnp.dot(lru, w_ref[ATTN_Q + GLA_V:D_MIX], preferred_element_type=F32)
    hn = h_ref[...] + g1_ref[0] * y
    hn_ref[...] = hn
    v = (hn * lax.rsqrt(jnp.mean(hn * hn, axis=-1, keepdims=True) + RMS_EPS)) * a2_ref[0] + s2_ref[0]
    _store_token_tiles(v_ref, _pack_bf16_pairs(v))
    v_hi = v.astype(BF16)
    v_lo = (v - v_hi.astype(F32)).astype(BF16)
    t = jnp.dot(v_hi, wr_ref[...], preferred_element_type=F32)
    t_lo = jnp.dot(v_lo, wr_ref[:, 0:LANES], preferred_element_type=F32)
    lgt_ref[...] = (t[:, :LANES] + t[:, LANES:]) + t_lo + br_ref[...]


def _mixer_out(h, attn, go, gla_arr, gg, lh, lru_arr, w_out, g1, a2, s2, wr, br, seq_len, tail):
    rows, D = h.shape
    tm = _tile(seq_len, 256)
    per = seq_len // tm
    n_main = rows // tm
    tw = tm * (D // 2 // LANES)
    if tail is None:
        tail = (jnp.zeros((tw, LANES), I32), jnp.zeros((tm, LANES), F32))
        n_tail = 0
    else:
        assert tail[1].shape[0] % tm == 0
        n_tail = tail[1].shape[0] // tm
    n_tok = rows + n_tail * tm
    main = lambda i: jnp.minimum(i, n_main - 1)
    row = lambda i: (main(i), 0)
    row3 = lambda i: (0, main(i), 0)
    bat = lambda i: (main(i) // per, 0, 0)
    const = lambda i: (0, 0)
    tok = lambda i: (i, 0)
    trow = lambda i: (jnp.maximum(i - n_main, 0), 0)
    in_specs = [
        pl.BlockSpec((tm, D), row),
        pl.BlockSpec((tm, ATTN_Q), row),
        pl.BlockSpec((2, tm, GLA_V), row3),
        pl.BlockSpec((tm, GLA_V), lambda i: (main(i), 2)),
        pl.BlockSpec((1, GLA_DV), const),
        pl.BlockSpec((2, tm, LRU_WIDTH), row3),
        pl.BlockSpec((tm, LRU_WIDTH), lambda i: (main(i), 1)),
        pl.BlockSpec((D_MIX, D), const, pipeline_mode=pl.Buffered(1)),
        pl.BlockSpec((1, 1, D), bat),
        pl.BlockSpec((1, 1, D), bat),
        pl.BlockSpec((1, 1, D), bat),
        pl.BlockSpec((D, 2 * LANES), const),
        pl.BlockSpec((1, LANES), const),
        pl.BlockSpec((tw, LANES), trow),
        pl.BlockSpec((tm, LANES), trow),
    ]
    args = [h, attn, go, gla_arr, gg, lh, lru_arr, w_out, g1, a2, s2, wr, br, tail[0], tail[1]]
    return pl.pallas_call(
        functools.partial(_out_kernel, n_main=n_main),
        grid=(n_main + n_tail,),
        in_specs=in_specs,
        out_specs=[
            pl.BlockSpec((tm, D), row),
            pl.BlockSpec((tw, LANES), tok),
            pl.BlockSpec((tm, LANES), tok),
        ],
        out_shape=[
            jax.ShapeDtypeStruct((rows, D), F32),
            jax.ShapeDtypeStruct((n_tok * (tw // tm), LANES), I32),
            jax.ShapeDtypeStruct((n_tok, LANES), F32),
        ],
        compiler_params=_cparams("arbitrary"),
        name="mixer_out",
    )(*args)


def _route_kernel(lg_ref, tri_ref, oi_ref, of_ref, cnt_ref, carry):
    i = pl.program_id(0)

    @pl.when(i == 0)
    def _():
        carry[...] = jnp.zeros_like(carry)

    lg = lg_ref[...]
    col = lax.broadcasted_iota(I32, lg.shape, 1)
    colf = col.astype(F32)
    big = float(LANES)
    is_g = col < N_GROUPS
    gm = jnp.max(jnp.where(is_g, lg, -jnp.inf), axis=-1, keepdims=True)
    eg = jnp.where(is_g, jnp.exp(lg - gm), 0.0)
    pg = eg / jnp.sum(eg, axis=-1, keepdims=True)
    p_grp = jnp.max(pg, axis=-1, keepdims=True)
    grp = jnp.min(jnp.where(is_g & (pg == p_grp), colf, big), axis=-1, keepdims=True).astype(I32)

    sel = (col >= N_GROUPS) & (col < N_GROUPS + N_EXPERTS) & (((col - N_GROUPS) >> 3) == grp)
    em = jnp.max(jnp.where(sel, lg, -jnp.inf), axis=-1, keepdims=True)
    ee = jnp.where(sel, jnp.exp(lg - em), 0.0)
    pe = ee / jnp.sum(ee, axis=-1, keepdims=True)
    p1 = jnp.max(jnp.where(sel, pe, -1.0), axis=-1, keepdims=True)
    c1 = jnp.min(jnp.where(sel & (pe == p1), colf, big), axis=-1, keepdims=True).astype(I32)
    rest = sel & (col != c1)
    p2 = jnp.max(jnp.where(rest, pe, -1.0), axis=-1, keepdims=True)
    c2 = jnp.min(jnp.where(rest & (pe == p2), colf, big), axis=-1, keepdims=True).astype(I32)
    e1 = c1 - N_GROUPS
    e2 = c2 - N_GROUPS
    den = p1 + p2
    g1 = p_grp * (p1 / den)
    g2 = p_grp * (p2 / den)

    hit1 = col == e1
    hit2 = col == e2
    oh = jnp.where(hit1 | hit2, 1.0, 0.0)
    before = jnp.dot(tri_ref[...], oh.astype(BF16), preferred_element_type=F32) + carry[0:1]
    r1 = jnp.sum(jnp.where(hit1, before, 0.0), axis=-1, keepdims=True).astype(I32)
    r2 = jnp.sum(jnp.where(hit2, before, 0.0), axis=-1, keepdims=True).astype(I32)
    new = carry[0:1] + jnp.sum(oh, axis=0, keepdims=True)
    carry[...] = jnp.broadcast_to(new, carry.shape)
    cnt_ref[...] = jnp.broadcast_to(new, cnt_ref.shape)

    zero = jnp.zeros_like(col)
    oi_ref[...] = jnp.where(col == 0, e1, jnp.where(col == 1, e2, jnp.where(col == 2, r1, jnp.where(col == 3, r2, zero))))
    of_ref[...] = jnp.where(col == 0, g1, jnp.where(col == 1, g2, 0.0))


def _route(logits):
    n_tok = logits.shape[0]
    tm = _tile(n_tok, 1024)
    row = lambda i: (i, 0)
    tri = jnp.tril(jnp.ones((tm, tm), BF16), -1)
    return pl.pallas_call(
        _route_kernel,
        grid=(n_tok // tm,),
        in_specs=[pl.BlockSpec((tm, LANES), row), pl.BlockSpec((tm, tm), lambda i: (0, 0))],
        out_specs=[
            pl.BlockSpec((tm, LANES), row),
            pl.BlockSpec((tm, LANES), row),
            pl.BlockSpec((SUBLANES, LANES), lambda i: (0, 0)),
        ],
        out_shape=[
            jax.ShapeDtypeStruct((n_tok, LANES), I32),
            jax.ShapeDtypeStruct((n_tok, LANES), F32),
            jax.ShapeDtypeStruct((SUBLANES, LANES), F32),
        ],
        scratch_shapes=[pltpu.VMEM((SUBLANES, LANES), F32)],
        compiler_params=_cparams("arbitrary"),
        name="route",
    )(logits, tri)


def _dot_casting(x, stage, dst):
    step = _tile(stage.shape[0], 256)
    acc = None
    for c in range(stage.shape[0] // step):
        rows = slice(c * step, (c + 1) * step)
        w = stage[rows, :].astype(BF16)
        dst[rows, :] = w
        part = jnp.dot(x[:, rows], w, preferred_element_type=F32)
        acc = part if acc is None else acc + part
    return acc


def _moe_ffn_kernel(be_ref, nu_ref, nx_ref, src_ref, srcn_ref, v_hbm, w1_hbm, w3_hbm, w2_hbm, o_ref,
                    xbuf, st1, st3, st2, w1s, w3s, w2s, sem_x, sem_w, *, layer, sub):
    i = pl.program_id(0)
    nu = nu_ref[0]
    rows = ROW_BLOCK * sub

    def gather(idx_ref, slot):
        for r in range(ROW_BLOCK):
            s = pl.multiple_of(idx_ref[0, 0, r] * sub, sub)
            pltpu.make_async_copy(v_hbm.at[pl.ds(s, sub)], xbuf.at[slot, pl.ds(r * sub, sub)], sem_x.at[slot]).start()

    def wait_rows(slot):
        pltpu.make_async_copy(v_hbm.at[pl.ds(0, rows)], xbuf.at[slot], sem_x.at[slot]).wait()

    def weight_copies(e):
        return (pltpu.make_async_copy(w1_hbm.at[layer, e], st1, sem_w.at[0]),
                pltpu.make_async_copy(w3_hbm.at[layer, e], st3, sem_w.at[1]),
                pltpu.make_async_copy(w2_hbm.at[layer, e], st2, sem_w.at[2]))

    @pl.when(i < nu)
    def _():
        e = be_ref[i]
        slot = i & 1

        @pl.when(i == 0)
        def _():
            for cp in weight_copies(e):
                cp.start(priority=1)
            gather(src_ref, 0)

        wait_rows(slot)
        gather(srcn_ref, 1 - slot)
        xb = _unpack_bf16_pairs(_load_token_tiles(xbuf.at[slot], ROW_BLOCK)).astype(BF16)
        first = (i == 0) | (e != be_ref[jnp.maximum(i - 1, 0)])

        def mlp(up1, up3, down):
            a = up1(xb)
            b = up3(xb)
            hmid = ((a * _sigmoid(a)) * b).astype(BF16)
            _store_token_tiles(o_ref, _pack_bf16_pairs(down(hmid)))

        @pl.when(first)
        def _():
            for cp in weight_copies(e):
                cp.wait()
            mlp(lambda x: _dot_casting(x, st1, w1s), lambda x: _dot_casting(x, st3, w3s),
                lambda x: _dot_casting(x, st2, w2s))

            @pl.when(nx_ref[i] >= 0)
            def _():
                for cp in weight_copies(nx_ref[i]):
                    cp.start(priority=1)

        @pl.when(jnp.logical_not(first))
        def _():
            mlp(lambda x: jnp.dot(x, w1s[...], preferred_element_type=F32),
                lambda x: jnp.dot(x, w3s[...], preferred_element_type=F32),
                lambda x: jnp.dot(x, w2s[...], preferred_element_type=F32))

        @pl.when(i == nu - 1)
        def _():
            wait_rows(1 - slot)

    @pl.when(i >= nu)
    def _():
        o_ref[...] = jnp.zeros_like(o_ref)


def _moe_experts(v_tiles, src, blk_e, n_used, nxt_e, w1, w3, w2, layer):
    D, FF = w1.shape[-2:]
    sub = D // 2 // LANES
    P = src.shape[0]
    nb = P // ROW_BLOCK
    tw = ROW_BLOCK * sub
    cur = lambda i, be, nu, nx: (jnp.minimum(i, nu[0] - 1), 0, 0)
    nxt = lambda i, be, nu, nx: (jnp.minimum(i + 1, nu[0] - 1), 0, 0)
    src3 = src.reshape(nb, 1, ROW_BLOCK)
    hbm = pl.BlockSpec(memory_space=pl.ANY)
    return pl.pallas_call(
        functools.partial(_moe_ffn_kernel, layer=layer, sub=sub),
        grid_spec=pltpu.PrefetchScalarGridSpec(
            num_scalar_prefetch=3,
            grid=(nb,),
            in_specs=[
                pl.BlockSpec((1, 1, ROW_BLOCK), cur, memory_space=pltpu.SMEM),
                pl.BlockSpec((1, 1, ROW_BLOCK), nxt, memory_space=pltpu.SMEM),
                hbm, hbm, hbm, hbm,
            ],
            out_specs=pl.BlockSpec((tw, LANES), lambda i, be, nu, nx: (i, 0)),
            scratch_shapes=[
                pltpu.VMEM((2, tw, LANES), I32),
                pltpu.VMEM((D, FF), F32), pltpu.VMEM((D, FF), F32), pltpu.VMEM((FF, D), F32),
                pltpu.VMEM((D, FF), BF16), pltpu.VMEM((D, FF), BF16), pltpu.VMEM((FF, D), BF16),
                pltpu.SemaphoreType.DMA((2,)), pltpu.SemaphoreType.DMA((3,)),
            ],
        ),
        out_shape=jax.ShapeDtypeStruct((P * sub, LANES), I32),
        compiler_params=_cparams("arbitrary"),
        name="moe_ffn",
    )(blk_e, n_used, nxt_e, src3, src3, v_tiles, w1, w3, w2)


def _combine_kernel(dest_ref, destn_ref, h_ref, gate_ref, g2_ref, fg_ref, ys_ref, o_ref, buf, sem, *, final, nsteps):
    i = pl.program_id(0)
    slot = i & 1
    tm = h_ref.shape[0]
    rows = buf.shape[2]
    sub = rows // tm

    def gather(idx_ref, sl):
        for j in range(tm * TOP_K):
            src = pl.multiple_of(idx_ref[0, 0, j] * sub, sub)
            cp = pltpu.make_async_copy(ys_ref.at[pl.ds(src, sub)], buf.at[sl, j % TOP_K, pl.ds((j // TOP_K) * sub, sub)],
                                       sem.at[sl])
            cp.start(priority=j % 2)

    def wait_rows(sl):
        for kk in range(TOP_K):
            pltpu.make_async_copy(ys_ref.at[pl.ds(0, rows)], buf.at[sl, kk], sem.at[sl]).wait()

    @pl.when(i == 0)
    def _():
        gather(dest_ref, 0)

    gather(destn_ref, 1 - slot)
    wait_rows(slot)
    gate = gate_ref[...]
    f = (_unpack_bf16_pairs(_load_token_tiles(buf.at[slot, 0], tm)) * gate[:, 0:1]
         + _unpack_bf16_pairs(_load_token_tiles(buf.at[slot, 1], tm)) * gate[:, 1:2])

    @pl.when(i == nsteps - 1)
    def _():
        wait_rows(1 - slot)

    hn = h_ref[...] + g2_ref[0] * f
    if final:
        hn = (hn * lax.rsqrt(jnp.mean(hn * hn, axis=-1, keepdims=True) + RMS_EPS)) * fg_ref[...]
    o_ref[...] = hn


def _combine(h, dest, gates, g2, fg, ys, seq_len, tok_off, final):
    rows, D = h.shape
    tm = _tile(seq_len, 256)
    per = seq_len // tm
    off = tok_off // tm
    n_blk_all = dest.shape[0] // (tm * TOP_K)
    nsteps = rows // tm
    dest3 = dest.reshape(n_blk_all, 1, tm * TOP_K)
    return pl.pallas_call(
        functools.partial(_combine_kernel, final=final, nsteps=nsteps),
        grid=(nsteps,),
        in_specs=[
            pl.BlockSpec((1, 1, tm * TOP_K), lambda i: (i + off, 0, 0), memory_space=pltpu.SMEM),
            pl.BlockSpec((1, 1, tm * TOP_K), lambda i: (jnp.minimum(i + 1, nsteps - 1) + off, 0, 0), memory_space=pltpu.SMEM),
            pl.BlockSpec((tm, D), lambda i: (i, 0)),
            pl.BlockSpec((tm, LANES), lambda i: (i + off, 0)),
            pl.BlockSpec((1, 1, D), lambda i: (i // per, 0, 0)),
            pl.BlockSpec((1, D), lambda i: (0, 0)),
            pl.BlockSpec(memory_space=pl.ANY),
        ],
        out_specs=pl.BlockSpec((tm, D), lambda i: (i, 0)),
        out_shape=jax.ShapeDtypeStruct((rows, D), F32),
        scratch_shapes=[pltpu.VMEM((2, TOP_K, tm * (D // 2 // LANES), LANES), I32), pltpu.SemaphoreType.DMA((2,))],
        compiler_params=_cparams("arbitrary"),
        name="combine",
    )(dest3, dest3, h, gates, g2, fg, ys)


def _rope_tables(S):
    rows = S // GRID_W
    row = jnp.repeat(jnp.arange(rows, dtype=F32), GRID_W)
    col = jnp.tile(jnp.arange(GRID_W, dtype=F32), rows)
    n_freq = HEAD_DIM // 4
    inv = ROPE_THETA ** (-jnp.arange(n_freq, dtype=F32) / n_freq)
    ang = jnp.concatenate([row[:, None] * inv, col[:, None] * inv], axis=-1)
    cos = jnp.repeat(jnp.cos(ang), 2, axis=1)
    sin = jnp.repeat(jnp.sin(ang), 2, axis=1)
    even = (jnp.arange(HEAD_DIM) % 2) == 0
    return cos, jnp.where(even, -sin, 0.0), jnp.where(even, 0.0, sin)


def _block_diag(w):
    n, c, _ = w.shape
    eye = jnp.eye(n, dtype=w.dtype)
    return (eye[:, None, :, None] * w[:, :, None, :]).reshape(n * c, n * c)


def _slot_plan(ids, counts_row, n_tok):
    e = ids[:, 0:TOP_K]
    rank = ids[:, TOP_K:2 * TOP_K]
    counts = counts_row[:N_EXPERTS].astype(I32)
    padded = (counts + ROW_BLOCK - 1) // ROW_BLOCK * ROW_BLOCK
    pad_end = jnp.cumsum(padded)
    pad_start = pad_end - padded
    dest = (pad_start[e] + rank).reshape(n_tok * TOP_K)
    n_blocks = -(-(n_tok * TOP_K + N_EXPERTS * (ROW_BLOCK - 1)) // ROW_BLOCK)
    starts = jnp.arange(n_blocks, dtype=I32) * ROW_BLOCK
    blk_e = jnp.minimum(jnp.sum(pad_end[None, :] <= starts[:, None], axis=1), N_EXPERTS - 1).astype(I32)
    n_used = (pad_end[-1] // ROW_BLOCK).astype(I32).reshape(1)
    run_next = pad_end[blk_e] // ROW_BLOCK
    nxt_e = jnp.where(run_next < n_used[0], blk_e[jnp.minimum(run_next, n_blocks - 1)], -1).astype(I32)
    src = jnp.zeros((n_blocks * ROW_BLOCK,), I32).at[dest].set(jnp.arange(n_tok * TOP_K, dtype=I32) // TOP_K,
                                                                 unique_indices=True)
    return dest, src, blk_e, n_used, nxt_e


def kernel(x, c, ctx, c_ctx, ada_w, ada_b, norm_mix_g, norm_ffn_g, w_in, attn_sink, gla_gate_w2, gla_gate_b, gla_norm_g, lru_conv_w, lru_conv_b, lru_wa, lru_ba, lru_wx, lru_bx, lru_lambda, w_out, router_g_w, router_g_b, router_e_w, router_e_b, moe_w1, moe_w3, moe_w2, final_norm_g):
    B, S, D = x.shape
    C = ctx.shape[1]
    L = ada_w.shape[0]
    assert S % max(ATTN_BLOCK, GLA_CHUNK, LRU_CHUNK, GRID_W) == 0 and C % max(ATTN_BLOCK, GLA_CHUNK, LRU_CHUNK) == 0
    assert B + 1 <= SUBLANES and D % (2 * LANES) == 0

    cond = jnp.concatenate([c, c_ctx[None], jnp.zeros((SUBLANES - B - 1, D), F32)], axis=0)
    mods = _ada_mods(cond, ada_w, ada_b).reshape(L, SUBLANES, 6, D)

    cos_l, se_l, so_l = _rope_tables(S)
    cos_c = jnp.ones((C, HEAD_DIM), F32)
    zero_c = jnp.zeros((C, HEAD_DIM), F32)
    e2_np, m2_np = _gla_constants()
    gla_e = jnp.asarray(e2_np, BF16)
    gla_m = jnp.asarray(m2_np, F32)

    h_lat = x.reshape(B * S, D)
    h_ctx = ctx.reshape(B * C, D)
    out = None
    for l in range(L):
        last = l == L - 1
        ml = mods[l, :B]
        mc = jnp.broadcast_to(mods[l, B][None], (B, 6, D))
        per_b = lambda m, j: m[:, j][:, None, :]
        a1_l, a1_c = [(1.0 + per_b(m, 1)) * norm_mix_g[l] for m in (ml, mc)]
        a2_l, a2_c = [(1.0 + per_b(m, 4)) * norm_ffn_g[l] for m in (ml, mc)]

        w = w_in[l]
        c_gz = ATTN_Q + 2 * ATTN_KV + 2 * GLA_QK + 2 * GLA_V
        w_packed = jnp.concatenate(
            [w[:, :c_gz], w[:, c_gz:c_gz + 2 * GLA_RANK], jnp.zeros((D, GZ_W - 2 * GLA_RANK), F32), w[:, c_gz + 2 * GLA_RANK:]],
            axis=1).astype(BF16)
        qkv_l, gla_l, gz_l, lru_l = _in_proj(h_lat, a1_l, per_b(ml, 0), cos_l, se_l, so_l, w_packed, S)
        qkv_c, gla_c, gz_c, lru_c = _in_proj(h_ctx, a1_c, per_b(mc, 0), cos_c, zero_c, zero_c, w_packed, C)

        sink_b = jnp.broadcast_to(attn_sink[l][:, None], (N_Q_HEADS, LANES)).astype(F32)
        attn_l = _attn_latent(qkv_l, qkv_c, sink_b, B, S, C)

        w2p = jnp.zeros((2, GZ_W, GLA_QK), F32)
        w2p = w2p.at[0, :GLA_RANK].set(gla_gate_w2[l, 0]).at[1, GLA_RANK:2 * GLA_RANK].set(gla_gate_w2[l, 1]).astype(BF16)
        gbias = gla_gate_b[l].reshape(2, 1, GLA_QK)
        s_zero = jnp.zeros((B, 2, GLA_QK, GLA_DV), F32)
        go_c, s_ctx = _gla(gla_c, gz_c, w2p, gbias, gla_e, gla_m, s_zero, B, C)
        go_l, _ = _gla(gla_l, gz_l, w2p, gbias, gla_e, gla_m, s_ctx, B, S)

        cw = jnp.concatenate([lru_conv_w[l], jnp.zeros((SUBLANES - CONV_W, LRU_WIDTH), F32)], axis=0)
        cb = lru_conv_b[l].reshape(1, LRU_WIDTH)
        wg = jnp.stack([jnp.concatenate([_block_diag(lru_wa[l, d]), _block_diag(lru_wx[l, d])], axis=1) for d in range(2)]).astype(BF16)
        bg = jnp.concatenate([lru_ba[l], lru_bx[l]], axis=1).reshape(2, 1, 2 * LRU_WIDTH)
        lam = lru_lambda[l].reshape(2, 1, LRU_WIDTH)
        h_zero = jnp.zeros((B, 2, 1, LRU_WIDTH), F32)
        lh_c, hs_ctx = _lru(lru_c, cw, cb, wg, bg, lam, h_zero, B, C)
        lh_l, _ = _lru(lru_l, cw, cb, wg, bg, lam, hs_ctx, B, S)

        wo = w_out[l].astype(BF16)
        gg = gla_norm_g[l].reshape(1, GLA_DV)
        wr = jnp.concatenate([router_g_w[l], router_e_w[l], jnp.zeros((D, LANES - N_GROUPS - N_EXPERTS), F32)], axis=1)
        wr_hi = wr.astype(BF16)
        wr = jnp.concatenate([wr_hi, (wr - wr_hi.astype(F32)).astype(BF16)], axis=1)
        br =jnp.concatenate([router_g_b[l], router_e_b[l], jnp.zeros((LANES - N_GROUPS - N_EXPERTS,), F32)]).reshape(1, LANES)
        n_tok = B * S if last else B * (S + C)
        tail = None
        if not last:
            attn_c = _attn_context(qkv_c, sink_b, B, C)
            h_ctx, v_c, lgt_c = _mixer_out(h_ctx, attn_c, go_c, gla_c, gg, lh_c, lru_c, wo, per_b(mc, 2), a2_c, per_b(mc, 3),
                                           wr, br, C, None)
            tail = (v_c, lgt_c)
        h_lat, v_all, lgt_all = _mixer_out(h_lat, attn_l, go_l, gla_l, gg, lh_l, lru_l, wo, per_b(ml, 2), a2_l, per_b(ml, 3),
                                           wr, br, S, tail)

        ids, gates, counts = _route(lgt_all)
        dest, src, blk_e, n_used, nxt_e = _slot_plan(ids, counts[0], n_tok)
        ys = _moe_experts(v_all, src, blk_e, n_used, nxt_e, moe_w1, moe_w3, moe_w2, l)
        fg = final_norm_g.reshape(1, D)
        h_lat = _combine(h_lat, dest, gates, per_b(ml, 5), fg, ys, S, 0, last)
        if not last:
            h_ctx = _combine(h_ctx, dest, gates, per_b(mc, 5), fg, ys, C, B * S, False)
        out = h_lat
    return out.reshape(B, S, D)
```

```python
import functools

import numpy as np
import jax
import jax.numpy as jnp
from jax import lax
from jax.experimental import pallas as pl
from jax.experimental.pallas import tpu as pltpu

F32 = jnp.float32
BF16 = jnp.bfloat16
I32 = jnp.int32

GRID_W = 64
RMS_EPS = 1e-6
N_Q_HEADS = 8
N_KV_HEADS = 2
HEAD_DIM = 128
WINDOW = 128
ATTN_BLOCK = 128
ROPE_THETA = 10000.0
GLA_HEADS = 4
GLA_DK = 64
GLA_DV = 128
GLA_RANK = 16
GLA_TAU = 16.0
LRU_WIDTH = 512
LRU_BLOCKS = 8
LRU_C = 8.0
CONV_W = 4
N_GROUPS = 4
EXPERTS_PER_GROUP = 8
N_EXPERTS = 32
TOP_K = 2
ATTN_Q = N_Q_HEADS * HEAD_DIM
ATTN_KV = N_KV_HEADS * HEAD_DIM
GLA_QK = GLA_HEADS * GLA_DK
GLA_V = GLA_HEADS * GLA_DV
D_MIX = ATTN_Q + GLA_V + LRU_WIDTH

LANES = 128
SUBLANES = 8
VMEM_LIMIT_BYTES = 56 * 1024 * 1024

QKV_W = ATTN_Q + 2 * ATTN_KV
GLA_W = 2 * GLA_QK + 2 * GLA_V
GZ_W = LANES
LRU_W = 2 * LRU_WIDTH
COL_GLA = QKV_W
COL_GZ = COL_GLA + GLA_W
COL_LRU = COL_GZ + GZ_W
W_IN_PACKED = COL_LRU + LRU_W

GLA_CHUNK = 128
GLA_LEVELS = 7
LRU_CHUNK = 256
ROW_BLOCK = 256


def _cparams(*sem):
    return pltpu.CompilerParams(dimension_semantics=sem, vmem_limit_bytes=VMEM_LIMIT_BYTES)


def _tile(n, pref):
    t = min(n, pref)
    while n % t:
        t -= SUBLANES
    return t


def _sigmoid(x):
    return 0.5 * (jnp.tanh(0.5 * x) + 1.0)


def _softplus(x):
    return jnp.maximum(x, 0.0) + jnp.log1p(jnp.exp(-jnp.abs(x)))


def _pack_bf16_pairs(x):
    n = x.shape[1] // 2
    lo = lax.bitcast_convert_type(x[:, :n].astype(BF16).astype(F32), I32)
    hi = lax.bitcast_convert_type(x[:, n:].astype(BF16).astype(F32), I32)
    return hi | lax.shift_right_logical(lo, jnp.full(lo.shape, 16, I32))


def _unpack_bf16_pairs(p):
    lo = lax.bitcast_convert_type(lax.shift_left(p, jnp.full(p.shape, 16, I32)), F32)
    hi = lax.bitcast_convert_type(p & jnp.int32(-65536), F32)
    return jnp.concatenate([lo, hi], axis=1)


def _store_token_tiles(ref, packed):
    m = packed.shape[0]
    sub = packed.shape[1] // LANES
    for s in range(sub):
        ref[pl.ds(s, m, stride=sub), :] = packed[:, s * LANES:(s + 1) * LANES]


def _load_token_tiles(ref, m):
    sub = ref.shape[0] // m
    return jnp.concatenate([ref[pl.ds(s, m, stride=sub), :] for s in range(sub)], axis=1)


def _ada_kernel(c_ref, w_ref, b_ref, o_ref):
    c = c_ref[...]
    s = c * _sigmoid(c)
    o_ref[0] = jnp.dot(s.astype(BF16), w_ref[0].astype(BF16), preferred_element_type=F32) + b_ref[0]


def _ada_mods(cond, ada_w, ada_b):
    L, D, D6 = ada_w.shape
    tn = _tile(D6, 1536)
    while tn % LANES:
        tn -= SUBLANES
    return pl.pallas_call(
        _ada_kernel,
        grid=(L, D6 // tn),
        in_specs=[
            pl.BlockSpec((SUBLANES, D), lambda l, j: (0, 0)),
            pl.BlockSpec((1, D, tn), lambda l, j: (l, 0, j)),
            pl.BlockSpec((1, 1, tn), lambda l, j: (l, 0, j)),
        ],
        out_specs=pl.BlockSpec((1, SUBLANES, tn), lambda l, j: (l, 0, j)),
        out_shape=jax.ShapeDtypeStruct((L, SUBLANES, D6), F32),
        compiler_params=_cparams("parallel", "parallel"),
        name="ada_mods",
    )(cond, ada_w, ada_b.reshape(L, 1, D6))


def _in_kernel(x_ref, a_ref, s_ref, cos_ref, se_ref, so_ref, w_ref, qkv_ref, gla_ref, gz_ref, lru_ref):
    x = x_ref[...]
    ms = jnp.mean(x * x, axis=-1, keepdims=True)
    u = (x * lax.rsqrt(ms + RMS_EPS)) * a_ref[0] + s_ref[0]
    ub = u.astype(BF16)
    cos, se, so = cos_ref[...], se_ref[...], so_ref[...]
    n_rot = N_Q_HEADS + N_KV_HEADS
    for jp in range(n_rot // 2):
        z2 = jnp.dot(ub, w_ref[:, 2 * jp * HEAD_DIM:(2 * jp + 2) * HEAD_DIM], preferred_element_type=F32)
        for j in (2 * jp, 2 * jp + 1):
            zh = z2[:, (j - 2 * jp) * HEAD_DIM:(j - 2 * jp + 1) * HEAD_DIM]
            rot = zh * cos + pltpu.roll(zh, HEAD_DIM - 1, 1) * se + pltpu.roll(zh, 1, 1) * so
            qkv_ref[:, j * HEAD_DIM:(j + 1) * HEAD_DIM] = rot.astype(BF16)
    c0 = n_rot * HEAD_DIM
    qkv_ref[:, c0:QKV_W] = jnp.dot(ub, w_ref[:, c0:QKV_W], preferred_element_type=F32).astype(BF16)
    gla_ref[...] = jnp.dot(ub, w_ref[:, COL_GLA:COL_GZ], preferred_element_type=F32)
    gz_ref[...] = jnp.dot(ub, w_ref[:, COL_GZ:COL_LRU], preferred_element_type=F32)
    lru_ref[...] = jnp.dot(ub, w_ref[:, COL_LRU:W_IN_PACKED], preferred_element_type=F32)


def _in_proj(h, a, s, cos, se, so, w, seq_len):
    rows, D = h.shape
    tm = _tile(seq_len, 512)
    per = seq_len // tm
    row = lambda i: (i, 0)
    bat = lambda i: (i // per, 0, 0)
    tab = lambda i: (i % per, 0)
    return pl.pallas_call(
        _in_kernel,
        grid=(rows // tm,),
        in_specs=[
            pl.BlockSpec((tm, D), row),
            pl.BlockSpec((1, 1, D), bat),
            pl.BlockSpec((1, 1, D), bat),
            pl.BlockSpec((tm, HEAD_DIM), tab),
            pl.BlockSpec((tm, HEAD_DIM), tab),
            pl.BlockSpec((tm, HEAD_DIM), tab),
            pl.BlockSpec((D, W_IN_PACKED), lambda i: (0, 0), pipeline_mode=pl.Buffered(1)),
        ],
        out_specs=[
            pl.BlockSpec((tm, QKV_W), row),
            pl.BlockSpec((tm, GLA_W), row),
            pl.BlockSpec((tm, GZ_W), row),
            pl.BlockSpec((tm, LRU_W), row),
        ],
        out_shape=[
            jax.ShapeDtypeStruct((rows, QKV_W), BF16),
            jax.ShapeDtypeStruct((rows, GLA_W), F32),
            jax.ShapeDtypeStruct((rows, GZ_W), F32),
            jax.ShapeDtypeStruct((rows, LRU_W), F32),
        ],
        compiler_params=_cparams("parallel"),
        name="in_proj",
    )(h, a, s, cos, se, so, w)


def _attn_heads(q, k_loc, v_loc, valid, k_ctx, v_ctx, sink_ref, o_ref):
    scale = HEAD_DIM ** -0.5
    group = N_Q_HEADS // N_KV_HEADS
    nq = q.shape[0]
    nt = (((1,), (1,)), ((), ()))
    if valid is not None:
        valid = jnp.concatenate([valid] * group, axis=0)
    for hk in range(N_KV_HEADS):
        heads = range(hk * group, (hk + 1) * group)
        qg = jnp.concatenate([q[:, h * HEAD_DIM:(h + 1) * HEAD_DIM] for h in heads], axis=0)
        sink = jnp.concatenate([jnp.broadcast_to(sink_ref[h:h + 1, 0:1], (nq, 1)) for h in heads], axis=0)
        s_ctx = lax.dot_general(qg, k_ctx[hk], nt, preferred_element_type=F32) * scale
        m = jnp.maximum(jnp.max(s_ctx, axis=-1, keepdims=True), sink)
        if k_loc is not None:
            s_loc = lax.dot_general(qg, k_loc[hk], nt, preferred_element_type=F32) * scale
            s_loc = jnp.where(valid, s_loc, -jnp.inf)
            m = jnp.maximum(m, jnp.max(s_loc, axis=-1, keepdims=True))
        ov = jnp.dot(jnp.exp(s_ctx - m).astype(BF16), _with_ones(v_ctx[hk]), preferred_element_type=F32)
        if k_loc is not None:
            ov = ov + jnp.dot(jnp.exp(s_loc - m).astype(BF16), _with_ones(v_loc[hk]), preferred_element_type=F32)
        den = ov[:, HEAD_DIM:HEAD_DIM + 1] + jnp.exp(sink - m)
        o = (ov[:, :HEAD_DIM] * (1.0 / den)).astype(o_ref.dtype)
        for n, h in enumerate(heads):
            o_ref[:, h * HEAD_DIM:(h + 1) * HEAD_DIM] = o[n * nq:(n + 1) * nq]


def _with_ones(v):
    return jnp.concatenate([v, jnp.ones_like(v)], axis=1)


def _split_kv(kv):
    ks = [kv[:, h * HEAD_DIM:(h + 1) * HEAD_DIM] for h in range(N_KV_HEADS)]
    vs = [kv[:, ATTN_KV + h * HEAD_DIM:ATTN_KV + (h + 1) * HEAD_DIM] for h in range(N_KV_HEADS)]
    return ks, vs


def _attn_lat_kernel(q_ref, kvp_ref, kvc_ref, kvn_ref, kvx_ref, sink_ref, o_ref, *, nb):
    n = pl.program_id(1)
    kp, vp = _split_kv(kvp_ref[...])
    kc, vc = _split_kv(kvc_ref[...])
    kn, vn = _split_kv(kvn_ref[...])
    k_ctx, v_ctx = _split_kv(kvx_ref[...])
    k_loc = [jnp.concatenate([kp[h], kc[h], kn[h]], axis=0) for h in range(N_KV_HEADS)]
    v_loc = [jnp.concatenate([vp[h], vc[h], vn[h]], axis=0) for h in range(N_KV_HEADS)]
    i = lax.broadcasted_iota(I32, (ATTN_BLOCK, 3 * ATTN_BLOCK), 0)
    j = lax.broadcasted_iota(I32, (ATTN_BLOCK, 3 * ATTN_BLOCK), 1)
    valid = (j >= i) & (j <= i + 2 * WINDOW)
    valid = valid & ((j >= ATTN_BLOCK) | (n > 0)) & ((j < 2 * ATTN_BLOCK) | (n < nb - 1))
    _attn_heads(q_ref[...], k_loc, v_loc, valid, k_ctx, v_ctx, sink_ref, o_ref)


def _attn_ctx_kernel(q_ref, kvx_ref, sink_ref, o_ref):
    k_ctx, v_ctx = _split_kv(kvx_ref[...])
    _attn_heads(q_ref[...], None, None, None, k_ctx, v_ctx, sink_ref, o_ref)


def _attn_latent(qkv_l, qkv_c, sink_b, B, S, C):
    nb = S // ATTN_BLOCK
    kvw = 2 * ATTN_KV
    kvcol = ATTN_Q // kvw
    return pl.pallas_call(
        functools.partial(_attn_lat_kernel, nb=nb),
        grid=(B, nb),
        in_specs=[
            pl.BlockSpec((ATTN_BLOCK, ATTN_Q), lambda b, n: (b * nb + n, 0)),
            pl.BlockSpec((ATTN_BLOCK, kvw), lambda b, n: (b * nb + jnp.maximum(n - 1, 0), kvcol)),
            pl.BlockSpec((ATTN_BLOCK, kvw), lambda b, n: (b * nb + n, kvcol)),
            pl.BlockSpec((ATTN_BLOCK, kvw), lambda b, n: (b * nb + jnp.minimum(n + 1, nb - 1), kvcol)),
            pl.BlockSpec((C, kvw), lambda b, n: (b, kvcol)),
            pl.BlockSpec((SUBLANES, LANES), lambda b, n: (0, 0)),
        ],
        out_specs=pl.BlockSpec((ATTN_BLOCK, ATTN_Q), lambda b, n: (b * nb + n, 0)),
        out_shape=jax.ShapeDtypeStruct((B * S, ATTN_Q), BF16),
        compiler_params=_cparams("parallel", "parallel"),
        name="attn_latent",
    )(qkv_l, qkv_l, qkv_l, qkv_l, qkv_c, sink_b)


def _attn_context(qkv_c, sink_b, B, C):
    nb = C // ATTN_BLOCK
    kvw = 2 * ATTN_KV
    kvcol = ATTN_Q // kvw
    return pl.pallas_call(
        _attn_ctx_kernel,
        grid=(B, nb),
        in_specs=[
            pl.BlockSpec((ATTN_BLOCK, ATTN_Q), lambda b, n: (b * nb + n, 0)),
            pl.BlockSpec((C, kvw), lambda b, n: (b, kvcol)),
            pl.BlockSpec((SUBLANES, LANES), lambda b, n: (0, 0)),
        ],
        out_specs=pl.BlockSpec((ATTN_BLOCK, ATTN_Q), lambda b, n: (b * nb + n, 0)),
        out_shape=jax.ShapeDtypeStruct((B * C, ATTN_Q), BF16),
        compiler_params=_cparams("parallel", "parallel"),
        name="attn_context",
    )(qkv_c, qkv_c, sink_b)


def _gla_constants():
    Lc = GLA_CHUNK
    e = np.zeros((GLA_LEVELS + 2, Lc, Lc), np.float32)
    msk = np.zeros((GLA_LEVELS + 1, Lc, Lc), np.float32)
    t = np.arange(Lc)
    for l in range(GLA_LEVELS):
        m = 1 << l
        blk = t // (2 * m)
        upper = (t % (2 * m)) >= m
        bnd = blk * 2 * m + m
        r = t[None, :]
        eq = upper[:, None] & (r >= bnd[:, None]) & (r <= t[:, None])
        ek = (~upper)[:, None] & (r > t[:, None]) & (r <= bnd[:, None] - 1)
        e[l] = (eq | ek).astype(np.float32)
        msk[l] = (upper[:, None] & (~upper)[None, :] & (blk[:, None] == blk[None, :])).astype(np.float32)
    e[GLA_LEVELS] = (t[None, :] <= t[:, None]).astype(np.float32)
    e[GLA_LEVELS + 1] = (t[None, :] > t[:, None]).astype(np.float32)
    msk[GLA_LEVELS] = np.eye(Lc, dtype=np.float32)
    e2 = np.stack([e, e[:, ::-1, ::-1]]).reshape(2, (GLA_LEVELS + 2) * Lc, Lc)
    m2 = np.stack([msk, msk[:, ::-1, ::-1]])
    return e2, m2


def _gla_kernel(x_ref, gz_ref, w2_ref, bias_ref, e_ref, m_ref, s0_ref, o_ref, sfin_ref, s_scr, *, nch):
    i = pl.program_id(2)

    @pl.when(i == 0)
    def _():
        s_scr[...] = s0_ref[:, 0]

    for bb in range(x_ref.shape[0]):
        _gla_chunk(x_ref.at[bb], gz_ref.at[bb], w2_ref, bias_ref, e_ref, m_ref, o_ref.at[0, bb], s_scr.at[bb])

    @pl.when(i == nch - 1)
    def _():
        sfin_ref[:, 0] = s_scr[...]


def _gla_chunk(x_ref, gz_ref, w2_ref, bias_ref, e_ref, m_ref, o_ref, s_scr):
    Lc = GLA_CHUNK
    z = jnp.dot(gz_ref[...].astype(BF16), w2_ref[0], preferred_element_type=F32) + bias_ref[0]
    la = (jnp.minimum(z, 0.0) - jnp.log1p(jnp.exp(-jnp.abs(z)))) * (1.0 / GLA_TAU)
    la_hi = la.astype(BF16)
    la_lo = (la - la_hi.astype(F32)).astype(BF16)
    la2 = jnp.concatenate([la_hi, la_lo], axis=1)
    ex = jnp.dot(e_ref[0], la2, preferred_element_type=F32)
    decay = jnp.exp(ex[:, :GLA_QK] + ex[:, GLA_QK:])
    tot = lax.dot_general(la2, jnp.ones((Lc, LANES), BF16), (((0,), (0,)), ((), ())), preferred_element_type=F32)
    a_tot = jnp.exp(tot[:GLA_QK] + tot[GLA_QK:])

    q = x_ref[:, 0:GLA_QK] * (GLA_DK ** -0.5)
    k = x_ref[:, GLA_QK:2 * GLA_QK]
    v = x_ref[:, 2 * GLA_QK:2 * GLA_QK + GLA_V].astype(BF16)
    row_head = lax.broadcasted_iota(I32, (GLA_QK, GLA_DV), 0) >> 6
    first_of_pair = (lax.broadcasted_iota(I32, (Lc, LANES), 1) >> 6) == 0
    nt = (((1,), (1,)), ((), ()))

    att = [jnp.zeros((Lc, Lc), F32) for _ in range(GLA_HEADS)]
    for l in range(GLA_LEVELS + 1):
        if l < GLA_LEVELS:
            dl = decay[l * Lc:(l + 1) * Lc]
            ql = (q * dl).astype(BF16)
            kl = (k * dl).astype(BF16)
        else:
            ql = q.astype(BF16)
            kl = k.astype(BF16)
        ml = m_ref[0, l]
        for pair in range(GLA_HEADS // 2):
            qp = ql[:, pair * LANES:(pair + 1) * LANES]
            kp = kl[:, pair * LANES:(pair + 1) * LANES]
            zero = jnp.zeros_like(kp)
            kk = jnp.concatenate([jnp.where(first_of_pair, kp, zero), jnp.where(first_of_pair, zero, kp)], axis=0)
            sc = lax.dot_general(qp, kk, nt, preferred_element_type=F32)
            att[2 * pair] = att[2 * pair] + ml * sc[:, :Lc]
            att[2 * pair + 1] = att[2 * pair + 1] + ml * sc[:, Lc:]

    s_old = s_scr[...]
    s_b = s_old.astype(BF16)
    zero_s = jnp.zeros_like(s_b)
    s_bd = jnp.concatenate([jnp.where(row_head == h, s_b, zero_s) for h in range(GLA_HEADS)], axis=1)
    q_in = (q * decay[GLA_LEVELS * Lc:(GLA_LEVELS + 1) * Lc]).astype(BF16)
    k_out = (k * decay[(GLA_LEVELS + 1) * Lc:(GLA_LEVELS + 2) * Lc]).astype(BF16)
    o_inter = jnp.dot(q_in, s_bd, preferred_element_type=F32)
    contrib = lax.dot_general(k_out, v, (((0,), (0,)), ((), ())), preferred_element_type=F32)
    s_new = a_tot * s_old
    for h in range(GLA_HEADS):
        cols = slice(h * GLA_DV, (h + 1) * GLA_DV)
        o_ref[:, cols] = jnp.dot(att[h].astype(BF16), v[:, cols], preferred_element_type=F32) + o_inter[:, cols]
        s_new = s_new + jnp.where(row_head == h, contrib[:, cols], 0.0)
    s_scr[...] = s_new


def _gla(gla_arr, gz_arr, w2p, bias, e2, m2, s0, B, T):
    nch = T // GLA_CHUNK
    nseq = 2 if B % 2 == 0 else 1
    chunk = lambda d, i: jnp.where(d == 0, i, nch - 1 - i)
    st = lambda b, d, i: (b, d, 0, 0)
    o, s_fin = pl.pallas_call(
        functools.partial(_gla_kernel, nch=nch),
        grid=(B // nseq, 2, nch),
        in_specs=[
            pl.BlockSpec((nseq, GLA_CHUNK, 2 * GLA_QK + GLA_V), lambda b, d, i: (b, chunk(d, i), 0)),
            pl.BlockSpec((nseq, GLA_CHUNK, GZ_W), lambda b, d, i: (b, chunk(d, i), 0)),
            pl.BlockSpec((1, GZ_W, GLA_QK), lambda b, d, i: (d, 0, 0)),
            pl.BlockSpec((1, 1, GLA_QK), lambda b, d, i: (d, 0, 0)),
            pl.BlockSpec((1, (GLA_LEVELS + 2) * GLA_CHUNK, GLA_CHUNK), lambda b, d, i: (d, 0, 0)),
            pl.BlockSpec((1, GLA_LEVELS + 1, GLA_CHUNK, GLA_CHUNK), lambda b, d, i: (d, 0, 0, 0)),
            pl.BlockSpec((nseq, 1, GLA_QK, GLA_DV), st),
        ],
        out_specs=[
            pl.BlockSpec((1, nseq, GLA_CHUNK, GLA_V), lambda b, d, i: (d, b, chunk(d, i), 0)),
            pl.BlockSpec((nseq, 1, GLA_QK, GLA_DV), st),
        ],
        out_shape=[
            jax.ShapeDtypeStruct((2, B, T, GLA_V), F32),
            jax.ShapeDtypeStruct((B, 2, GLA_QK, GLA_DV), F32),
        ],
        scratch_shapes=[pltpu.VMEM((nseq, GLA_QK, GLA_DV), F32)],
        compiler_params=_cparams("parallel", "parallel", "arbitrary"),
        name="gla_scan",
    )(gla_arr.reshape(B, T, GLA_W), gz_arr.reshape(B, T, GZ_W), w2p, bias, e2, m2, s0)
    return o, s_fin


def _lru_kernel(x_ref, pv_ref, nx_ref, cw_ref, cb_ref, wg_ref, bg_ref, lam_ref, h0_ref, o_ref, hfin_ref, carry, a_s, u_s,
                h_s, *, nblk):
    d = pl.program_id(1)
    i = pl.program_id(2)
    T = LRU_CHUNK
    W = LRU_WIDTH

    @pl.when(i == 0)
    def _():
        carry[...] = jnp.broadcast_to(h0_ref[0, 0], (SUBLANES, W))

    li = jnp.where(d == 0, i, nblk - 1 - i)
    pv = jnp.where(li > 0, pv_ref[...], 0.0)
    nx = jnp.where(li < nblk - 1, nx_ref[...], 0.0)
    xe = jnp.concatenate([pv, x_ref[...], nx], axis=0)
    n_ext = T + 2 * SUBLANES
    win = lambda off: pltpu.roll(xe, n_ext - off, 0)[0:T]
    cw = cw_ref[...]
    xc = cb_ref[...] + win(6) * cw[0:1] + win(7) * cw[1:2] + xe[SUBLANES:SUBLANES + T] * cw[2:3] + win(9) * cw[3:4]

    g = jnp.dot(xc.astype(BF16), wg_ref[0], preferred_element_type=F32) + bg_ref[0]
    r = _sigmoid(g[:, :W])
    gi = _sigmoid(g[:, W:])
    log_a = (-LRU_C * _softplus(-lam_ref[0])) * r
    a = jnp.exp(log_a)
    u = jnp.sqrt(-jnp.tanh(log_a) * (a * a + 1.0)) * (gi * xc)
    panels = W // LANES
    for c in range(panels):
        a_s[c] = a[:, c * LANES:(c + 1) * LANES]
        u_s[c] = u[:, c * LANES:(c + 1) * LANES]

    groups = T // SUBLANES
    slab = lambda ref, j: jnp.concatenate([ref[c, pl.ds(j, groups, stride=SUBLANES), :] for c in range(panels)], axis=1)
    g_row = lax.broadcasted_iota(I32, (groups, W), 0)

    def scan(order, shift, edge):
        us = {order[0]: slab(u_s, order[0])}
        ps = {order[0]: slab(a_s, order[0])}
        for prev, j in zip(order[:-1], order[1:]):
            aj = slab(a_s, j)
            us[j] = slab(u_s, j) + aj * us[prev]
            ps[j] = aj * ps[prev]
        cu, cp = us[order[-1]], ps[order[-1]]
        sh = 1
        while sh < groups:
            ok = (g_row >= sh) if shift == 1 else (g_row < groups - sh)
            amt = sh if shift == 1 else groups - sh
            cu = cu + cp * jnp.where(ok, pltpu.roll(cu, amt, 0), 0.0)
            cp = cp * jnp.where(ok, pltpu.roll(cp, amt, 0), 1.0)
            sh *= 2
        c_out = cu + cp * carry[0:1]
        c_in = jnp.where(g_row == edge, carry[0:1], pltpu.roll(c_out, shift if shift == 1 else groups - 1, 0))
        for j in order:
            hj = us[j] + ps[j] * c_in
            for c in range(panels):
                h_s[c, pl.ds(j, groups, stride=SUBLANES), :] = hj[:, c * LANES:(c + 1) * LANES]
        o_ref[0] = jnp.concatenate([h_s[c] for c in range(panels)], axis=1)
        last = groups - 1 - edge
        carry[...] = jnp.broadcast_to(c_out[last:last + 1], (SUBLANES, W))

    @pl.when(d == 0)
    def _():
        scan(list(range(SUBLANES)), 1, 0)

    @pl.when(d == 1)
    def _():
        scan(list(reversed(range(SUBLANES))), -1, groups - 1)

    @pl.when(i == nblk - 1)
    def _():
        hfin_ref[0, 0] = carry[0:1]


def _lru(lru_arr, cw, cb, wg, bg, lam, h0, B, T):
    nblk = T // LRU_CHUNK
    per8 = LRU_CHUNK // SUBLANES
    n8 = B * T // SUBLANES
    blk = lambda b, d, i: b * nblk + jnp.where(d == 0, i, nblk - 1 - i)
    dirw = lambda b, d, i: (d, 0, 0)
    st = lambda b, d, i: (b, d, 0, 0)
    return pl.pallas_call(
        functools.partial(_lru_kernel, nblk=nblk),
        grid=(B, 2, nblk),
        in_specs=[
            pl.BlockSpec((LRU_CHUNK, LRU_WIDTH), lambda b, d, i: (blk(b, d, i), 0)),
            pl.BlockSpec((SUBLANES, LRU_WIDTH), lambda b, d, i: (jnp.maximum(blk(b, d, i) * per8 - 1, 0), 0)),
            pl.BlockSpec((SUBLANES, LRU_WIDTH), lambda b, d, i: (jnp.minimum((blk(b, d, i) + 1) * per8, n8 - 1), 0)),
            pl.BlockSpec((SUBLANES, LRU_WIDTH), lambda b, d, i: (0, 0)),
            pl.BlockSpec((1, LRU_WIDTH), lambda b, d, i: (0, 0)),
            pl.BlockSpec((1, LRU_WIDTH, 2 * LRU_WIDTH), dirw),
            pl.BlockSpec((1, 1, 2 * LRU_WIDTH), dirw),
            pl.BlockSpec((1, 1, LRU_WIDTH), dirw),
            pl.BlockSpec((1, 1, 1, LRU_WIDTH), st),
        ],
        out_specs=[
            pl.BlockSpec((1, LRU_CHUNK, LRU_WIDTH), lambda b, d, i: (d, blk(b, d, i), 0)),
            pl.BlockSpec((1, 1, 1, LRU_WIDTH), st),
        ],
        out_shape=[
            jax.ShapeDtypeStruct((2, B * T, LRU_WIDTH), F32),
            jax.ShapeDtypeStruct((B, 2, 1, LRU_WIDTH), F32),
        ],
        scratch_shapes=[pltpu.VMEM((SUBLANES, LRU_WIDTH), F32)]
        + [pltpu.VMEM((LRU_WIDTH // LANES, LRU_CHUNK, LANES), F32) for _ in range(3)],
        compiler_params=_cparams("parallel", "parallel", "arbitrary"),
        name="lru_scan",
    )(lru_arr, lru_arr, lru_arr, cw, cb, wg, bg, lam, h0)


def _out_kernel(h_ref, attn_ref, go_ref, gr_ref, gg_ref, lh_ref, lg_ref, w_ref, g1_ref, a2_ref, s2_ref, wr_ref, br_ref,
                vt_ref, lt_ref, hn_ref, v_ref, lgt_ref, *, n_main):
    i = pl.program_id(0)

    @pl.when(i < n_main)
    def _():
        _out_body(h_ref, attn_ref, go_ref, gr_ref, gg_ref, lh_ref, lg_ref, w_ref, g1_ref, a2_ref, s2_ref, wr_ref, br_ref,
                  hn_ref, v_ref, lgt_ref)

    @pl.when(i >= n_main)
    def _():
        v_ref[...] = vt_ref[...]
        lgt_ref[...] = lt_ref[...]


def _out_body(h_ref, attn_ref, go_ref, gr_ref, gg_ref, lh_ref, lg_ref, w_ref, g1_ref, a2_ref, s2_ref, wr_ref, br_ref,
              hn_ref, v_ref, lgt_ref):
    o = go_ref[0, 0] + go_ref[1, 0]
    gr = gr_ref[...]
    gate = gr * _sigmoid(gr)
    parts = []
    for hh in range(GLA_HEADS):
        oh = o[:, hh * GLA_DV:(hh + 1) * GLA_DV]
        y = (oh * lax.rsqrt(jnp.mean(oh * oh, axis=-1, keepdims=True) + RMS_EPS)) * gg_ref[...]
        parts.append(y * gate[:, hh * GLA_DV:(hh + 1) * GLA_DV])
    gla = jnp.concatenate(parts, axis=1).astype(BF16)
    lg = lg_ref[...]
    gelu = lg * (0.5 * (1.0 + jnp.tanh(np.sqrt(2.0 / np.pi).astype(np.float32) * (lg + 0.044715 * (lg * lg * lg)))))
    lru = ((lh_ref[0] + lh_ref[1]) * gelu).astype(BF16)
    y = jnp.dot(attn_ref[...], w_ref[0:ATTN_Q], preferred_element_type=F32)
    y = y + jnp.dot(gla, w_ref[ATTN_Q:ATTN_Q + GLA_V], preferred_element_type=F32)
    y = y + jnp.dot(lru, w_ref[ATTN_Q + GLA_V:D_MIX], preferred_element_type=F32)
    hn = h_ref[...] + g1_ref[0] * y
    hn_ref[...] = hn
    v = (hn * lax.rsqrt(jnp.mean(hn * hn, axis=-1, keepdims=True) + RMS_EPS)) * a2_ref[0] + s2_ref[0]
    _store_token_tiles(v_ref, _pack_bf16_pairs(v))
    v_hi = v.astype(BF16)
    v_lo = (v - v_hi.astype(F32)).astype(BF16)
    t = jnp.dot(v_hi, wr_ref[...], preferred_element_type=F32)
    t_lo = jnp.dot(v_lo, wr_ref[:, 0:LANES], preferred_element_type=F32)
    lgt_ref[...] = (t[:, :LANES] + t[:, LANES:]) + t_lo + br_ref[...]


def _mixer_out(h, attn, go, gla_arr, gg, lh, lru_arr, w_out, g1, a2, s2, wr, br, seq_len, tail):
    rows, D = h.shape
    tm = _tile(seq_len, 256)
    per = seq_len // tm
    n_main = rows // tm
    tw = tm * (D // 2 // LANES)
    if tail is None:
        tail = (jnp.zeros((tw, LANES), I32), jnp.zeros((tm, LANES), F32))
        n_tail = 0
    else:
        assert tail[1].shape[0] % tm == 0
        n_tail = tail[1].shape[0] // tm
    n_tok = rows + n_tail * tm
    main = lambda i: jnp.minimum(i, n_main - 1)
    row = lambda i: (main(i), 0)
    row3 = lambda i: (0, main(i), 0)
    bat = lambda i: (main(i) // per, 0, 0)
    const = lambda i: (0, 0)
    tok = lambda i: (i, 0)
    trow = lambda i: (jnp.maximum(i - n_main, 0), 0)
    in_specs = [
        pl.BlockSpec((tm, D), row),
        pl.BlockSpec((tm, ATTN_Q), row),
        pl.BlockSpec((2, 1, tm, GLA_V), lambda i: (0, main(i) // per, main(i) % per, 0)),
        pl.BlockSpec((tm, GLA_V), lambda i: (main(i), 2)),
        pl.BlockSpec((1, GLA_DV), const),
        pl.BlockSpec((2, tm, LRU_WIDTH), row3),
        pl.BlockSpec((tm, LRU_WIDTH), lambda i: (main(i), 1)),
        pl.BlockSpec((D_MIX, D), const, pipeline_mode=pl.Buffered(1)),
        pl.BlockSpec((1, 1, D), bat),
        pl.BlockSpec((1, 1, D), bat),
        pl.BlockSpec((1, 1, D), bat),
        pl.BlockSpec((D, 2 * LANES), const),
        pl.BlockSpec((1, LANES), const),
        pl.BlockSpec((tw, LANES), trow),
        pl.BlockSpec((tm, LANES), trow),
    ]
    args = [h, attn, go, gla_arr, gg, lh, lru_arr, w_out, g1, a2, s2, wr, br, tail[0], tail[1]]
    return pl.pallas_call(
        functools.partial(_out_kernel, n_main=n_main),
        grid=(n_main + n_tail,),
        in_specs=in_specs,
        out_specs=[
            pl.BlockSpec((tm, D), row),
            pl.BlockSpec((tw, LANES), tok),
            pl.BlockSpec((tm, LANES), tok),
        ],
        out_shape=[
            jax.ShapeDtypeStruct((rows, D), F32),
            jax.ShapeDtypeStruct((n_tok * (tw // tm), LANES), I32),
            jax.ShapeDtypeStruct((n_tok, LANES), F32),
        ],
        compiler_params=_cparams("arbitrary"),
        name="mixer_out",
    )(*args)


def _route_kernel(lg_ref, tri_ref, oi_ref, of_ref, cnt_ref, carry):
    i = pl.program_id(0)

    @pl.when(i == 0)
    def _():
        carry[...] = jnp.zeros_like(carry)

    lg = lg_ref[...]
    col = lax.broadcasted_iota(I32, lg.shape, 1)
    colf = col.astype(F32)
    big = float(LANES)
    is_g = col < N_GROUPS
    gm = jnp.max(jnp.where(is_g, lg, -jnp.inf), axis=-1, keepdims=True)
    eg = jnp.where(is_g, jnp.exp(lg - gm), 0.0)
    pg = eg / jnp.sum(eg, axis=-1, keepdims=True)
    p_grp = jnp.max(pg, axis=-1, keepdims=True)
    grp = jnp.min(jnp.where(is_g & (pg == p_grp), colf, big), axis=-1, keepdims=True).astype(I32)

    sel = (col >= N_GROUPS) & (col < N_GROUPS + N_EXPERTS) & (((col - N_GROUPS) >> 3) == grp)
    em = jnp.max(jnp.where(sel, lg, -jnp.inf), axis=-1, keepdims=True)
    ee = jnp.where(sel, jnp.exp(lg - em), 0.0)
    pe = ee / jnp.sum(ee, axis=-1, keepdims=True)
    p1 = jnp.max(jnp.where(sel, pe, -1.0), axis=-1, keepdims=True)
    c1 = jnp.min(jnp.where(sel & (pe == p1), colf, big), axis=-1, keepdims=True).astype(I32)
    rest = sel & (col != c1)
    p2 = jnp.max(jnp.where(rest, pe, -1.0), axis=-1, keepdims=True)
    c2 = jnp.min(jnp.where(rest & (pe == p2), colf, big), axis=-1, keepdims=True).astype(I32)
    e1 = c1 - N_GROUPS
    e2 = c2 - N_GROUPS
    den = p1 + p2
    g1 = p_grp * (p1 / den)
    g2 = p_grp * (p2 / den)

    hit1 = col == e1
    hit2 = col == e2
    oh = jnp.where(hit1 | hit2, 1.0, 0.0)
    before = jnp.dot(tri_ref[...], oh.astype(BF16), preferred_element_type=F32) + carry[0:1]
    r1 = jnp.sum(jnp.where(hit1, before, 0.0), axis=-1, keepdims=True).astype(I32)
    r2 = jnp.sum(jnp.where(hit2, before, 0.0), axis=-1, keepdims=True).astype(I32)
    new = carry[0:1] + jnp.sum(oh, axis=0, keepdims=True)
    carry[...] = jnp.broadcast_to(new, carry.shape)
    cnt_ref[...] = jnp.broadcast_to(new, cnt_ref.shape)

    zero = jnp.zeros_like(col)
    oi_ref[...] = jnp.where(col == 0, e1, jnp.where(col == 1, e2, jnp.where(col == 2, r1, jnp.where(col == 3, r2, zero))))
    of_ref[...] = jnp.where(col == 0, g1, jnp.where(col == 1, g2, 0.0))


def _route(logits):
    n_tok = logits.shape[0]
    tm = _tile(n_tok, 1024)
    row = lambda i: (i, 0)
    tri = jnp.tril(jnp.ones((tm, tm), BF16), -1)
    return pl.pallas_call(
        _route_kernel,
        grid=(n_tok // tm,),
        in_specs=[pl.BlockSpec((tm, LANES), row), pl.BlockSpec((tm, tm), lambda i: (0, 0))],
        out_specs=[
            pl.BlockSpec((tm, LANES), row),
            pl.BlockSpec((tm, LANES), row),
            pl.BlockSpec((SUBLANES, LANES), lambda i: (0, 0)),
        ],
        out_shape=[
            jax.ShapeDtypeStruct((n_tok, LANES), I32),
            jax.ShapeDtypeStruct((n_tok, LANES), F32),
            jax.ShapeDtypeStruct((SUBLANES, LANES), F32),
        ],
        scratch_shapes=[pltpu.VMEM((SUBLANES, LANES), F32)],
        compiler_params=_cparams("arbitrary"),
        name="route",
    )(logits, tri)


def _dot_casting(x, stage, dst, rows, cols):
    n = rows.stop - rows.start
    step = _tile(n, 256)
    acc = None
    for c in range(n // step):
        r = slice(rows.start + c * step, rows.start + (c + 1) * step)
        w = stage[r, cols].astype(BF16)
        dst[r, cols] = w
        part = jnp.dot(x[:, c * step:(c + 1) * step], w, preferred_element_type=F32)
        acc = part if acc is None else acc + part
    return acc


def _moe_ffn_kernel(be_ref, nu_ref, nx_ref, src_ref, srcn_ref, v_hbm, w1_hbm, w3_hbm, w2_hbm, o_ref,
                    xbuf, st1, st3, st2, w1s, w3s, w2s, sem_x, sem_w, *, layer, sub):
    i = pl.program_id(0)
    nu = nu_ref[0]
    rows = ROW_BLOCK * sub

    def gather(idx_ref, slot):
        for r in range(ROW_BLOCK):
            s = pl.multiple_of(idx_ref[0, 0, r] * sub, sub)
            pltpu.make_async_copy(v_hbm.at[pl.ds(s, sub)], xbuf.at[slot, pl.ds(r * sub, sub)], sem_x.at[slot]).start()

    def wait_rows(slot):
        pltpu.make_async_copy(v_hbm.at[pl.ds(0, rows)], xbuf.at[slot], sem_x.at[slot]).wait()

    def weight_copies(e):
        return (pltpu.make_async_copy(w1_hbm.at[layer, e], st1, sem_w.at[0]),
                pltpu.make_async_copy(w3_hbm.at[layer, e], st3, sem_w.at[1]),
                pltpu.make_async_copy(w2_hbm.at[layer, e], st2, sem_w.at[2]))

    @pl.when(i < nu)
    def _():
        e = be_ref[i]
        slot = i & 1

        @pl.when(i == 0)
        def _():
            for cp in weight_copies(e):
                cp.start(priority=1)
            gather(src_ref, 0)

        wait_rows(slot)
        gather(srcn_ref, 1 - slot)
        xb = _unpack_bf16_pairs(_load_token_tiles(xbuf.at[slot], ROW_BLOCK)).astype(BF16)
        first = (i == 0) | (e != be_ref[jnp.maximum(i - 1, 0)])

        d_model, d_ff = w1s.shape
        n_ff = 2 if d_ff % (2 * LANES) == 0 else 1
        ffs = [slice(c * d_ff // n_ff, (c + 1) * d_ff // n_ff) for c in range(n_ff)]
        every = slice(0, d_model)

        def mlp(up1, up3, down):
            y = None
            for ff in ffs:
                a = up1(xb, ff)
                b = up3(xb, ff)
                part = down(((a * _sigmoid(a)) * b).astype(BF16), ff)
                y = part if y is None else y + part
            _store_token_tiles(o_ref, _pack_bf16_pairs(y))

        @pl.when(first)
        def _():
            for cp in weight_copies(e):
                cp.wait()
            mlp(lambda x, ff: _dot_casting(x, st1, w1s, every, ff), lambda x, ff: _dot_casting(x, st3, w3s, every, ff),
                lambda h, ff: _dot_casting(h, st2, w2s, ff, every))

            @pl.when(nx_ref[i] >= 0)
            def _():
                for cp in weight_copies(nx_ref[i]):
                    cp.start(priority=1)

        @pl.when(jnp.logical_not(first))
        def _():
            mlp(lambda x, ff: jnp.dot(x, w1s[:, ff], preferred_element_type=F32),
                lambda x, ff: jnp.dot(x, w3s[:, ff], preferred_element_type=F32),
                lambda h, ff: jnp.dot(h, w2s[ff, :], preferred_element_type=F32))

        @pl.when(i == nu - 1)
        def _():
            wait_rows(1 - slot)

    @pl.when(i >= nu)
    def _():
        o_ref[...] = jnp.zeros_like(o_ref)


def _moe_experts(v_tiles, src, blk_e, n_used, nxt_e, w1, w3, w2, layer):
    D, FF = w1.shape[-2:]
    sub = D // 2 // LANES
    P = src.shape[0]
    nb = P // ROW_BLOCK
    tw = ROW_BLOCK * sub
    cur = lambda i, be, nu, nx: (jnp.minimum(i, nu[0] - 1), 0, 0)
    nxt = lambda i, be, nu, nx: (jnp.minimum(i + 1, nu[0] - 1), 0, 0)
    src3 = src.reshape(nb, 1, ROW_BLOCK)
    hbm = pl.BlockSpec(memory_space=pl.ANY)
    return pl.pallas_call(
        functools.partial(_moe_ffn_kernel, layer=layer, sub=sub),
        grid_spec=pltpu.PrefetchScalarGridSpec(
            num_scalar_prefetch=3,
            grid=(nb,),
            in_specs=[
                pl.BlockSpec((1, 1, ROW_BLOCK), cur, memory_space=pltpu.SMEM),
                pl.BlockSpec((1, 1, ROW_BLOCK), nxt, memory_space=pltpu.SMEM),
                hbm, hbm, hbm, hbm,
            ],
            out_specs=pl.BlockSpec((tw, LANES), lambda i, be, nu, nx: (i, 0)),
            scratch_shapes=[
                pltpu.VMEM((2, tw, LANES), I32),
                pltpu.VMEM((D, FF), F32), pltpu.VMEM((D, FF), F32), pltpu.VMEM((FF, D), F32),
                pltpu.VMEM((D, FF), BF16), pltpu.VMEM((D, FF), BF16), pltpu.VMEM((FF, D), BF16),
                pltpu.SemaphoreType.DMA((2,)), pltpu.SemaphoreType.DMA((3,)),
            ],
        ),
        out_shape=jax.ShapeDtypeStruct((P * sub, LANES), I32),
        compiler_params=_cparams("arbitrary"),
        name="moe_ffn",
    )(blk_e, n_used, nxt_e, src3, src3, v_tiles, w1, w3, w2)


def _combine_kernel(dest_ref, destn_ref, h_ref, gate_ref, g2_ref, fg_ref, ys_ref, o_ref, buf, sem, *, final, nsteps):
    i = pl.program_id(0)
    slot = i & 1
    tm = h_ref.shape[0]
    rows = buf.shape[2]
    sub = rows // tm

    def gather(idx_ref, sl):
        for j in range(tm * TOP_K):
            src = pl.multiple_of(idx_ref[0, 0, j] * sub, sub)
            cp = pltpu.make_async_copy(ys_ref.at[pl.ds(src, sub)], buf.at[sl, j % TOP_K, pl.ds((j // TOP_K) * sub, sub)],
                                       sem.at[sl])
            cp.start(priority=j % 2)

    def wait_rows(sl):
        for kk in range(TOP_K):
            pltpu.make_async_copy(ys_ref.at[pl.ds(0, rows)], buf.at[sl, kk], sem.at[sl]).wait()

    @pl.when(i == 0)
    def _():
        gather(dest_ref, 0)

    gather(destn_ref, 1 - slot)
    wait_rows(slot)
    gate = gate_ref[...]
    f = (_unpack_bf16_pairs(_load_token_tiles(buf.at[slot, 0], tm)) * gate[:, 0:1]
         + _unpack_bf16_pairs(_load_token_tiles(buf.at[slot, 1], tm)) * gate[:, 1:2])

    @pl.when(i == nsteps - 1)
    def _():
        wait_rows(1 - slot)

    hn = h_ref[...] + g2_ref[0] * f
    if final:
        hn = (hn * lax.rsqrt(jnp.mean(hn * hn, axis=-1, keepdims=True) + RMS_EPS)) * fg_ref[...]
    o_ref[...] = hn


def _combine(h, dest, gates, g2, fg, ys, seq_len, tok_off, final):
    rows, D = h.shape
    tm = _tile(seq_len, 256)
    per = seq_len // tm
    off = tok_off // tm
    n_blk_all = dest.shape[0] // (tm * TOP_K)
    nsteps = rows // tm
    dest3 = dest.reshape(n_blk_all, 1, tm * TOP_K)
    return pl.pallas_call(
        functools.partial(_combine_kernel, final=final, nsteps=nsteps),
        grid=(nsteps,),
        in_specs=[
            pl.BlockSpec((1, 1, tm * TOP_K), lambda i: (i + off, 0, 0), memory_space=pltpu.SMEM),
            pl.BlockSpec((1, 1, tm * TOP_K), lambda i: (jnp.minimum(i + 1, nsteps - 1) + off, 0, 0), memory_space=pltpu.SMEM),
            pl.BlockSpec((tm, D), lambda i: (i, 0)),
            pl.BlockSpec((tm, LANES), lambda i: (i + off, 0)),
            pl.BlockSpec((1, 1, D), lambda i: (i // per, 0, 0)),
            pl.BlockSpec((1, D), lambda i: (0, 0)),
            pl.BlockSpec(memory_space=pl.ANY),
        ],
        out_specs=pl.BlockSpec((tm, D), lambda i: (i, 0)),
        out_shape=jax.ShapeDtypeStruct((rows, D), F32),
        scratch_shapes=[pltpu.VMEM((2, TOP_K, tm * (D // 2 // LANES), LANES), I32), pltpu.SemaphoreType.DMA((2,))],
        compiler_params=_cparams("arbitrary"),
        name="combine",
    )(dest3, dest3, h, gates, g2, fg, ys)


def _rope_tables(S):
    rows = S // GRID_W
    row = jnp.repeat(jnp.arange(rows, dtype=F32), GRID_W)
    col = jnp.tile(jnp.arange(GRID_W, dtype=F32), rows)
    n_freq = HEAD_DIM // 4
    inv = ROPE_THETA ** (-jnp.arange(n_freq, dtype=F32) / n_freq)
    ang = jnp.concatenate([row[:, None] * inv, col[:, None] * inv], axis=-1)
    cos = jnp.repeat(jnp.cos(ang), 2, axis=1)
    sin = jnp.repeat(jnp.sin(ang), 2, axis=1)
    even = (jnp.arange(HEAD_DIM) % 2) == 0
    return cos, jnp.where(even, -sin, 0.0), jnp.where(even, 0.0, sin)


def _block_diag(w):
    n, c, _ = w.shape
    eye = jnp.eye(n, dtype=w.dtype)
    return (eye[:, None, :, None] * w[:, :, None, :]).reshape(n * c, n * c)


def _slot_plan(ids, counts_row, n_tok):
    e = ids[:, 0:TOP_K]
    rank = ids[:, TOP_K:2 * TOP_K]
    counts = counts_row[:N_EXPERTS].astype(I32)
    padded = (counts + ROW_BLOCK - 1) // ROW_BLOCK * ROW_BLOCK
    pad_end = jnp.cumsum(padded)
    pad_start = pad_end - padded
    dest = (pad_start[e] + rank).reshape(n_tok * TOP_K)
    n_blocks = -(-(n_tok * TOP_K + N_EXPERTS * (ROW_BLOCK - 1)) // ROW_BLOCK)
    starts = jnp.arange(n_blocks, dtype=I32) * ROW_BLOCK
    blk_e = jnp.minimum(jnp.sum(pad_end[None, :] <= starts[:, None], axis=1), N_EXPERTS - 1).astype(I32)
    n_used = (pad_end[-1] // ROW_BLOCK).astype(I32).reshape(1)
    run_next = pad_end[blk_e] // ROW_BLOCK
    nxt_e = jnp.where(run_next < n_used[0], blk_e[jnp.minimum(run_next, n_blocks - 1)], -1).astype(I32)
    src = jnp.zeros((n_blocks * ROW_BLOCK,), I32).at[dest].set(jnp.arange(n_tok * TOP_K, dtype=I32) // TOP_K,
                                                                 unique_indices=True)
    return dest, src, blk_e, n_used, nxt_e


def kernel(x, c, ctx, c_ctx, ada_w, ada_b, norm_mix_g, norm_ffn_g, w_in, attn_sink, gla_gate_w2, gla_gate_b, gla_norm_g, lru_conv_w, lru_conv_b, lru_wa, lru_ba, lru_wx, lru_bx, lru_lambda, w_out, router_g_w, router_g_b, router_e_w, router_e_b, moe_w1, moe_w3, moe_w2, final_norm_g):
    B, S, D = x.shape
    C = ctx.shape[1]
    L = ada_w.shape[0]
    assert S % max(ATTN_BLOCK, GLA_CHUNK, LRU_CHUNK, GRID_W) == 0 and C % max(ATTN_BLOCK, GLA_CHUNK, LRU_CHUNK) == 0
    assert B + 1 <= SUBLANES and D % (2 * LANES) == 0

    cond = jnp.concatenate([c, c_ctx[None], jnp.zeros((SUBLANES - B - 1, D), F32)], axis=0)
    mods = _ada_mods(cond, ada_w, ada_b).reshape(L, SUBLANES, 6, D)

    cos_l, se_l, so_l = _rope_tables(S)
    cos_c = jnp.ones((C, HEAD_DIM), F32)
    zero_c = jnp.zeros((C, HEAD_DIM), F32)
    e2_np, m2_np = _gla_constants()
    gla_e = jnp.asarray(e2_np, BF16)
    gla_m = jnp.asarray(m2_np, F32)

    h_lat = x.reshape(B * S, D)
    h_ctx = ctx.reshape(B * C, D)
    out = None
    for l in range(L):
        last = l == L - 1
        ml = mods[l, :B]
        mc = jnp.broadcast_to(mods[l, B][None], (B, 6, D))
        per_b = lambda m, j: m[:, j][:, None, :]
        a1_l, a1_c = [(1.0 + per_b(m, 1)) * norm_mix_g[l] for m in (ml, mc)]
        a2_l, a2_c = [(1.0 + per_b(m, 4)) * norm_ffn_g[l] for m in (ml, mc)]

        w = w_in[l]
        c_gz = ATTN_Q + 2 * ATTN_KV + 2 * GLA_QK + 2 * GLA_V
        w_packed = jnp.concatenate(
            [w[:, :c_gz], w[:, c_gz:c_gz + 2 * GLA_RANK], jnp.zeros((D, GZ_W - 2 * GLA_RANK), F32), w[:, c_gz + 2 * GLA_RANK:]],
            axis=1).astype(BF16)
        qkv_l, gla_l, gz_l, lru_l = _in_proj(h_lat, a1_l, per_b(ml, 0), cos_l, se_l, so_l, w_packed, S)
        qkv_c, gla_c, gz_c, lru_c = _in_proj(h_ctx, a1_c, per_b(mc, 0), cos_c, zero_c, zero_c, w_packed, C)

        sink_b = jnp.broadcast_to(attn_sink[l][:, None], (N_Q_HEADS, LANES)).astype(F32)
        attn_l = _attn_latent(qkv_l, qkv_c, sink_b, B, S, C)

        w2p = jnp.zeros((2, GZ_W, GLA_QK), F32)
        w2p = w2p.at[0, :GLA_RANK].set(gla_gate_w2[l, 0]).at[1, GLA_RANK:2 * GLA_RANK].set(gla_gate_w2[l, 1]).astype(BF16)
        gbias = gla_gate_b[l].reshape(2, 1, GLA_QK)
        s_zero = jnp.zeros((B, 2, GLA_QK, GLA_DV), F32)
        go_c, s_ctx = _gla(gla_c, gz_c, w2p, gbias, gla_e, gla_m, s_zero, B, C)
        go_l, _ = _gla(gla_l, gz_l, w2p, gbias, gla_e, gla_m, s_ctx, B, S)

        cw = jnp.concatenate([lru_conv_w[l], jnp.zeros((SUBLANES - CONV_W, LRU_WIDTH), F32)], axis=0)
        cb = lru_conv_b[l].reshape(1, LRU_WIDTH)
        wg = jnp.stack([jnp.concatenate([_block_diag(lru_wa[l, d]), _block_diag(lru_wx[l, d])], axis=1) for d in range(2)]).astype(BF16)
        bg = jnp.concatenate([lru_ba[l], lru_bx[l]], axis=1).reshape(2, 1, 2 * LRU_WIDTH)
        lam = lru_lambda[l].reshape(2, 1, LRU_WIDTH)
        h_zero = jnp.zeros((B, 2, 1, LRU_WIDTH), F32)
        lh_c, hs_ctx = _lru(lru_c, cw, cb, wg, bg, lam, h_zero, B, C)
        lh_l, _ = _lru(lru_l, cw, cb, wg, bg, lam, hs_ctx, B, S)

        wo = w_out[l].astype(BF16)
        gg = gla_norm_g[l].reshape(1, GLA_DV)
        wr = jnp.concatenate([router_g_w[l], router_e_w[l], jnp.zeros((D, LANES - N_GROUPS - N_EXPERTS), F32)], axis=1)
        wr_hi = wr.astype(BF16)
        wr = jnp.concatenate([wr_hi, (wr - wr_hi.astype(F32)).astype(BF16)], axis=1)
        br =jnp.concatenate([router_g_b[l], router_e_b[l], jnp.zeros((LANES - N_GROUPS - N_EXPERTS,), F32)]).reshape(1, LANES)
        n_tok = B * S if last else B * (S + C)
        tail = None
        if not last:
            attn_c = _attn_context(qkv_c, sink_b, B, C)
            h_ctx, v_c, lgt_c = _mixer_out(h_ctx, attn_c, go_c, gla_c, gg, lh_c, lru_c, wo, per_b(mc, 2), a2_c, per_b(mc, 3),
                                           wr, br, C, None)
            tail = (v_c, lgt_c)
        h_lat, v_all, lgt_all = _mixer_out(h_lat, attn_l, go_l, gla_l, gg, lh_l, lru_l, wo, per_b(ml, 2), a2_l, per_b(ml, 3),
                                           wr, br, S, tail)

        ids, gates, counts = _route(lgt_all)
        dest, src, blk_e, n_used, nxt_e = _slot_plan(ids, counts[0], n_tok)
        ys = _moe_experts(v_all, src, blk_e, n_used, nxt_e, moe_w1, moe_w3, moe_w2, l)
        fg = final_norm_g.reshape(1, D)
        h_lat = _combine(h_lat, dest, gates, per_b(ml, 5), fg, ys, S, 0, last)
        if not last:
            h_ctx = _combine(h_ctx, dest, gates, per_b(mc, 5), fg, ys, C, B * S, False)
        out = h_lat
    return out.reshape(B, S, D)
```

```python
import functools

import numpy as np
import jax
import jax.numpy as jnp
from jax import lax
from jax.experimental import pallas as pl
from jax.experimental.pallas import tpu as pltpu

F32 = jnp.float32
BF16 = jnp.bfloat16
I32 = jnp.int32

GRID_W = 64
RMS_EPS = 1e-6
N_Q_HEADS = 8
N_KV_HEADS = 2
HEAD_DIM = 128
WINDOW = 128
ATTN_BLOCK = 128
ROPE_THETA = 10000.0
GLA_HEADS = 4
GLA_DK = 64
GLA_DV = 128
GLA_RANK = 16
GLA_TAU = 16.0
LRU_WIDTH = 512
LRU_BLOCKS = 8
LRU_C = 8.0
CONV_W = 4
N_GROUPS = 4
EXPERTS_PER_GROUP = 8
N_EXPERTS = 32
TOP_K = 2
ATTN_Q = N_Q_HEADS * HEAD_DIM
ATTN_KV = N_KV_HEADS * HEAD_DIM
GLA_QK = GLA_HEADS * GLA_DK
GLA_V = GLA_HEADS * GLA_DV
D_MIX = ATTN_Q + GLA_V + LRU_WIDTH

LANES = 128
SUBLANES = 8
VMEM_LIMIT_BYTES = 56 * 1024 * 1024

QKV_W = ATTN_Q + 2 * ATTN_KV
GLA_W = 2 * GLA_QK + 2 * GLA_V
GZ_W = LANES
LRU_W = 2 * LRU_WIDTH
COL_GLA = QKV_W
COL_GZ = COL_GLA + GLA_W
COL_LRU = COL_GZ + GZ_W
W_IN_PACKED = COL_LRU + LRU_W

GLA_CHUNK = 128
GLA_LEVELS = 7
LRU_CHUNK = 256
ROW_BLOCK = 256


def _cparams(*sem):
    return pltpu.CompilerParams(dimension_semantics=sem, vmem_limit_bytes=VMEM_LIMIT_BYTES)


def _tile(n, pref):
    t = min(n, pref)
    while n % t:
        t -= SUBLANES
    return t


def _sigmoid(x):
    return 0.5 * (jnp.tanh(0.5 * x) + 1.0)


def _softplus(x):
    return jnp.maximum(x, 0.0) + jnp.log1p(jnp.exp(-jnp.abs(x)))


def _pack_bf16_pairs(x):
    n = x.shape[1] // 2
    lo = lax.bitcast_convert_type(x[:, :n].astype(BF16).astype(F32), I32)
    hi = lax.bitcast_convert_type(x[:, n:].astype(BF16).astype(F32), I32)
    return hi | lax.shift_right_logical(lo, jnp.full(lo.shape, 16, I32))


def _unpack_bf16_pairs(p):
    lo = lax.bitcast_convert_type(lax.shift_left(p, jnp.full(p.shape, 16, I32)), F32)
    hi = lax.bitcast_convert_type(p & jnp.int32(-65536), F32)
    return jnp.concatenate([lo, hi], axis=1)


def _store_token_tiles(ref, packed):
    m = packed.shape[0]
    sub = packed.shape[1] // LANES
    for s in range(sub):
        ref[pl.ds(s, m, stride=sub), :] = packed[:, s * LANES:(s + 1) * LANES]


def _load_token_tiles(ref, m):
    sub = ref.shape[0] // m
    return jnp.concatenate([ref[pl.ds(s, m, stride=sub), :] for s in range(sub)], axis=1)


def _ada_kernel(c_ref, w_ref, b_ref, o_ref):
    c = c_ref[...]
    s = c * _sigmoid(c)
    o_ref[0] = jnp.dot(s.astype(BF16), w_ref[0].astype(BF16), preferred_element_type=F32) + b_ref[0]


def _ada_mods(cond, ada_w, ada_b):
    L, D, D6 = ada_w.shape
    tn = _tile(D6, 1536)
    while tn % LANES:
        tn -= SUBLANES
    return pl.pallas_call(
        _ada_kernel,
        grid=(L, D6 // tn),
        in_specs=[
            pl.BlockSpec((SUBLANES, D), lambda l, j: (0, 0)),
            pl.BlockSpec((1, D, tn), lambda l, j: (l, 0, j)),
            pl.BlockSpec((1, 1, tn), lambda l, j: (l, 0, j)),
        ],
        out_specs=pl.BlockSpec((1, SUBLANES, tn), lambda l, j: (l, 0, j)),
        out_shape=jax.ShapeDtypeStruct((L, SUBLANES, D6), F32),
        compiler_params=_cparams("parallel", "parallel"),
        name="ada_mods",
    )(cond, ada_w, ada_b.reshape(L, 1, D6))


def _in_kernel(x_ref, a_ref, s_ref, cos_ref, se_ref, so_ref, w_ref, qkv_ref, gla_ref, gz_ref, lru_ref):
    x = x_ref[...]
    ms = jnp.mean(x * x, axis=-1, keepdims=True)
    u = (x * lax.rsqrt(ms + RMS_EPS)) * a_ref[0] + s_ref[0]
    ub = u.astype(BF16)
    cos, se, so = cos_ref[...], se_ref[...], so_ref[...]
    n_rot = N_Q_HEADS + N_KV_HEADS
    for jp in range(n_rot // 2):
        z2 = jnp.dot(ub, w_ref[:, 2 * jp * HEAD_DIM:(2 * jp + 2) * HEAD_DIM], preferred_element_type=F32)
        for j in (2 * jp, 2 * jp + 1):
            zh = z2[:, (j - 2 * jp) * HEAD_DIM:(j - 2 * jp + 1) * HEAD_DIM]
            rot = zh * cos + pltpu.roll(zh, HEAD_DIM - 1, 1) * se + pltpu.roll(zh, 1, 1) * so
            qkv_ref[:, j * HEAD_DIM:(j + 1) * HEAD_DIM] = rot.astype(BF16)
    c0 = n_rot * HEAD_DIM
    qkv_ref[:, c0:QKV_W] = jnp.dot(ub, w_ref[:, c0:QKV_W], preferred_element_type=F32).astype(BF16)
    gla_ref[...] = jnp.dot(ub, w_ref[:, COL_GLA:COL_GZ], preferred_element_type=F32)
    gz_ref[...] = jnp.dot(ub, w_ref[:, COL_GZ:COL_LRU], preferred_element_type=F32)
    lru_ref[...] = jnp.dot(ub, w_ref[:, COL_LRU:W_IN_PACKED], preferred_element_type=F32)


def _in_proj(h, a, s, cos, se, so, w, seq_len):
    rows, D = h.shape
    tm = _tile(seq_len, 512)
    per = seq_len // tm
    row = lambda i: (i, 0)
    bat = lambda i: (i // per, 0, 0)
    tab = lambda i: (i % per, 0)
    return pl.pallas_call(
        _in_kernel,
        grid=(rows // tm,),
        in_specs=[
            pl.BlockSpec((tm, D), row),
            pl.BlockSpec((1, 1, D), bat),
            pl.BlockSpec((1, 1, D), bat),
            pl.BlockSpec((tm, HEAD_DIM), tab),
            pl.BlockSpec((tm, HEAD_DIM), tab),
            pl.BlockSpec((tm, HEAD_DIM), tab),
            pl.BlockSpec((D, W_IN_PACKED), lambda i: (0, 0), pipeline_mode=pl.Buffered(1)),
        ],
        out_specs=[
            pl.BlockSpec((tm, QKV_W), row),
            pl.BlockSpec((tm, GLA_W), row),
            pl.BlockSpec((tm, GZ_W), row),
            pl.BlockSpec((tm, LRU_W), row),
        ],
        out_shape=[
            jax.ShapeDtypeStruct((rows, QKV_W), BF16),
            jax.ShapeDtypeStruct((rows, GLA_W), F32),
            jax.ShapeDtypeStruct((rows, GZ_W), F32),
            jax.ShapeDtypeStruct((rows, LRU_W), F32),
        ],
        compiler_params=_cparams("parallel"),
        name="in_proj",
    )(h, a, s, cos, se, so, w)


def _attn_heads(q, k_loc, v_loc, valid, k_ctx, v_ctx, sink_ref, o_ref):
    scale = HEAD_DIM ** -0.5
    group = N_Q_HEADS // N_KV_HEADS
    nq = q.shape[0]
    nt = (((1,), (1,)), ((), ()))
    if valid is not None:
        valid = jnp.concatenate([valid] * group, axis=0)
    for hk in range(N_KV_HEADS):
        heads = range(hk * group, (hk + 1) * group)
        qg = jnp.concatenate([q[:, h * HEAD_DIM:(h + 1) * HEAD_DIM] for h in heads], axis=0)
        sink = jnp.concatenate([jnp.broadcast_to(sink_ref[h:h + 1, 0:1], (nq, 1)) for h in heads], axis=0)
        s_ctx = lax.dot_general(qg, k_ctx[hk], nt, preferred_element_type=F32) * scale
        m = jnp.maximum(jnp.max(s_ctx, axis=-1, keepdims=True), sink)
        if k_loc is not None:
            s_loc = lax.dot_general(qg, k_loc[hk], nt, preferred_element_type=F32) * scale
            s_loc = jnp.where(valid, s_loc, -jnp.inf)
            m = jnp.maximum(m, jnp.max(s_loc, axis=-1, keepdims=True))
        ov = jnp.dot(jnp.exp(s_ctx - m).astype(BF16), _with_ones(v_ctx[hk]), preferred_element_type=F32)
        if k_loc is not None:
            ov = ov + jnp.dot(jnp.exp(s_loc - m).astype(BF16), _with_ones(v_loc[hk]), preferred_element_type=F32)
        den = ov[:, HEAD_DIM:HEAD_DIM + 1] + jnp.exp(sink - m)
        o = (ov[:, :HEAD_DIM] * (1.0 / den)).astype(o_ref.dtype)
        for n, h in enumerate(heads):
            o_ref[:, h * HEAD_DIM:(h + 1) * HEAD_DIM] = o[n * nq:(n + 1) * nq]


def _with_ones(v):
    return jnp.concatenate([v, jnp.ones_like(v)], axis=1)


def _split_kv(kv):
    ks = [kv[:, h * HEAD_DIM:(h + 1) * HEAD_DIM] for h in range(N_KV_HEADS)]
    vs = [kv[:, ATTN_KV + h * HEAD_DIM:ATTN_KV + (h + 1) * HEAD_DIM] for h in range(N_KV_HEADS)]
    return ks, vs


def _attn_lat_kernel(q_ref, kvp_ref, kvc_ref, kvn_ref, kvx_ref, sink_ref, o_ref, *, nb):
    n = pl.program_id(1)
    kp, vp = _split_kv(kvp_ref[...])
    kc, vc = _split_kv(kvc_ref[...])
    kn, vn = _split_kv(kvn_ref[...])
    k_ctx, v_ctx = _split_kv(kvx_ref[...])
    k_loc = [jnp.concatenate([kp[h], kc[h], kn[h]], axis=0) for h in range(N_KV_HEADS)]
    v_loc = [jnp.concatenate([vp[h], vc[h], vn[h]], axis=0) for h in range(N_KV_HEADS)]
    i = lax.broadcasted_iota(I32, (ATTN_BLOCK, 3 * ATTN_BLOCK), 0)
    j = lax.broadcasted_iota(I32, (ATTN_BLOCK, 3 * ATTN_BLOCK), 1)
    valid = (j >= i) & (j <= i + 2 * WINDOW)
    valid = valid & ((j >= ATTN_BLOCK) | (n > 0)) & ((j < 2 * ATTN_BLOCK) | (n < nb - 1))
    _attn_heads(q_ref[...], k_loc, v_loc, valid, k_ctx, v_ctx, sink_ref, o_ref)


def _attn_ctx_kernel(q_ref, kvx_ref, sink_ref, o_ref):
    k_ctx, v_ctx = _split_kv(kvx_ref[...])
    _attn_heads(q_ref[...], None, None, None, k_ctx, v_ctx, sink_ref, o_ref)


def _attn_latent(qkv_l, qkv_c, sink_b, B, S, C):
    nb = S // ATTN_BLOCK
    kvw = 2 * ATTN_KV
    kvcol = ATTN_Q // kvw
    return pl.pallas_call(
        functools.partial(_attn_lat_kernel, nb=nb),
        grid=(B, nb),
        in_specs=[
            pl.BlockSpec((ATTN_BLOCK, ATTN_Q), lambda b, n: (b * nb + n, 0)),
            pl.BlockSpec((ATTN_BLOCK, kvw), lambda b, n: (b * nb + jnp.maximum(n - 1, 0), kvcol)),
            pl.BlockSpec((ATTN_BLOCK, kvw), lambda b, n: (b * nb + n, kvcol)),
            pl.BlockSpec((ATTN_BLOCK, kvw), lambda b, n: (b * nb + jnp.minimum(n + 1, nb - 1), kvcol)),
            pl.BlockSpec((C, kvw), lambda b, n: (b, kvcol)),
            pl.BlockSpec((SUBLANES, LANES), lambda b, n: (0, 0)),
        ],
        out_specs=pl.BlockSpec((ATTN_BLOCK, ATTN_Q), lambda b, n: (b * nb + n, 0)),
        out_shape=jax.ShapeDtypeStruct((B * S, ATTN_Q), BF16),
        compiler_params=_cparams("parallel", "parallel"),
        name="attn_latent",
    )(qkv_l, qkv_l, qkv_l, qkv_l, qkv_c, sink_b)


def _attn_context(qkv_c, sink_b, B, C):
    nb = C // ATTN_BLOCK
    kvw = 2 * ATTN_KV
    kvcol = ATTN_Q // kvw
    return pl.pallas_call(
        _attn_ctx_kernel,
        grid=(B, nb),
        in_specs=[
            pl.BlockSpec((ATTN_BLOCK, ATTN_Q), lambda b, n: (b * nb + n, 0)),
            pl.BlockSpec((C, kvw), lambda b, n: (b, kvcol)),
            pl.BlockSpec((SUBLANES, LANES), lambda b, n: (0, 0)),
        ],
        out_specs=pl.BlockSpec((ATTN_BLOCK, ATTN_Q), lambda b, n: (b * nb + n, 0)),
        out_shape=jax.ShapeDtypeStruct((B * C, ATTN_Q), BF16),
        compiler_params=_cparams("parallel", "parallel"),
        name="attn_context",
    )(qkv_c, qkv_c, sink_b)


def _gla_constants():
    Lc = GLA_CHUNK
    e = np.zeros((GLA_LEVELS + 2, Lc, Lc), np.float32)
    msk = np.zeros((GLA_LEVELS + 1, Lc, Lc), np.float32)
    t = np.arange(Lc)
    for l in range(GLA_LEVELS):
        m = 1 << l
        blk = t // (2 * m)
        upper = (t % (2 * m)) >= m
        bnd = blk * 2 * m + m
        r = t[None, :]
        eq = upper[:, None] & (r >= bnd[:, None]) & (r <= t[:, None])
        ek = (~upper)[:, None] & (r > t[:, None]) & (r <= bnd[:, None] - 1)
        e[l] = (eq | ek).astype(np.float32)
        msk[l] = (upper[:, None] & (~upper)[None, :] & (blk[:, None] == blk[None, :])).astype(np.float32)
    e[GLA_LEVELS] = (t[None, :] <= t[:, None]).astype(np.float32)
    e[GLA_LEVELS + 1] = (t[None, :] > t[:, None]).astype(np.float32)
    msk[GLA_LEVELS] = np.eye(Lc, dtype=np.float32)
    e2 = np.stack([e, e[:, ::-1, ::-1]]).reshape(2, (GLA_LEVELS + 2) * Lc, Lc)
    m2 = np.stack([msk, msk[:, ::-1, ::-1]])
    return e2, m2


def _gla_kernel(x_ref, gz_ref, w2_ref, bias_ref, e_ref, m_ref, s0_ref, o_ref, sfin_ref, s_scr, *, nch):
    i = pl.program_id(2)

    @pl.when(i == 0)
    def _():
        s_scr[...] = s0_ref[:, 0]

    for bb in range(x_ref.shape[0]):
        _gla_chunk(x_ref.at[bb], gz_ref.at[bb], w2_ref, bias_ref, e_ref, m_ref, o_ref.at[0, bb], s_scr.at[bb])

    @pl.when(i == nch - 1)
    def _():
        sfin_ref[:, 0] = s_scr[...]


def _gla_chunk(x_ref, gz_ref, w2_ref, bias_ref, e_ref, m_ref, o_ref, s_scr):
    Lc = GLA_CHUNK
    z = jnp.dot(gz_ref[...].astype(BF16), w2_ref[0], preferred_element_type=F32) + bias_ref[0]
    la = (jnp.minimum(z, 0.0) - jnp.log1p(jnp.exp(-jnp.abs(z)))) * (1.0 / GLA_TAU)
    la_hi = la.astype(BF16)
    la_lo = (la - la_hi.astype(F32)).astype(BF16)
    la2 = jnp.concatenate([la_hi, la_lo], axis=1)
    ex = jnp.dot(e_ref[0], la2, preferred_element_type=F32)
    decay = jnp.exp(ex[:, :GLA_QK] + ex[:, GLA_QK:])
    tot = lax.dot_general(la2, jnp.ones((Lc, LANES), BF16), (((0,), (0,)), ((), ())), preferred_element_type=F32)
    a_tot = jnp.exp(tot[:GLA_QK] + tot[GLA_QK:])

    q = x_ref[:, 0:GLA_QK] * (GLA_DK ** -0.5)
    k = x_ref[:, GLA_QK:2 * GLA_QK]
    v = x_ref[:, 2 * GLA_QK:2 * GLA_QK + GLA_V].astype(BF16)
    row_head = lax.broadcasted_iota(I32, (GLA_QK, GLA_DV), 0) >> 6
    first_of_pair = (lax.broadcasted_iota(I32, (Lc, LANES), 1) >> 6) == 0
    nt = (((1,), (1,)), ((), ()))

    att = [jnp.zeros((Lc, Lc), F32) for _ in range(GLA_HEADS)]
    for l in range(GLA_LEVELS + 1):
        if l < GLA_LEVELS:
            dl = decay[l * Lc:(l + 1) * Lc]
            ql = (q * dl).astype(BF16)
            kl = (k * dl).astype(BF16)
        else:
            ql = q.astype(BF16)
            kl = k.astype(BF16)
        ml = m_ref[0, l]
        for pair in range(GLA_HEADS // 2):
            qp = ql[:, pair * LANES:(pair + 1) * LANES]
            kp = kl[:, pair * LANES:(pair + 1) * LANES]
            zero = jnp.zeros_like(kp)
            kk = jnp.concatenate([jnp.where(first_of_pair, kp, zero), jnp.where(first_of_pair, zero, kp)], axis=0)
            sc = lax.dot_general(qp, kk, nt, preferred_element_type=F32)
            att[2 * pair] = att[2 * pair] + ml * sc[:, :Lc]
            att[2 * pair + 1] = att[2 * pair + 1] + ml * sc[:, Lc:]

    s_old = s_scr[...]
    s_b = s_old.astype(BF16)
    zero_s = jnp.zeros_like(s_b)
    s_bd = jnp.concatenate([jnp.where(row_head == h, s_b, zero_s) for h in range(GLA_HEADS)], axis=1)
    q_in = (q * decay[GLA_LEVELS * Lc:(GLA_LEVELS + 1) * Lc]).astype(BF16)
    k_out = (k * decay[(GLA_LEVELS + 1) * Lc:(GLA_LEVELS + 2) * Lc]).astype(BF16)
    o_inter = jnp.dot(q_in, s_bd, preferred_element_type=F32)
    contrib = lax.dot_general(k_out, v, (((0,), (0,)), ((), ())), preferred_element_type=F32)
    s_new = a_tot * s_old
    for h in range(GLA_HEADS):
        cols = slice(h * GLA_DV, (h + 1) * GLA_DV)
        o_ref[:, cols] = jnp.dot(att[h].astype(BF16), v[:, cols], preferred_element_type=F32) + o_inter[:, cols]
        s_new = s_new + jnp.where(row_head == h, contrib[:, cols], 0.0)
    s_scr[...] = s_new


def _gla(gla_arr, gz_arr, w2p, bias, e2, m2, s0, B, T):
    nch = T // GLA_CHUNK
    nseq = 2 if B % 2 == 0 else 1
    chunk = lambda d, i: jnp.where(d == 0, i, nch - 1 - i)
    st = lambda b, d, i: (b, d, 0, 0)
    o, s_fin = pl.pallas_call(
        functools.partial(_gla_kernel, nch=nch),
        grid=(B // nseq, 2, nch),
        in_specs=[
            pl.BlockSpec((nseq, GLA_CHUNK, 2 * GLA_QK + GLA_V), lambda b, d, i: (b, chunk(d, i), 0)),
            pl.BlockSpec((nseq, GLA_CHUNK, GZ_W), lambda b, d, i: (b, chunk(d, i), 0)),
            pl.BlockSpec((1, GZ_W, GLA_QK), lambda b, d, i: (d, 0, 0)),
            pl.BlockSpec((1, 1, GLA_QK), lambda b, d, i: (d, 0, 0)),
            pl.BlockSpec((1, (GLA_LEVELS + 2) * GLA_CHUNK, GLA_CHUNK), lambda b, d, i: (d, 0, 0)),
            pl.BlockSpec((1, GLA_LEVELS + 1, GLA_CHUNK, GLA_CHUNK), lambda b, d, i: (d, 0, 0, 0)),
            pl.BlockSpec((nseq, 1, GLA_QK, GLA_DV), st),
        ],
        out_specs=[
            pl.BlockSpec((1, nseq, GLA_CHUNK, GLA_V), lambda b, d, i: (d, b, chunk(d, i), 0)),
            pl.BlockSpec((nseq, 1, GLA_QK, GLA_DV), st),
        ],
        out_shape=[
            jax.ShapeDtypeStruct((2, B, T, GLA_V), F32),
            jax.ShapeDtypeStruct((B, 2, GLA_QK, GLA_DV), F32),
        ],
        scratch_shapes=[pltpu.VMEM((nseq, GLA_QK, GLA_DV), F32)],
        compiler_params=_cparams("parallel", "parallel", "arbitrary"),
        name="gla_scan",
    )(gla_arr.reshape(B, T, GLA_W), gz_arr.reshape(B, T, GZ_W), w2p, bias, e2, m2, s0)
    return o, s_fin


def _lru_kernel(x_ref, pv_ref, nx_ref, cw_ref, cb_ref, wg_ref, bg_ref, lam_ref, h0_ref, o_ref, hfin_ref, carry, a_s, u_s,
                h_s, *, nblk):
    d = pl.program_id(1)
    i = pl.program_id(2)
    T = LRU_CHUNK
    W = LRU_WIDTH

    @pl.when(i == 0)
    def _():
        carry[...] = jnp.broadcast_to(h0_ref[0, 0], (SUBLANES, W))

    li = jnp.where(d == 0, i, nblk - 1 - i)
    pv = jnp.where(li > 0, pv_ref[...], 0.0)
    nx = jnp.where(li < nblk - 1, nx_ref[...], 0.0)
    xe = jnp.concatenate([pv, x_ref[...], nx], axis=0)
    n_ext = T + 2 * SUBLANES
    win = lambda off: pltpu.roll(xe, n_ext - off, 0)[0:T]
    cw = cw_ref[...]
    xc = cb_ref[...] + win(6) * cw[0:1] + win(7) * cw[1:2] + xe[SUBLANES:SUBLANES + T] * cw[2:3] + win(9) * cw[3:4]

    g = jnp.dot(xc.astype(BF16), wg_ref[0], preferred_element_type=F32) + bg_ref[0]
    r = _sigmoid(g[:, :W])
    gi = _sigmoid(g[:, W:])
    log_a = (-LRU_C * _softplus(-lam_ref[0])) * r
    a = jnp.exp(log_a)
    u = jnp.sqrt(-jnp.tanh(log_a) * (a * a + 1.0)) * (gi * xc)
    panels = W // LANES
    for c in range(panels):
        a_s[c] = a[:, c * LANES:(c + 1) * LANES]
        u_s[c] = u[:, c * LANES:(c + 1) * LANES]

    groups = T // SUBLANES
    slab = lambda ref, j: jnp.concatenate([ref[c, pl.ds(j, groups, stride=SUBLANES), :] for c in range(panels)], axis=1)
    g_row = lax.broadcasted_iota(I32, (groups, W), 0)

    def scan(order, shift, edge):
        us = {order[0]: slab(u_s, order[0])}
        ps = {order[0]: slab(a_s, order[0])}
        for prev, j in zip(order[:-1], order[1:]):
            aj = slab(a_s, j)
            us[j] = slab(u_s, j) + aj * us[prev]
            ps[j] = aj * ps[prev]
        cu, cp = us[order[-1]], ps[order[-1]]
        sh = 1
        while sh < groups:
            ok = (g_row >= sh) if shift == 1 else (g_row < groups - sh)
            amt = sh if shift == 1 else groups - sh
            cu = cu + cp * jnp.where(ok, pltpu.roll(cu, amt, 0), 0.0)
            cp = cp * jnp.where(ok, pltpu.roll(cp, amt, 0), 1.0)
            sh *= 2
        c_out = cu + cp * carry[0:1]
        c_in = jnp.where(g_row == edge, carry[0:1], pltpu.roll(c_out, shift if shift == 1 else groups - 1, 0))
        for j in order:
            hj = us[j] + ps[j] * c_in
            for c in range(panels):
                h_s[c, pl.ds(j, groups, stride=SUBLANES), :] = hj[:, c * LANES:(c + 1) * LANES]
        o_ref[0] = jnp.concatenate([h_s[c] for c in range(panels)], axis=1)
        last = groups - 1 - edge
        carry[...] = jnp.broadcast_to(c_out[last:last + 1], (SUBLANES, W))

    @pl.when(d == 0)
    def _():
        scan(list(range(SUBLANES)), 1, 0)

    @pl.when(d == 1)
    def _():
        scan(list(reversed(range(SUBLANES))), -1, groups - 1)

    @pl.when(i == nblk - 1)
    def _():
        hfin_ref[0, 0] = carry[0:1]


def _lru(lru_arr, cw, cb, wg, bg, lam, h0, B, T):
    nblk = T // LRU_CHUNK
    per8 = LRU_CHUNK // SUBLANES
    n8 = B * T // SUBLANES
    blk = lambda b, d, i: b * nblk + jnp.where(d == 0, i, nblk - 1 - i)
    dirw = lambda b, d, i: (d, 0, 0)
    st = lambda b, d, i: (b, d, 0, 0)
    return pl.pallas_call(
        functools.partial(_lru_kernel, nblk=nblk),
        grid=(B, 2, nblk),
        in_specs=[
            pl.BlockSpec((LRU_CHUNK, LRU_WIDTH), lambda b, d, i: (blk(b, d, i), 0)),
            pl.BlockSpec((SUBLANES, LRU_WIDTH), lambda b, d, i: (jnp.maximum(blk(b, d, i) * per8 - 1, 0), 0)),
            pl.BlockSpec((SUBLANES, LRU_WIDTH), lambda b, d, i: (jnp.minimum((blk(b, d, i) + 1) * per8, n8 - 1), 0)),
            pl.BlockSpec((SUBLANES, LRU_WIDTH), lambda b, d, i: (0, 0)),
            pl.BlockSpec((1, LRU_WIDTH), lambda b, d, i: (0, 0)),
            pl.BlockSpec((1, LRU_WIDTH, 2 * LRU_WIDTH), dirw),
            pl.BlockSpec((1, 1, 2 * LRU_WIDTH), dirw),
            pl.BlockSpec((1, 1, LRU_WIDTH), dirw),
            pl.BlockSpec((1, 1, 1, LRU_WIDTH), st),
        ],
        out_specs=[
            pl.BlockSpec((1, LRU_CHUNK, LRU_WIDTH), lambda b, d, i: (d, blk(b, d, i), 0)),
            pl.BlockSpec((1, 1, 1, LRU_WIDTH), st),
        ],
        out_shape=[
            jax.ShapeDtypeStruct((2, B * T, LRU_WIDTH), F32),
            jax.ShapeDtypeStruct((B, 2, 1, LRU_WIDTH), F32),
        ],
        scratch_shapes=[pltpu.VMEM((SUBLANES, LRU_WIDTH), F32)]
        + [pltpu.VMEM((LRU_WIDTH // LANES, LRU_CHUNK, LANES), F32) for _ in range(3)],
        compiler_params=_cparams("parallel", "parallel", "arbitrary"),
        name="lru_scan",
    )(lru_arr, lru_arr, lru_arr, cw, cb, wg, bg, lam, h0)


def _out_kernel(h_ref, attn_ref, go_ref, gr_ref, gg_ref, lh_ref, lg_ref, w_ref, g1_ref, a2_ref, s2_ref, wr_ref, br_ref,
                vt_ref, lt_ref, hn_ref, v_ref, lgt_ref, *, n_main):
    i = pl.program_id(0)

    @pl.when(i < n_main)
    def _():
        _out_body(h_ref, attn_ref, go_ref, gr_ref, gg_ref, lh_ref, lg_ref, w_ref, g1_ref, a2_ref, s2_ref, wr_ref, br_ref,
                  hn_ref, v_ref, lgt_ref)

    @pl.when(i >= n_main)
    def _():
        v_ref[...] = vt_ref[...]
        lgt_ref[...] = lt_ref[...]


def _out_body(h_ref, attn_ref, go_ref, gr_ref, gg_ref, lh_ref, lg_ref, w_ref, g1_ref, a2_ref, s2_ref, wr_ref, br_ref,
              hn_ref, v_ref, lgt_ref):
    o = go_ref[0, 0] + go_ref[1, 0]
    gr = gr_ref[...]
    gate = gr * _sigmoid(gr)
    parts = []
    for hh in range(GLA_HEADS):
        oh = o[:, hh * GLA_DV:(hh + 1) * GLA_DV]
        y = (oh * lax.rsqrt(jnp.mean(oh * oh, axis=-1, keepdims=True) + RMS_EPS)) * gg_ref[...]
        parts.append(y * gate[:, hh * GLA_DV:(hh + 1) * GLA_DV])
    gla = jnp.concatenate(parts, axis=1).astype(BF16)
    lg = lg_ref[...]
    gelu = lg * (0.5 * (1.0 + jnp.tanh(np.sqrt(2.0 / np.pi).astype(np.float32) * (lg + 0.044715 * (lg * lg * lg)))))
    lru = ((lh_ref[0] + lh_ref[1]) * gelu).astype(BF16)
    y = jnp.dot(attn_ref[...], w_ref[0:ATTN_Q], preferred_element_type=F32)
    y = y + jnp.dot(gla, w_ref[ATTN_Q:ATTN_Q + GLA_V], preferred_element_type=F32)
    y = y + jnp.dot(lru, w_ref[ATTN_Q + GLA_V:D_MIX], preferred_element_type=F32)
    hn = h_ref[...] + g1_ref[0] * y
    hn_ref[...] = hn
    v = (hn * lax.rsqrt(jnp.mean(hn * hn, axis=-1, keepdims=True) + RMS_EPS)) * a2_ref[0] + s2_ref[0]
    _store_token_tiles(v_ref, _pack_bf16_pairs(v))
    v_hi = v.astype(BF16)
    v_lo = (v - v_hi.astype(F32)).astype(BF16)
    t = jnp.dot(v_hi, wr_ref[...], preferred_element_type=F32)
    t_lo = jnp.dot(v_lo, wr_ref[:, 0:LANES], preferred_element_type=F32)
    lgt_ref[...] = (t[:, :LANES] + t[:, LANES:]) + t_lo + br_ref[...]


def _mixer_out(h, attn, go, gla_arr, gg, lh, lru_arr, w_out, g1, a2, s2, wr, br, seq_len, tail):
    rows, D = h.shape
    tm = _tile(seq_len, 256)
    per = seq_len // tm
    n_main = rows // tm
    tw = tm * (D // 2 // LANES)
    if tail is None:
        tail = (jnp.zeros((tw, LANES), I32), jnp.zeros((tm, LANES), F32))
        n_tail = 0
    else:
        assert tail[1].shape[0] % tm == 0
        n_tail = tail[1].shape[0] // tm
    n_tok = rows + n_tail * tm
    main = lambda i: jnp.minimum(i, n_main - 1)
    row = lambda i: (main(i), 0)
    row3 = lambda i: (0, main(i), 0)
    bat = lambda i: (main(i) // per, 0, 0)
    const = lambda i: (0, 0)
    tok = lambda i: (i, 0)
    trow = lambda i: (jnp.maximum(i - n_main, 0), 0)
    in_specs = [
        pl.BlockSpec((tm, D), row),
        pl.BlockSpec((tm, ATTN_Q), row),
        pl.BlockSpec((2, 1, tm, GLA_V), lambda i: (0, main(i) // per, main(i) % per, 0)),
        pl.BlockSpec((tm, GLA_V), lambda i: (main(i), 2)),
        pl.BlockSpec((1, GLA_DV), const),
        pl.BlockSpec((2, tm, LRU_WIDTH), row3),
        pl.BlockSpec((tm, LRU_WIDTH), lambda i: (main(i), 1)),
        pl.BlockSpec((D_MIX, D), const, pipeline_mode=pl.Buffered(1)),
        pl.BlockSpec((1, 1, D), bat),
        pl.BlockSpec((1, 1, D), bat),
        pl.BlockSpec((1, 1, D), bat),
        pl.BlockSpec((D, 2 * LANES), const),
        pl.BlockSpec((1, LANES), const),
        pl.BlockSpec((tw, LANES), trow),
        pl.BlockSpec((tm, LANES), trow),
    ]
    args = [h, attn, go, gla_arr, gg, lh, lru_arr, w_out, g1, a2, s2, wr, br, tail[0], tail[1]]
    return pl.pallas_call(
        functools.partial(_out_kernel, n_main=n_main),
        grid=(n_main + n_tail,),
        in_specs=in_specs,
        out_specs=[
            pl.BlockSpec((tm, D), row),
            pl.BlockSpec((tw, LANES), tok),
            pl.BlockSpec((tm, LANES), tok),
        ],
        out_shape=[
            jax.ShapeDtypeStruct((rows, D), F32),
            jax.ShapeDtypeStruct((n_tok * (tw // tm), LANES), I32),
            jax.ShapeDtypeStruct((n_tok, LANES), F32),
        ],
        compiler_params=_cparams("arbitrary"),
        name="mixer_out",
    )(*args)


def _route_kernel(lg_ref, tri_ref, oi_ref, of_ref, cnt_ref, carry):
    i = pl.program_id(0)

    @pl.when(i == 0)
    def _():
        carry[...] = jnp.zeros_like(carry)

    lg = lg_ref[...]
    col = lax.broadcasted_iota(I32, lg.shape, 1)
    colf = col.astype(F32)
    big = float(LANES)
    is_g = col < N_GROUPS
    gm = jnp.max(jnp.where(is_g, lg, -jnp.inf), axis=-1, keepdims=True)
    eg = jnp.where(is_g, jnp.exp(lg - gm), 0.0)
    pg = eg / jnp.sum(eg, axis=-1, keepdims=True)
    p_grp = jnp.max(pg, axis=-1, keepdims=True)
    grp = jnp.min(jnp.where(is_g & (pg == p_grp), colf, big), axis=-1, keepdims=True).astype(I32)

    sel = (col >= N_GROUPS) & (col < N_GROUPS + N_EXPERTS) & (((col - N_GROUPS) >> 3) == grp)
    em = jnp.max(jnp.where(sel, lg, -jnp.inf), axis=-1, keepdims=True)
    ee = jnp.where(sel, jnp.exp(lg - em), 0.0)
    pe = ee / jnp.sum(ee, axis=-1, keepdims=True)
    p1 = jnp.max(jnp.where(sel, pe, -1.0), axis=-1, keepdims=True)
    c1 = jnp.min(jnp.where(sel & (pe == p1), colf, big), axis=-1, keepdims=True).astype(I32)
    rest = sel & (col != c1)
    p2 = jnp.max(jnp.where(rest, pe, -1.0), axis=-1, keepdims=True)
    c2 = jnp.min(jnp.where(rest & (pe == p2), colf, big), axis=-1, keepdims=True).astype(I32)
    e1 = c1 - N_GROUPS
    e2 = c2 - N_GROUPS
    den = p1 + p2
    g1 = p_grp * (p1 / den)
    g2 = p_grp * (p2 / den)

    hit1 = col == e1
    hit2 = col == e2
    oh = jnp.where(hit1 | hit2, 1.0, 0.0)
    before = jnp.dot(tri_ref[...], oh.astype(BF16), preferred_element_type=F32) + carry[0:1]
    r1 = jnp.sum(jnp.where(hit1, before, 0.0), axis=-1, keepdims=True)
    r2 = jnp.sum(jnp.where(hit2, before, 0.0), axis=-1, keepdims=True)
    new = carry[0:1] + jnp.sum(oh, axis=0, keepdims=True)
    carry[...] = jnp.broadcast_to(new, carry.shape)
    cnt_ref[...] = jnp.broadcast_to(new, cnt_ref.shape)

    packed = jnp.where(col == 0, e1.astype(F32), jnp.where(col == 1, e2.astype(F32),
                                                           jnp.where(col == 2, r1, jnp.where(col == 3, r2, 0.0))))
    oi_ref[...] = packed.T[0:SUBLANES].astype(I32)
    of_ref[...] = jnp.where(col == 0, g1, jnp.where(col == 1, g2, 0.0))


def _route(logits):
    n_tok = logits.shape[0]
    tm = _tile(n_tok, 1024)
    row = lambda i: (i, 0)
    tri = jnp.tril(jnp.ones((tm, tm), BF16), -1)
    return pl.pallas_call(
        _route_kernel,
        grid=(n_tok // tm,),
        in_specs=[pl.BlockSpec((tm, LANES), row), pl.BlockSpec((tm, tm), lambda i: (0, 0))],
        out_specs=[
            pl.BlockSpec((SUBLANES, tm), lambda i: (0, i)),
            pl.BlockSpec((tm, LANES), row),
            pl.BlockSpec((SUBLANES, LANES), lambda i: (0, 0)),
        ],
        out_shape=[
            jax.ShapeDtypeStruct((SUBLANES, n_tok), I32),
            jax.ShapeDtypeStruct((n_tok, LANES), F32),
            jax.ShapeDtypeStruct((SUBLANES, LANES), F32),
        ],
        scratch_shapes=[pltpu.VMEM((SUBLANES, LANES), F32)],
        compiler_params=_cparams("arbitrary"),
        name="route",
    )(logits, tri)


def _dot_casting(x, stage, dst, rows, cols):
    n = rows.stop - rows.start
    step = _tile(n, 256)
    acc = None
    for c in range(n // step):
        r = slice(rows.start + c * step, rows.start + (c + 1) * step)
        w = stage[r, cols].astype(BF16)
        dst[r, cols] = w
        part = jnp.dot(x[:, c * step:(c + 1) * step], w, preferred_element_type=F32)
        acc = part if acc is None else acc + part
    return acc


def _moe_ffn_kernel(be_ref, nu_ref, nx_ref, src_ref, srcn_ref, v_hbm, w1_hbm, w3_hbm, w2_hbm, o_ref,
                    xbuf, st1, st3, st2, w1s, w3s, w2s, sem_x, sem_w, *, layer, sub):
    i = pl.program_id(0)
    nu = nu_ref[0]
    rows = ROW_BLOCK * sub

    def gather(idx_ref, slot):
        for r in range(ROW_BLOCK):
            s = pl.multiple_of(idx_ref[0, 0, r] * sub, sub)
            pltpu.make_async_copy(v_hbm.at[pl.ds(s, sub)], xbuf.at[slot, pl.ds(r * sub, sub)], sem_x.at[slot]).start()

    def wait_rows(slot):
        pltpu.make_async_copy(v_hbm.at[pl.ds(0, rows)], xbuf.at[slot], sem_x.at[slot]).wait()

    def weight_copies(e):
        return (pltpu.make_async_copy(w1_hbm.at[layer, e], st1, sem_w.at[0]),
                pltpu.make_async_copy(w3_hbm.at[layer, e], st3, sem_w.at[1]),
                pltpu.make_async_copy(w2_hbm.at[layer, e], st2, sem_w.at[2]))

    @pl.when(i < nu)
    def _():
        e = be_ref[i]
        slot = i & 1

        @pl.when(i == 0)
        def _():
            for cp in weight_copies(e):
                cp.start(priority=1)
            gather(src_ref, 0)

        wait_rows(slot)
        gather(srcn_ref, 1 - slot)
        xb = _unpack_bf16_pairs(_load_token_tiles(xbuf.at[slot], ROW_BLOCK)).astype(BF16)
        first = (i == 0) | (e != be_ref[jnp.maximum(i - 1, 0)])

        d_model, d_ff = w1s.shape
        n_ff = 1
        ffs = [slice(c * d_ff // n_ff, (c + 1) * d_ff // n_ff) for c in range(n_ff)]
        every = slice(0, d_model)

        def mlp(up1, up3, down):
            y = None
            for ff in ffs:
                a = up1(xb, ff)
                b = up3(xb, ff)
                part = down(((a * _sigmoid(a)) * b).astype(BF16), ff)
                y = part if y is None else y + part
            _store_token_tiles(o_ref, _pack_bf16_pairs(y))

        @pl.when(first)
        def _():
            for cp in weight_copies(e):
                cp.wait()
            mlp(lambda x, ff: _dot_casting(x, st1, w1s, every, ff), lambda x, ff: _dot_casting(x, st3, w3s, every, ff),
                lambda h, ff: _dot_casting(h, st2, w2s, ff, every))

            @pl.when(nx_ref[i] >= 0)
            def _():
                for cp in weight_copies(nx_ref[i]):
                    cp.start(priority=1)

        @pl.when(jnp.logical_not(first))
        def _():
            mlp(lambda x, ff: jnp.dot(x, w1s[:, ff], preferred_element_type=F32),
                lambda x, ff: jnp.dot(x, w3s[:, ff], preferred_element_type=F32),
                lambda h, ff: jnp.dot(h, w2s[ff, :], preferred_element_type=F32))

        @pl.when(i == nu - 1)
        def _():
            wait_rows(1 - slot)

    @pl.when(i >= nu)
    def _():
        o_ref[...] = jnp.zeros_like(o_ref)


def _moe_experts(v_tiles, src, blk_e, n_used, nxt_e, w1, w3, w2, layer):
    D, FF = w1.shape[-2:]
    sub = D // 2 // LANES
    P = src.shape[0]
    nb = P // ROW_BLOCK
    tw = ROW_BLOCK * sub
    cur = lambda i, be, nu, nx: (jnp.minimum(i, nu[0] - 1), 0, 0)
    nxt = lambda i, be, nu, nx: (jnp.minimum(i + 1, nu[0] - 1), 0, 0)
    src3 = src.reshape(nb, 1, ROW_BLOCK)
    hbm = pl.BlockSpec(memory_space=pl.ANY)
    return pl.pallas_call(
        functools.partial(_moe_ffn_kernel, layer=layer, sub=sub),
        grid_spec=pltpu.PrefetchScalarGridSpec(
            num_scalar_prefetch=3,
            grid=(nb,),
            in_specs=[
                pl.BlockSpec((1, 1, ROW_BLOCK), cur, memory_space=pltpu.SMEM),
                pl.BlockSpec((1, 1, ROW_BLOCK), nxt, memory_space=pltpu.SMEM),
                hbm, hbm, hbm, hbm,
            ],
            out_specs=pl.BlockSpec((tw, LANES), lambda i, be, nu, nx: (i, 0)),
            scratch_shapes=[
                pltpu.VMEM((2, tw, LANES), I32),
                pltpu.VMEM((D, FF), F32), pltpu.VMEM((D, FF), F32), pltpu.VMEM((FF, D), F32),
                pltpu.VMEM((D, FF), BF16), pltpu.VMEM((D, FF), BF16), pltpu.VMEM((FF, D), BF16),
                pltpu.SemaphoreType.DMA((2,)), pltpu.SemaphoreType.DMA((3,)),
            ],
        ),
        out_shape=jax.ShapeDtypeStruct((P * sub, LANES), I32),
        compiler_params=_cparams("arbitrary"),
        name="moe_ffn",
    )(blk_e, n_used, nxt_e, src3, src3, v_tiles, w1, w3, w2)


def _combine_kernel(dest_ref, destn_ref, h_ref, gate_ref, g2_ref, fg_ref, ys_ref, o_ref, buf, sem, *, final, nsteps):
    i = pl.program_id(0)
    slot = i & 1
    tm = h_ref.shape[0]
    rows = buf.shape[2]
    sub = rows // tm

    def gather(idx_ref, sl):
        for j in range(tm * TOP_K):
            src = pl.multiple_of(idx_ref[j % TOP_K, 0, 0, j // TOP_K] * sub, sub)
            cp = pltpu.make_async_copy(ys_ref.at[pl.ds(src, sub)], buf.at[sl, j % TOP_K, pl.ds((j // TOP_K) * sub, sub)],
                                       sem.at[sl])
            cp.start(priority=j % 2)

    def wait_rows(sl):
        for kk in range(TOP_K):
            pltpu.make_async_copy(ys_ref.at[pl.ds(0, rows)], buf.at[sl, kk], sem.at[sl]).wait()

    @pl.when(i == 0)
    def _():
        gather(dest_ref, 0)

    gather(destn_ref, 1 - slot)
    wait_rows(slot)
    gate = gate_ref[...]
    f = (_unpack_bf16_pairs(_load_token_tiles(buf.at[slot, 0], tm)) * gate[:, 0:1]
         + _unpack_bf16_pairs(_load_token_tiles(buf.at[slot, 1], tm)) * gate[:, 1:2])

    @pl.when(i == nsteps - 1)
    def _():
        wait_rows(1 - slot)

    hn = h_ref[...] + g2_ref[0] * f
    if final:
        hn = (hn * lax.rsqrt(jnp.mean(hn * hn, axis=-1, keepdims=True) + RMS_EPS)) * fg_ref[...]
    o_ref[...] = hn


def _combine(h, dest, gates, g2, fg, ys, seq_len, tok_off, final):
    rows, D = h.shape
    tm = _tile(seq_len, 256)
    per = seq_len // tm
    off = tok_off // tm
    nsteps = rows // tm
    dest3 = dest.reshape(TOP_K, dest.shape[1] // tm, 1, tm)
    return pl.pallas_call(
        functools.partial(_combine_kernel, final=final, nsteps=nsteps),
        grid=(nsteps,),
        in_specs=[
            pl.BlockSpec((TOP_K, 1, 1, tm), lambda i: (0, i + off, 0, 0), memory_space=pltpu.SMEM),
            pl.BlockSpec((TOP_K, 1, 1, tm), lambda i: (0, jnp.minimum(i + 1, nsteps - 1) + off, 0, 0),
                         memory_space=pltpu.SMEM),
            pl.BlockSpec((tm, D), lambda i: (i, 0)),
            pl.BlockSpec((tm, LANES), lambda i: (i + off, 0)),
            pl.BlockSpec((1, 1, D), lambda i: (i // per, 0, 0)),
            pl.BlockSpec((1, D), lambda i: (0, 0)),
            pl.BlockSpec(memory_space=pl.ANY),
        ],
        out_specs=pl.BlockSpec((tm, D), lambda i: (i, 0)),
        out_shape=jax.ShapeDtypeStruct((rows, D), F32),
        scratch_shapes=[pltpu.VMEM((2, TOP_K, tm * (D // 2 // LANES), LANES), I32), pltpu.SemaphoreType.DMA((2,))],
        compiler_params=_cparams("arbitrary"),
        name="combine",
    )(dest3, dest3, h, gates, g2, fg, ys)


def _rope_tables(S):
    rows = S // GRID_W
    row = jnp.repeat(jnp.arange(rows, dtype=F32), GRID_W)
    col = jnp.tile(jnp.arange(GRID_W, dtype=F32), rows)
    n_freq = HEAD_DIM // 4
    inv = ROPE_THETA ** (-jnp.arange(n_freq, dtype=F32) / n_freq)
    ang = jnp.concatenate([row[:, None] * inv, col[:, None] * inv], axis=-1)
    cos = jnp.repeat(jnp.cos(ang), 2, axis=1)
    sin = jnp.repeat(jnp.sin(ang), 2, axis=1)
    even = (jnp.arange(HEAD_DIM) % 2) == 0
    return cos, jnp.where(even, -sin, 0.0), jnp.where(even, 0.0, sin)


def _block_diag(w):
    n, c, _ = w.shape
    eye = jnp.eye(n, dtype=w.dtype)
    return (eye[:, None, :, None] * w[:, :, None, :]).reshape(n * c, n * c)


def _slot_plan(ids, counts_row, n_tok):
    e = ids[0:TOP_K]
    rank = ids[TOP_K:2 * TOP_K]
    counts = counts_row[:N_EXPERTS].astype(I32)
    padded = (counts + ROW_BLOCK - 1) // ROW_BLOCK * ROW_BLOCK
    pad_end = jnp.cumsum(padded)
    pad_start = pad_end - padded
    dest = pad_start[e] + rank
    n_blocks = -(-(n_tok * TOP_K + N_EXPERTS * (ROW_BLOCK - 1)) // ROW_BLOCK)
    starts = jnp.arange(n_blocks, dtype=I32) * ROW_BLOCK
    blk_e = jnp.minimum(jnp.sum(pad_end[None, :] <= starts[:, None], axis=1), N_EXPERTS - 1).astype(I32)
    n_used = (pad_end[-1] // ROW_BLOCK).astype(I32).reshape(1)
    run_next = pad_end[blk_e] // ROW_BLOCK
    nxt_e = jnp.where(run_next < n_used[0], blk_e[jnp.minimum(run_next, n_blocks - 1)], -1).astype(I32)
    tok = jnp.broadcast_to(jnp.arange(n_tok, dtype=I32), (TOP_K, n_tok))
    src = jnp.zeros((n_blocks * ROW_BLOCK,), I32).at[dest.reshape(-1)].set(tok.reshape(-1), unique_indices=True)
    return dest, src, blk_e, n_used, nxt_e


def kernel(x, c, ctx, c_ctx, ada_w, ada_b, norm_mix_g, norm_ffn_g, w_in, attn_sink, gla_gate_w2, gla_gate_b, gla_norm_g, lru_conv_w, lru_conv_b, lru_wa, lru_ba, lru_wx, lru_bx, lru_lambda, w_out, router_g_w, router_g_b, router_e_w, router_e_b, moe_w1, moe_w3, moe_w2, final_norm_g):
    B, S, D = x.shape
    C = ctx.shape[1]
    L = ada_w.shape[0]
    assert S % max(ATTN_BLOCK, GLA_CHUNK, LRU_CHUNK, GRID_W) == 0 and C % max(ATTN_BLOCK, GLA_CHUNK, LRU_CHUNK) == 0
    assert B + 1 <= SUBLANES and D % (2 * LANES) == 0

    cond = jnp.concatenate([c, c_ctx[None], jnp.zeros((SUBLANES - B - 1, D), F32)], axis=0)
    mods = _ada_mods(cond, ada_w, ada_b).reshape(L, SUBLANES, 6, D)

    cos_l, se_l, so_l = _rope_tables(S)
    cos_c = jnp.ones((C, HEAD_DIM), F32)
    zero_c = jnp.zeros((C, HEAD_DIM), F32)
    e2_np, m2_np = _gla_constants()
    gla_e = jnp.asarray(e2_np, BF16)
    gla_m = jnp.asarray(m2_np, F32)

    h_lat = x.reshape(B * S, D)
    h_ctx = ctx.reshape(B * C, D)
    out = None
    for l in range(L):
        last = l == L - 1
        ml = mods[l, :B]
        mc = jnp.broadcast_to(mods[l, B][None], (B, 6, D))
        per_b = lambda m, j: m[:, j][:, None, :]
        a1_l, a1_c = [(1.0 + per_b(m, 1)) * norm_mix_g[l] for m in (ml, mc)]
        a2_l, a2_c = [(1.0 + per_b(m, 4)) * norm_ffn_g[l] for m in (ml, mc)]

        w = w_in[l]
        c_gz = ATTN_Q + 2 * ATTN_KV + 2 * GLA_QK + 2 * GLA_V
        w_packed = jnp.concatenate(
            [w[:, :c_gz], w[:, c_gz:c_gz + 2 * GLA_RANK], jnp.zeros((D, GZ_W - 2 * GLA_RANK), F32), w[:, c_gz + 2 * GLA_RANK:]],
            axis=1).astype(BF16)
        qkv_l, gla_l, gz_l, lru_l = _in_proj(h_lat, a1_l, per_b(ml, 0), cos_l, se_l, so_l, w_packed, S)
        qkv_c, gla_c, gz_c, lru_c = _in_proj(h_ctx, a1_c, per_b(mc, 0), cos_c, zero_c, zero_c, w_packed, C)

        sink_b = jnp.broadcast_to(attn_sink[l][:, None], (N_Q_HEADS, LANES)).astype(F32)
        attn_l = _attn_latent(qkv_l, qkv_c, sink_b, B, S, C)

        w2p = jnp.zeros((2, GZ_W, GLA_QK), F32)
        w2p = w2p.at[0, :GLA_RANK].set(gla_gate_w2[l, 0]).at[1, GLA_RANK:2 * GLA_RANK].set(gla_gate_w2[l, 1]).astype(BF16)
        gbias = gla_gate_b[l].reshape(2, 1, GLA_QK)
        s_zero = jnp.zeros((B, 2, GLA_QK, GLA_DV), F32)
        go_c, s_ctx = _gla(gla_c, gz_c, w2p, gbias, gla_e, gla_m, s_zero, B, C)
        go_l, _ = _gla(gla_l, gz_l, w2p, gbias, gla_e, gla_m, s_ctx, B, S)

        cw = jnp.concatenate([lru_conv_w[l], jnp.zeros((SUBLANES - CONV_W, LRU_WIDTH), F32)], axis=0)
        cb = lru_conv_b[l].reshape(1, LRU_WIDTH)
        wg = jnp.stack([jnp.concatenate([_block_diag(lru_wa[l, d]), _block_diag(lru_wx[l, d])], axis=1) for d in range(2)]).astype(BF16)
        bg = jnp.concatenate([lru_ba[l], lru_bx[l]], axis=1).reshape(2, 1, 2 * LRU_WIDTH)
        lam = lru_lambda[l].reshape(2, 1, LRU_WIDTH)
        h_zero = jnp.zeros((B, 2, 1, LRU_WIDTH), F32)
        lh_c, hs_ctx = _lru(lru_c, cw, cb, wg, bg, lam, h_zero, B, C)
        lh_l, _ = _lru(lru_l, cw, cb, wg, bg, lam, hs_ctx, B, S)

        wo = w_out[l].astype(BF16)
        gg = gla_norm_g[l].reshape(1, GLA_DV)
        wr = jnp.concatenate([router_g_w[l], router_e_w[l], jnp.zeros((D, LANES - N_GROUPS - N_EXPERTS), F32)], axis=1)
        wr_hi = wr.astype(BF16)
        wr = jnp.concatenate([wr_hi, (wr - wr_hi.astype(F32)).astype(BF16)], axis=1)
        br =jnp.concatenate([router_g_b[l], router_e_b[l], jnp.zeros((LANES - N_GROUPS - N_EXPERTS,), F32)]).reshape(1, LANES)
        n_tok = B * S if last else B * (S + C)
        tail = None
        if not last:
            attn_c = _attn_context(qkv_c, sink_b, B, C)
            h_ctx, v_c, lgt_c = _mixer_out(h_ctx, attn_c, go_c, gla_c, gg, lh_c, lru_c, wo, per_b(mc, 2), a2_c, per_b(mc, 3),
                                           wr, br, C, None)
            tail = (v_c, lgt_c)
        h_lat, v_all, lgt_all = _mixer_out(h_lat, attn_l, go_l, gla_l, gg, lh_l, lru_l, wo, per_b(ml, 2), a2_l, per_b(ml, 3),
                                           wr, br, S, tail)

        ids, gates, counts = _route(lgt_all)
        dest, src, blk_e, n_used, nxt_e = _slot_plan(ids, counts[0], n_tok)
        ys = _moe_experts(v_all, src, blk_e, n_used, nxt_e, moe_w1, moe_w3, moe_w2, l)
        fg = final_norm_g.reshape(1, D)
        h_lat = _combine(h_lat, dest, gates, per_b(ml, 5), fg, ys, S, 0, last)
        if not last:
            h_ctx = _combine(h_ctx, dest, gates, per_b(mc, 5), fg, ys, C, B * S, False)
        out = h_lat
    return out.reshape(B, S, D)
```

```python
import functools

import numpy as np
import jax
import jax.numpy as jnp
from jax import lax
from jax.experimental import pallas as pl
from jax.experimental.pallas import tpu as pltpu

F32 = jnp.float32
BF16 = jnp.bfloat16
I32 = jnp.int32

GRID_W = 64
RMS_EPS = 1e-6
N_Q_HEADS = 8
N_KV_HEADS = 2
HEAD_DIM = 128
WINDOW = 128
ATTN_BLOCK = 128
ROPE_THETA = 10000.0
GLA_HEADS = 4
GLA_DK = 64
GLA_DV = 128
GLA_RANK = 16
GLA_TAU = 16.0
LRU_WIDTH = 512
LRU_BLOCKS = 8
LRU_C = 8.0
CONV_W = 4
N_GROUPS = 4
EXPERTS_PER_GROUP = 8
N_EXPERTS = 32
TOP_K = 2
ATTN_Q = N_Q_HEADS * HEAD_DIM
ATTN_KV = N_KV_HEADS * HEAD_DIM
GLA_QK = GLA_HEADS * GLA_DK
GLA_V = GLA_HEADS * GLA_DV
D_MIX = ATTN_Q + GLA_V + LRU_WIDTH

LANES = 128
SUBLANES = 8
VMEM_LIMIT_BYTES = 56 * 1024 * 1024

QKV_W = ATTN_Q + 2 * ATTN_KV
GLA_W = 2 * GLA_QK + 2 * GLA_V
GZ_W = LANES
LRU_W = 2 * LRU_WIDTH
COL_GLA = QKV_W
COL_GZ = COL_GLA + GLA_W
COL_LRU = COL_GZ + GZ_W
W_IN_PACKED = COL_LRU + LRU_W

GLA_CHUNK = 128
GLA_LEVELS = 7
LRU_CHUNK = 256
ROW_BLOCK = 256


def _cparams(*sem):
    return pltpu.CompilerParams(dimension_semantics=sem, vmem_limit_bytes=VMEM_LIMIT_BYTES)


def _tile(n, pref):
    t = min(n, pref)
    while n % t:
        t -= SUBLANES
    return t


def _sigmoid(x):
    return 0.5 * (jnp.tanh(0.5 * x) + 1.0)


def _softplus(x):
    return jnp.maximum(x, 0.0) + jnp.log1p(jnp.exp(-jnp.abs(x)))


def _pack_bf16_pairs(x):
    n = x.shape[1] // 2
    lo = lax.bitcast_convert_type(x[:, :n].astype(BF16).astype(F32), I32)
    hi = lax.bitcast_convert_type(x[:, n:].astype(BF16).astype(F32), I32)
    return hi | lax.shift_right_logical(lo, jnp.full(lo.shape, 16, I32))


def _unpack_bf16_pairs(p):
    lo = lax.bitcast_convert_type(lax.shift_left(p, jnp.full(p.shape, 16, I32)), F32)
    hi = lax.bitcast_convert_type(p & jnp.int32(-65536), F32)
    return jnp.concatenate([lo, hi], axis=1)


def _store_token_tiles(ref, packed):
    m = packed.shape[0]
    sub = packed.shape[1] // LANES
    for s in range(sub):
        ref[pl.ds(s, m, stride=sub), :] = packed[:, s * LANES:(s + 1) * LANES]


def _load_token_tiles(ref, m):
    sub = ref.shape[0] // m
    return jnp.concatenate([ref[pl.ds(s, m, stride=sub), :] for s in range(sub)], axis=1)


def _ada_kernel(c_ref, w_ref, b_ref, o_ref):
    c = c_ref[...]
    s = c * _sigmoid(c)
    o_ref[0] = jnp.dot(s.astype(BF16), w_ref[0].astype(BF16), preferred_element_type=F32) + b_ref[0]


def _ada_mods(cond, ada_w, ada_b):
    L, D, D6 = ada_w.shape
    tn = _tile(D6, 1536)
    while tn % LANES:
        tn -= SUBLANES
    return pl.pallas_call(
        _ada_kernel,
        grid=(L, D6 // tn),
        in_specs=[
            pl.BlockSpec((SUBLANES, D), lambda l, j: (0, 0)),
            pl.BlockSpec((1, D, tn), lambda l, j: (l, 0, j)),
            pl.BlockSpec((1, 1, tn), lambda l, j: (l, 0, j)),
        ],
        out_specs=pl.BlockSpec((1, SUBLANES, tn), lambda l, j: (l, 0, j)),
        out_shape=jax.ShapeDtypeStruct((L, SUBLANES, D6), F32),
        compiler_params=_cparams("parallel", "parallel"),
        name="ada_mods",
    )(cond, ada_w, ada_b.reshape(L, 1, D6))


def _in_kernel(x_ref, a_ref, s_ref, cos_ref, se_ref, so_ref, w_ref, qkv_ref, gla_ref, gz_ref, lru_ref):
    x = x_ref[...]
    ms = jnp.mean(x * x, axis=-1, keepdims=True)
    u = (x * lax.rsqrt(ms + RMS_EPS)) * a_ref[0] + s_ref[0]
    ub = u.astype(BF16)
    cos, se, so = cos_ref[...], se_ref[...], so_ref[...]
    n_rot = N_Q_HEADS + N_KV_HEADS
    for jp in range(n_rot // 2):
        z2 = jnp.dot(ub, w_ref[:, 2 * jp * HEAD_DIM:(2 * jp + 2) * HEAD_DIM], preferred_element_type=F32)
        for j in (2 * jp, 2 * jp + 1):
            zh = z2[:, (j - 2 * jp) * HEAD_DIM:(j - 2 * jp + 1) * HEAD_DIM]
            rot = zh * cos + pltpu.roll(zh, HEAD_DIM - 1, 1) * se + pltpu.roll(zh, 1, 1) * so
            qkv_ref[:, j * HEAD_DIM:(j + 1) * HEAD_DIM] = rot.astype(BF16)
    c0 = n_rot * HEAD_DIM
    qkv_ref[:, c0:QKV_W] = jnp.dot(ub, w_ref[:, c0:QKV_W], preferred_element_type=F32).astype(BF16)
    gla_ref[...] = jnp.dot(ub, w_ref[:, COL_GLA:COL_GZ], preferred_element_type=F32)
    gz_ref[...] = jnp.dot(ub, w_ref[:, COL_GZ:COL_LRU], preferred_element_type=F32)
    lru_ref[...] = jnp.dot(ub, w_ref[:, COL_LRU:W_IN_PACKED], preferred_element_type=F32)


def _in_proj(h, a, s, cos, se, so, w, seq_len):
    rows, D = h.shape
    tm = _tile(seq_len, 512)
    per = seq_len // tm
    row = lambda i: (i, 0)
    bat = lambda i: (i // per, 0, 0)
    tab = lambda i: (i % per, 0)
    return pl.pallas_call(
        _in_kernel,
        grid=(rows // tm,),
        in_specs=[
            pl.BlockSpec((tm, D), row),
            pl.BlockSpec((1, 1, D), bat),
            pl.BlockSpec((1, 1, D), bat),
            pl.BlockSpec((tm, HEAD_DIM), tab),
            pl.BlockSpec((tm, HEAD_DIM), tab),
            pl.BlockSpec((tm, HEAD_DIM), tab),
            pl.BlockSpec((D, W_IN_PACKED), lambda i: (0, 0), pipeline_mode=pl.Buffered(1)),
        ],
        out_specs=[
            pl.BlockSpec((tm, QKV_W), row),
            pl.BlockSpec((tm, GLA_W), row),
            pl.BlockSpec((tm, GZ_W), row),
            pl.BlockSpec((tm, LRU_W), row),
        ],
        out_shape=[
            jax.ShapeDtypeStruct((rows, QKV_W), BF16),
            jax.ShapeDtypeStruct((rows, GLA_W), F32),
            jax.ShapeDtypeStruct((rows, GZ_W), F32),
            jax.ShapeDtypeStruct((rows, LRU_W), F32),
        ],
        compiler_params=_cparams("parallel"),
        name="in_proj",
    )(h, a, s, cos, se, so, w)


def _attn_heads(q, k_loc, v_loc, valid, k_ctx, v_ctx, sink_ref, o_ref):
    scale = HEAD_DIM ** -0.5
    group = N_Q_HEADS // N_KV_HEADS
    nq = q.shape[0]
    nt = (((1,), (1,)), ((), ()))
    if valid is not None:
        valid = jnp.concatenate([valid] * group, axis=0)
    for hk in range(N_KV_HEADS):
        heads = range(hk * group, (hk + 1) * group)
        qg = jnp.concatenate([q[:, h * HEAD_DIM:(h + 1) * HEAD_DIM] for h in heads], axis=0)
        sink = jnp.concatenate([jnp.broadcast_to(sink_ref[h:h + 1, 0:1], (nq, 1)) for h in heads], axis=0)
        s_ctx = lax.dot_general(qg, k_ctx[hk], nt, preferred_element_type=F32) * scale
        m = jnp.maximum(jnp.max(s_ctx, axis=-1, keepdims=True), sink)
        if k_loc is not None:
            s_loc = lax.dot_general(qg, k_loc[hk], nt, preferred_element_type=F32) * scale
            s_loc = jnp.where(valid, s_loc, -jnp.inf)
            m = jnp.maximum(m, jnp.max(s_loc, axis=-1, keepdims=True))
        ov = jnp.dot(jnp.exp(s_ctx - m).astype(BF16), _with_ones(v_ctx[hk]), preferred_element_type=F32)
        if k_loc is not None:
            ov = ov + jnp.dot(jnp.exp(s_loc - m).astype(BF16), _with_ones(v_loc[hk]), preferred_element_type=F32)
        den = ov[:, HEAD_DIM:HEAD_DIM + 1] + jnp.exp(sink - m)
        o = (ov[:, :HEAD_DIM] * (1.0 / den)).astype(o_ref.dtype)
        for n, h in enumerate(heads):
            o_ref[:, h * HEAD_DIM:(h + 1) * HEAD_DIM] = o[n * nq:(n + 1) * nq]


def _with_ones(v):
    return jnp.concatenate([v, jnp.ones_like(v)], axis=1)


def _split_kv(kv):
    ks = [kv[:, h * HEAD_DIM:(h + 1) * HEAD_DIM] for h in range(N_KV_HEADS)]
    vs = [kv[:, ATTN_KV + h * HEAD_DIM:ATTN_KV + (h + 1) * HEAD_DIM] for h in range(N_KV_HEADS)]
    return ks, vs


def _attn_lat_kernel(q_ref, kvp_ref, kvc_ref, kvn_ref, kvx_ref, sink_ref, o_ref, *, nb):
    n = pl.program_id(1)
    kp, vp = _split_kv(kvp_ref[...])
    kc, vc = _split_kv(kvc_ref[...])
    kn, vn = _split_kv(kvn_ref[...])
    k_ctx, v_ctx = _split_kv(kvx_ref[...])
    k_loc = [jnp.concatenate([kp[h], kc[h], kn[h]], axis=0) for h in range(N_KV_HEADS)]
    v_loc = [jnp.concatenate([vp[h], vc[h], vn[h]], axis=0) for h in range(N_KV_HEADS)]
    i = lax.broadcasted_iota(I32, (ATTN_BLOCK, 3 * ATTN_BLOCK), 0)
    j = lax.broadcasted_iota(I32, (ATTN_BLOCK, 3 * ATTN_BLOCK), 1)
    valid = (j >= i) & (j <= i + 2 * WINDOW)
    valid = valid & ((j >= ATTN_BLOCK) | (n > 0)) & ((j < 2 * ATTN_BLOCK) | (n < nb - 1))
    _attn_heads(q_ref[...], k_loc, v_loc, valid, k_ctx, v_ctx, sink_ref, o_ref)


def _attn_ctx_kernel(q_ref, kvx_ref, sink_ref, o_ref):
    k_ctx, v_ctx = _split_kv(kvx_ref[...])
    _attn_heads(q_ref[...], None, None, None, k_ctx, v_ctx, sink_ref, o_ref)


def _attn_latent(qkv_l, qkv_c, sink_b, B, S, C):
    nb = S // ATTN_BLOCK
    kvw = 2 * ATTN_KV
    kvcol = ATTN_Q // kvw
    return pl.pallas_call(
        functools.partial(_attn_lat_kernel, nb=nb),
        grid=(B, nb),
        in_specs=[
            pl.BlockSpec((ATTN_BLOCK, ATTN_Q), lambda b, n: (b * nb + n, 0)),
            pl.BlockSpec((ATTN_BLOCK, kvw), lambda b, n: (b * nb + jnp.maximum(n - 1, 0), kvcol)),
            pl.BlockSpec((ATTN_BLOCK, kvw), lambda b, n: (b * nb + n, kvcol)),
            pl.BlockSpec((ATTN_BLOCK, kvw), lambda b, n: (b * nb + jnp.minimum(n + 1, nb - 1), kvcol)),
            pl.BlockSpec((C, kvw), lambda b, n: (b, kvcol)),
            pl.BlockSpec((SUBLANES, LANES), lambda b, n: (0, 0)),
        ],
        out_specs=pl.BlockSpec((ATTN_BLOCK, ATTN_Q), lambda b, n: (b * nb + n, 0)),
        out_shape=jax.ShapeDtypeStruct((B * S, ATTN_Q), BF16),
        compiler_params=_cparams("parallel", "parallel"),
        name="attn_latent",
    )(qkv_l, qkv_l, qkv_l, qkv_l, qkv_c, sink_b)


def _attn_context(qkv_c, sink_b, B, C):
    nb = C // ATTN_BLOCK
    kvw = 2 * ATTN_KV
    kvcol = ATTN_Q // kvw
    return pl.pallas_call(
        _attn_ctx_kernel,
        grid=(B, nb),
        in_specs=[
            pl.BlockSpec((ATTN_BLOCK, ATTN_Q), lambda b, n: (b * nb + n, 0)),
            pl.BlockSpec((C, kvw), lambda b, n: (b, kvcol)),
            pl.BlockSpec((SUBLANES, LANES), lambda b, n: (0, 0)),
        ],
        out_specs=pl.BlockSpec((ATTN_BLOCK, ATTN_Q), lambda b, n: (b * nb + n, 0)),
        out_shape=jax.ShapeDtypeStruct((B * C, ATTN_Q), BF16),
        compiler_params=_cparams("parallel", "parallel"),
        name="attn_context",
    )(qkv_c, qkv_c, sink_b)


def _gla_constants():
    Lc = GLA_CHUNK
    e = np.zeros((GLA_LEVELS + 2, Lc, Lc), np.float32)
    msk = np.zeros((GLA_LEVELS + 1, Lc, Lc), np.float32)
    t = np.arange(Lc)
    for l in range(GLA_LEVELS):
        m = 1 << l
        blk = t // (2 * m)
        upper = (t % (2 * m)) >= m
        bnd = blk * 2 * m + m
        r = t[None, :]
        eq = upper[:, None] & (r >= bnd[:, None]) & (r <= t[:, None])
        ek = (~upper)[:, None] & (r > t[:, None]) & (r <= bnd[:, None] - 1)
        e[l] = (eq | ek).astype(np.float32)
        msk[l] = (upper[:, None] & (~upper)[None, :] & (blk[:, None] == blk[None, :])).astype(np.float32)
    e[GLA_LEVELS] = (t[None, :] <= t[:, None]).astype(np.float32)
    e[GLA_LEVELS + 1] = (t[None, :] > t[:, None]).astype(np.float32)
    msk[GLA_LEVELS] = np.eye(Lc, dtype=np.float32)
    e2 = np.stack([e, e[:, ::-1, ::-1]]).reshape(2, (GLA_LEVELS + 2) * Lc, Lc)
    m2 = np.stack([msk, msk[:, ::-1, ::-1]])
    return e2, m2


def _gla_kernel(x_ref, gz_ref, w2_ref, bias_ref, e_ref, m_ref, s0_ref, o_ref, sfin_ref, s_scr, *, nch):
    i = pl.program_id(2)

    @pl.when(i == 0)
    def _():
        s_scr[...] = s0_ref[:, 0]

    for bb in range(x_ref.shape[0]):
        _gla_chunk(x_ref.at[bb], gz_ref.at[bb], w2_ref, bias_ref, e_ref, m_ref, o_ref.at[0, bb], s_scr.at[bb])

    @pl.when(i == nch - 1)
    def _():
        sfin_ref[:, 0] = s_scr[...]


def _gla_chunk(x_ref, gz_ref, w2_ref, bias_ref, e_ref, m_ref, o_ref, s_scr):
    Lc = GLA_CHUNK
    z = jnp.dot(gz_ref[...].astype(BF16), w2_ref[0], preferred_element_type=F32) + bias_ref[0]
    la = (jnp.minimum(z, 0.0) - jnp.log1p(jnp.exp(-jnp.abs(z)))) * (1.0 / GLA_TAU)
    la_hi = la.astype(BF16)
    la_lo = (la - la_hi.astype(F32)).astype(BF16)
    la2 = jnp.concatenate([la_hi, la_lo], axis=1)
    ex = jnp.dot(e_ref[0], la2, preferred_element_type=F32)
    decay = jnp.exp(ex[:, :GLA_QK] + ex[:, GLA_QK:])
    tot = lax.dot_general(la2, jnp.ones((Lc, LANES), BF16), (((0,), (0,)), ((), ())), preferred_element_type=F32)
    a_tot = jnp.exp(tot[:GLA_QK] + tot[GLA_QK:])

    q = x_ref[:, 0:GLA_QK] * (GLA_DK ** -0.5)
    k = x_ref[:, GLA_QK:2 * GLA_QK]
    v = x_ref[:, 2 * GLA_QK:2 * GLA_QK + GLA_V].astype(BF16)
    row_head = lax.broadcasted_iota(I32, (GLA_QK, GLA_DV), 0) >> 6
    first_of_pair = (lax.broadcasted_iota(I32, (Lc, LANES), 1) >> 6) == 0
    nt = (((1,), (1,)), ((), ()))

    att = [jnp.zeros((Lc, Lc), F32) for _ in range(GLA_HEADS)]
    for l in range(GLA_LEVELS + 1):
        if l < GLA_LEVELS:
            dl = decay[l * Lc:(l + 1) * Lc]
            ql = (q * dl).astype(BF16)
            kl = (k * dl).astype(BF16)
        else:
            ql = q.astype(BF16)
            kl = k.astype(BF16)
        ml = m_ref[0, l]
        for pair in range(GLA_HEADS // 2):
            qp = ql[:, pair * LANES:(pair + 1) * LANES]
            kp = kl[:, pair * LANES:(pair + 1) * LANES]
            zero = jnp.zeros_like(kp)
            kk = jnp.concatenate([jnp.where(first_of_pair, kp, zero), jnp.where(first_of_pair, zero, kp)], axis=0)
            sc = lax.dot_general(qp, kk, nt, preferred_element_type=F32)
            att[2 * pair] = att[2 * pair] + ml * sc[:, :Lc]
            att[2 * pair + 1] = att[2 * pair + 1] + ml * sc[:, Lc:]

    s_old = s_scr[...]
    s_b = s_old.astype(BF16)
    zero_s = jnp.zeros_like(s_b)
    s_bd = jnp.concatenate([jnp.where(row_head == h, s_b, zero_s) for h in range(GLA_HEADS)], axis=1)
    q_in = (q * decay[GLA_LEVELS * Lc:(GLA_LEVELS + 1) * Lc]).astype(BF16)
    k_out = (k * decay[(GLA_LEVELS + 1) * Lc:(GLA_LEVELS + 2) * Lc]).astype(BF16)
    o_inter = jnp.dot(q_in, s_bd, preferred_element_type=F32)
    contrib = lax.dot_general(k_out, v, (((0,), (0,)), ((), ())), preferred_element_type=F32)
    s_new = a_tot * s_old
    for h in range(GLA_HEADS):
        cols = slice(h * GLA_DV, (h + 1) * GLA_DV)
        o_ref[:, cols] = jnp.dot(att[h].astype(BF16), v[:, cols], preferred_element_type=F32) + o_inter[:, cols]
        s_new = s_new + jnp.where(row_head == h, contrib[:, cols], 0.0)
    s_scr[...] = s_new


def _gla(gla_arr, gz_arr, w2p, bias, e2, m2, s0, B, T):
    nch = T // GLA_CHUNK
    nseq = 2 if B % 2 == 0 else 1
    chunk = lambda d, i: jnp.where(d == 0, i, nch - 1 - i)
    st = lambda b, d, i: (b, d, 0, 0)
    o, s_fin = pl.pallas_call(
        functools.partial(_gla_kernel, nch=nch),
        grid=(B // nseq, 2, nch),
        in_specs=[
            pl.BlockSpec((nseq, GLA_CHUNK, 2 * GLA_QK + GLA_V), lambda b, d, i: (b, chunk(d, i), 0)),
            pl.BlockSpec((nseq, GLA_CHUNK, GZ_W), lambda b, d, i: (b, chunk(d, i), 0)),
            pl.BlockSpec((1, GZ_W, GLA_QK), lambda b, d, i: (d, 0, 0)),
            pl.BlockSpec((1, 1, GLA_QK), lambda b, d, i: (d, 0, 0)),
            pl.BlockSpec((1, (GLA_LEVELS + 2) * GLA_CHUNK, GLA_CHUNK), lambda b, d, i: (d, 0, 0)),
            pl.BlockSpec((1, GLA_LEVELS + 1, GLA_CHUNK, GLA_CHUNK), lambda b, d, i: (d, 0, 0, 0)),
            pl.BlockSpec((nseq, 1, GLA_QK, GLA_DV), st),
        ],
        out_specs=[
            pl.BlockSpec((1, nseq, GLA_CHUNK, GLA_V), lambda b, d, i: (d, b, chunk(d, i), 0)),
            pl.BlockSpec((nseq, 1, GLA_QK, GLA_DV), st),
        ],
        out_shape=[
            jax.ShapeDtypeStruct((2, B, T, GLA_V), F32),
            jax.ShapeDtypeStruct((B, 2, GLA_QK, GLA_DV), F32),
        ],
        scratch_shapes=[pltpu.VMEM((nseq, GLA_QK, GLA_DV), F32)],
        compiler_params=_cparams("parallel", "parallel", "arbitrary"),
        name="gla_scan",
    )(gla_arr.reshape(B, T, GLA_W), gz_arr.reshape(B, T, GZ_W), w2p, bias, e2, m2, s0)
    return o, s_fin


def _lru_kernel(x_ref, pv_ref, nx_ref, cw_ref, cb_ref, wg_ref, bg_ref, lam_ref, h0_ref, o_ref, hfin_ref, carry, a_s, u_s,
                h_s, *, nblk):
    d = pl.program_id(1)
    i = pl.program_id(2)
    T = LRU_CHUNK
    W = LRU_WIDTH

    @pl.when(i == 0)
    def _():
        carry[...] = jnp.broadcast_to(h0_ref[0, 0], (SUBLANES, W))

    li = jnp.where(d == 0, i, nblk - 1 - i)
    pv = jnp.where(li > 0, pv_ref[...], 0.0)
    nx = jnp.where(li < nblk - 1, nx_ref[...], 0.0)
    xe = jnp.concatenate([pv, x_ref[...], nx], axis=0)
    n_ext = T + 2 * SUBLANES
    win = lambda off: pltpu.roll(xe, n_ext - off, 0)[0:T]
    cw = cw_ref[...]
    xc = cb_ref[...] + win(6) * cw[0:1] + win(7) * cw[1:2] + xe[SUBLANES:SUBLANES + T] * cw[2:3] + win(9) * cw[3:4]

    g = jnp.dot(xc.astype(BF16), wg_ref[0], preferred_element_type=F32) + bg_ref[0]
    r = _sigmoid(g[:, :W])
    gi = _sigmoid(g[:, W:])
    log_a = (-LRU_C * _softplus(-lam_ref[0])) * r
    a = jnp.exp(log_a)
    u = jnp.sqrt(-jnp.tanh(log_a) * (a * a + 1.0)) * (gi * xc)
    panels = W // LANES
    for c in range(panels):
        a_s[c] = a[:, c * LANES:(c + 1) * LANES]
        u_s[c] = u[:, c * LANES:(c + 1) * LANES]

    groups = T // SUBLANES
    slab = lambda ref, j: jnp.concatenate([ref[c, pl.ds(j, groups, stride=SUBLANES), :] for c in range(panels)], axis=1)
    g_row = lax.broadcasted_iota(I32, (groups, W), 0)

    def scan(order, shift, edge):
        us = {order[0]: slab(u_s, order[0])}
        ps = {order[0]: slab(a_s, order[0])}
        for prev, j in zip(order[:-1], order[1:]):
            aj = slab(a_s, j)
            us[j] = slab(u_s, j) + aj * us[prev]
            ps[j] = aj * ps[prev]
        cu, cp = us[order[-1]], ps[order[-1]]
        sh = 1
        while sh < groups:
            ok = (g_row >= sh) if shift == 1 else (g_row < groups - sh)
            amt = sh if shift == 1 else groups - sh
            cu = cu + cp * jnp.where(ok, pltpu.roll(cu, amt, 0), 0.0)
            cp = cp * jnp.where(ok, pltpu.roll(cp, amt, 0), 1.0)
            sh *= 2
        c_out = cu + cp * carry[0:1]
        c_in = jnp.where(g_row == edge, carry[0:1], pltpu.roll(c_out, shift if shift == 1 else groups - 1, 0))
        for j in order:
            hj = us[j] + ps[j] * c_in
            for c in range(panels):
                h_s[c, pl.ds(j, groups, stride=SUBLANES), :] = hj[:, c * LANES:(c + 1) * LANES]
        o_ref[0] = jnp.concatenate([h_s[c] for c in range(panels)], axis=1)
        last = groups - 1 - edge
        carry[...] = jnp.broadcast_to(c_out[last:last + 1], (SUBLANES, W))

    @pl.when(d == 0)
    def _():
        scan(list(range(SUBLANES)), 1, 0)

    @pl.when(d == 1)
    def _():
        scan(list(reversed(range(SUBLANES))), -1, groups - 1)

    @pl.when(i == nblk - 1)
    def _():
        hfin_ref[0, 0] = carry[0:1]


def _lru(lru_arr, cw, cb, wg, bg, lam, h0, B, T):
    nblk = T // LRU_CHUNK
    per8 = LRU_CHUNK // SUBLANES
    n8 = B * T // SUBLANES
    blk = lambda b, d, i: b * nblk + jnp.where(d == 0, i, nblk - 1 - i)
    dirw = lambda b, d, i: (d, 0, 0)
    st = lambda b, d, i: (b, d, 0, 0)
    return pl.pallas_call(
        functools.partial(_lru_kernel, nblk=nblk),
        grid=(B, 2, nblk),
        in_specs=[
            pl.BlockSpec((LRU_CHUNK, LRU_WIDTH), lambda b, d, i: (blk(b, d, i), 0)),
            pl.BlockSpec((SUBLANES, LRU_WIDTH), lambda b, d, i: (jnp.maximum(blk(b, d, i) * per8 - 1, 0), 0)),
            pl.BlockSpec((SUBLANES, LRU_WIDTH), lambda b, d, i: (jnp.minimum((blk(b, d, i) + 1) * per8, n8 - 1), 0)),
            pl.BlockSpec((SUBLANES, LRU_WIDTH), lambda b, d, i: (0, 0)),
            pl.BlockSpec((1, LRU_WIDTH), lambda b, d, i: (0, 0)),
            pl.BlockSpec((1, LRU_WIDTH, 2 * LRU_WIDTH), dirw),
            pl.BlockSpec((1, 1, 2 * LRU_WIDTH), dirw),
            pl.BlockSpec((1, 1, LRU_WIDTH), dirw),
            pl.BlockSpec((1, 1, 1, LRU_WIDTH), st),
        ],
        out_specs=[
            pl.BlockSpec((1, LRU_CHUNK, LRU_WIDTH), lambda b, d, i: (d, blk(b, d, i), 0)),
            pl.BlockSpec((1, 1, 1, LRU_WIDTH), st),
        ],
        out_shape=[
            jax.ShapeDtypeStruct((2, B * T, LRU_WIDTH), F32),
            jax.ShapeDtypeStruct((B, 2, 1, LRU_WIDTH), F32),
        ],
        scratch_shapes=[pltpu.VMEM((SUBLANES, LRU_WIDTH), F32)]
        + [pltpu.VMEM((LRU_WIDTH // LANES, LRU_CHUNK, LANES), F32) for _ in range(3)],
        compiler_params=_cparams("parallel", "parallel", "arbitrary"),
        name="lru_scan",
    )(lru_arr, lru_arr, lru_arr, cw, cb, wg, bg, lam, h0)


def _out_kernel(h_ref, attn_ref, go_ref, gr_ref, gg_ref, lh_ref, lg_ref, w_ref, g1_ref, a2_ref, s2_ref, wr_ref, br_ref,
                vt_ref, lt_ref, hn_ref, v_ref, lgt_ref, *, n_main):
    i = pl.program_id(0)

    @pl.when(i < n_main)
    def _():
        _out_body(h_ref, attn_ref, go_ref, gr_ref, gg_ref, lh_ref, lg_ref, w_ref, g1_ref, a2_ref, s2_ref, wr_ref, br_ref,
                  hn_ref, v_ref, lgt_ref)

    @pl.when(i >= n_main)
    def _():
        v_ref[...] = vt_ref[...]
        lgt_ref[...] = lt_ref[...]


def _out_body(h_ref, attn_ref, go_ref, gr_ref, gg_ref, lh_ref, lg_ref, w_ref, g1_ref, a2_ref, s2_ref, wr_ref, br_ref,
              hn_ref, v_ref, lgt_ref):
    o = go_ref[0, 0] + go_ref[1, 0]
    gr = gr_ref[...]
    gate = gr * _sigmoid(gr)
    parts = []
    for hh in range(GLA_HEADS):
        oh = o[:, hh * GLA_DV:(hh + 1) * GLA_DV]
        y = (oh * lax.rsqrt(jnp.mean(oh * oh, axis=-1, keepdims=True) + RMS_EPS)) * gg_ref[...]
        parts.append(y * gate[:, hh * GLA_DV:(hh + 1) * GLA_DV])
    gla = jnp.concatenate(parts, axis=1).astype(BF16)
    lg = lg_ref[...]
    gelu = lg * (0.5 * (1.0 + jnp.tanh(np.sqrt(2.0 / np.pi).astype(np.float32) * (lg + 0.044715 * (lg * lg * lg)))))
    lru = ((lh_ref[0] + lh_ref[1]) * gelu).astype(BF16)
    y = jnp.dot(attn_ref[...], w_ref[0:ATTN_Q], preferred_element_type=F32)
    y = y + jnp.dot(gla, w_ref[ATTN_Q:ATTN_Q + GLA_V], preferred_element_type=F32)
    y = y + jnp.dot(lru, w_ref[ATTN_Q + GLA_V:D_MIX], preferred_element_type=F32)
    hn = h_ref[...] + g1_ref[0] * y
    hn_ref[...] = hn
    v = (hn * lax.rsqrt(jnp.mean(hn * hn, axis=-1, keepdims=True) + RMS_EPS)) * a2_ref[0] + s2_ref[0]
    _store_token_tiles(v_ref, _pack_bf16_pairs(v))
    v_hi = v.astype(BF16)
    v_lo = (v - v_hi.astype(F32)).astype(BF16)
    t = jnp.dot(v_hi, wr_ref[...], preferred_element_type=F32)
    t_lo = jnp.dot(v_lo, wr_ref[:, 0:LANES], preferred_element_type=F32)
    lgt_ref[...] = (t[:, :LANES] + t[:, LANES:]) + t_lo + br_ref[...]


def _mixer_out(h, attn, go, gla_arr, gg, lh, lru_arr, w_out, g1, a2, s2, wr, br, seq_len, tail):
    rows, D = h.shape
    tm = _tile(seq_len, 256)
    per = seq_len // tm
    n_main = rows // tm
    tw = tm * (D // 2 // LANES)
    if tail is None:
        tail = (jnp.zeros((tw, LANES), I32), jnp.zeros((tm, LANES), F32))
        n_tail = 0
    else:
        assert tail[1].shape[0] % tm == 0
        n_tail = tail[1].shape[0] // tm
    n_tok = rows + n_tail * tm
    main = lambda i: jnp.minimum(i, n_main - 1)
    row = lambda i: (main(i), 0)
    row3 = lambda i: (0, main(i), 0)
    bat = lambda i: (main(i) // per, 0, 0)
    const = lambda i: (0, 0)
    tok = lambda i: (i, 0)
    trow = lambda i: (jnp.maximum(i - n_main, 0), 0)
    in_specs = [
        pl.BlockSpec((tm, D), row),
        pl.BlockSpec((tm, ATTN_Q), row),
        pl.BlockSpec((2, 1, tm, GLA_V), lambda i: (0, main(i) // per, main(i) % per, 0)),
        pl.BlockSpec((tm, GLA_V), lambda i: (main(i), 2)),
        pl.BlockSpec((1, GLA_DV), const),
        pl.BlockSpec((2, tm, LRU_WIDTH), row3),
        pl.BlockSpec((tm, LRU_WIDTH), lambda i: (main(i), 1)),
        pl.BlockSpec((D_MIX, D), const, pipeline_mode=pl.Buffered(1)),
        pl.BlockSpec((1, 1, D), bat),
        pl.BlockSpec((1, 1, D), bat),
        pl.BlockSpec((1, 1, D), bat),
        pl.BlockSpec((D, 2 * LANES), const),
        pl.BlockSpec((1, LANES), const),
        pl.BlockSpec((tw, LANES), trow),
        pl.BlockSpec((tm, LANES), trow),
    ]
    args = [h, attn, go, gla_arr, gg, lh, lru_arr, w_out, g1, a2, s2, wr, br, tail[0], tail[1]]
    return pl.pallas_call(
        functools.partial(_out_kernel, n_main=n_main),
        grid=(n_main + n_tail,),
        in_specs=in_specs,
        out_specs=[
            pl.BlockSpec((tm, D), row),
            pl.BlockSpec((tw, LANES), tok),
            pl.BlockSpec((tm, LANES), tok),
        ],
        out_shape=[
            jax.ShapeDtypeStruct((rows, D), F32),
            jax.ShapeDtypeStruct((n_tok * (tw // tm), LANES), I32),
            jax.ShapeDtypeStruct((n_tok, LANES), F32),
        ],
        compiler_params=_cparams("arbitrary"),
        name="mixer_out",
    )(*args)


def _route_kernel(lg_ref, tri_ref, oi_ref, of_ref, cnt_ref, carry):
    i = pl.program_id(0)

    @pl.when(i == 0)
    def _():
        carry[...] = jnp.zeros_like(carry)

    lg = lg_ref[...]
    col = lax.broadcasted_iota(I32, lg.shape, 1)
    colf = col.astype(F32)
    big = float(LANES)
    is_g = col < N_GROUPS
    gm = jnp.max(jnp.where(is_g, lg, -jnp.inf), axis=-1, keepdims=True)
    eg = jnp.where(is_g, jnp.exp(lg - gm), 0.0)
    pg = eg / jnp.sum(eg, axis=-1, keepdims=True)
    p_grp = jnp.max(pg, axis=-1, keepdims=True)
    grp = jnp.min(jnp.where(is_g & (pg == p_grp), colf, big), axis=-1, keepdims=True).astype(I32)

    sel = (col >= N_GROUPS) & (col < N_GROUPS + N_EXPERTS) & (((col - N_GROUPS) >> 3) == grp)
    em = jnp.max(jnp.where(sel, lg, -jnp.inf), axis=-1, keepdims=True)
    ee = jnp.where(sel, jnp.exp(lg - em), 0.0)
    pe = ee / jnp.sum(ee, axis=-1, keepdims=True)
    p1 = jnp.max(jnp.where(sel, pe, -1.0), axis=-1, keepdims=True)
    c1 = jnp.min(jnp.where(sel & (pe == p1), colf, big), axis=-1, keepdims=True).astype(I32)
    rest = sel & (col != c1)
    p2 = jnp.max(jnp.where(rest, pe, -1.0), axis=-1, keepdims=True)
    c2 = jnp.min(jnp.where(rest & (pe == p2), colf, big), axis=-1, keepdims=True).astype(I32)
    e1 = c1 - N_GROUPS
    e2 = c2 - N_GROUPS
    den = p1 + p2
    g1 = p_grp * (p1 / den)
    g2 = p_grp * (p2 / den)

    hit1 = col == e1
    hit2 = col == e2
    oh = jnp.where(hit1 | hit2, 1.0, 0.0)
    before = jnp.dot(tri_ref[...], oh.astype(BF16), preferred_element_type=F32) + carry[0:1]
    r1 = jnp.sum(jnp.where(hit1, before, 0.0), axis=-1, keepdims=True)
    r2 = jnp.sum(jnp.where(hit2, before, 0.0), axis=-1, keepdims=True)
    new = carry[0:1] + jnp.sum(oh, axis=0, keepdims=True)
    carry[...] = jnp.broadcast_to(new, carry.shape)
    cnt_ref[...] = jnp.broadcast_to(new, cnt_ref.shape)

    packed = jnp.where(col == 0, e1.astype(F32), jnp.where(col == 1, e2.astype(F32),
                                                           jnp.where(col == 2, r1, jnp.where(col == 3, r2, 0.0))))
    oi_ref[...] = packed.T[0:SUBLANES].astype(I32)
    of_ref[...] = jnp.where(col == 0, g1, jnp.where(col == 1, g2, 0.0))


def _route(logits):
    n_tok = logits.shape[0]
    tm = _tile(n_tok, 1024)
    row = lambda i: (i, 0)
    tri = jnp.tril(jnp.ones((tm, tm), BF16), -1)
    return pl.pallas_call(
        _route_kernel,
        grid=(n_tok // tm,),
        in_specs=[pl.BlockSpec((tm, LANES), row), pl.BlockSpec((tm, tm), lambda i: (0, 0))],
        out_specs=[
            pl.BlockSpec((SUBLANES, tm), lambda i: (0, i)),
            pl.BlockSpec((tm, LANES), row),
            pl.BlockSpec((SUBLANES, LANES), lambda i: (0, 0)),
        ],
        out_shape=[
            jax.ShapeDtypeStruct((SUBLANES, n_tok), I32),
            jax.ShapeDtypeStruct((n_tok, LANES), F32),
            jax.ShapeDtypeStruct((SUBLANES, LANES), F32),
        ],
        scratch_shapes=[pltpu.VMEM((SUBLANES, LANES), F32)],
        compiler_params=_cparams("arbitrary"),
        name="route",
    )(logits, tri)


def _dot_casting(x, stage, dst, rows, cols):
    n = rows.stop - rows.start
    step = _tile(n, 256)
    acc = None
    for c in range(n // step):
        r = slice(rows.start + c * step, rows.start + (c + 1) * step)
        w = stage[r, cols].astype(BF16)
        dst[r, cols] = w
        part = jnp.dot(x[:, c * step:(c + 1) * step], w, preferred_element_type=F32)
        acc = part if acc is None else acc + part
    return acc


def _moe_ffn_kernel(be_ref, nu_ref, nx_ref, src_ref, srcn_ref, v_hbm, w1_hbm, w3_hbm, w2_hbm, o_ref,
                    xbuf, st1, st3, st2, w1s, w3s, w2s, sem_x, sem_w, *, layer, sub):
    i = pl.program_id(0)
    nu = nu_ref[0]
    rows = ROW_BLOCK * sub

    def gather(idx_ref, slot):
        for r in range(ROW_BLOCK):
            s = pl.multiple_of(idx_ref[0, 0, r] * sub, sub)
            pltpu.make_async_copy(v_hbm.at[pl.ds(s, sub)], xbuf.at[slot, pl.ds(r * sub, sub)], sem_x.at[slot]).start()

    def wait_rows(slot):
        pltpu.make_async_copy(v_hbm.at[pl.ds(0, rows)], xbuf.at[slot], sem_x.at[slot]).wait()

    def weight_copies(e):
        return (pltpu.make_async_copy(w1_hbm.at[layer, e], st1, sem_w.at[0]),
                pltpu.make_async_copy(w3_hbm.at[layer, e], st3, sem_w.at[1]),
                pltpu.make_async_copy(w2_hbm.at[layer, e], st2, sem_w.at[2]))

    @pl.when(i < nu)
    def _():
        e = be_ref[i]
        slot = i & 1

        @pl.when(i == 0)
        def _():
            for cp in weight_copies(e):
                cp.start(priority=1)
            gather(src_ref, 0)

        wait_rows(slot)
        gather(srcn_ref, 1 - slot)
        xb = _unpack_bf16_pairs(_load_token_tiles(xbuf.at[slot], ROW_BLOCK)).astype(BF16)
        first = (i == 0) | (e != be_ref[jnp.maximum(i - 1, 0)])

        d_model, d_ff = w1s.shape
        n_ff = 1
        ffs = [slice(c * d_ff // n_ff, (c + 1) * d_ff // n_ff) for c in range(n_ff)]
        every = slice(0, d_model)

        def mlp(up1, up3, down):
            y = None
            for ff in ffs:
                a = up1(xb, ff)
                b = up3(xb, ff)
                part = down(((a * _sigmoid(a)) * b).astype(BF16), ff)
                y = part if y is None else y + part
            _store_token_tiles(o_ref, _pack_bf16_pairs(y))

        @pl.when(first)
        def _():
            for cp in weight_copies(e):
                cp.wait()
            mlp(lambda x, ff: _dot_casting(x, st1, w1s, every, ff), lambda x, ff: _dot_casting(x, st3, w3s, every, ff),
                lambda h, ff: _dot_casting(h, st2, w2s, ff, every))

            @pl.when(nx_ref[i] >= 0)
            def _():
                for cp in weight_copies(nx_ref[i]):
                    cp.start(priority=1)

        @pl.when(jnp.logical_not(first))
        def _():
            mlp(lambda x, ff: jnp.dot(x, w1s[:, ff], preferred_element_type=F32),
                lambda x, ff: jnp.dot(x, w3s[:, ff], preferred_element_type=F32),
                lambda h, ff: jnp.dot(h, w2s[ff, :], preferred_element_type=F32))

        @pl.when(i == nu - 1)
        def _():
            wait_rows(1 - slot)

    @pl.when(i >= nu)
    def _():
        o_ref[...] = jnp.zeros_like(o_ref)


def _moe_experts(v_tiles, src, blk_e, n_used, nxt_e, w1, w3, w2, layer):
    D, FF = w1.shape[-2:]
    sub = D // 2 // LANES
    P = src.shape[0]
    nb = P // ROW_BLOCK
    tw = ROW_BLOCK * sub
    cur = lambda i, be, nu, nx: (jnp.minimum(i, nu[0] - 1), 0, 0)
    nxt = lambda i, be, nu, nx: (jnp.minimum(i + 1, nu[0] - 1), 0, 0)
    src3 = src.reshape(nb, 1, ROW_BLOCK)
    hbm = pl.BlockSpec(memory_space=pl.ANY)
    return pl.pallas_call(
        functools.partial(_moe_ffn_kernel, layer=layer, sub=sub),
        grid_spec=pltpu.PrefetchScalarGridSpec(
            num_scalar_prefetch=3,
            grid=(nb,),
            in_specs=[
                pl.BlockSpec((1, 1, ROW_BLOCK), cur, memory_space=pltpu.SMEM),
                pl.BlockSpec((1, 1, ROW_BLOCK), nxt, memory_space=pltpu.SMEM),
                hbm, hbm, hbm, hbm,
            ],
            out_specs=pl.BlockSpec((tw, LANES), lambda i, be, nu, nx: (i, 0)),
            scratch_shapes=[
                pltpu.VMEM((2, tw, LANES), I32),
                pltpu.VMEM((D, FF), F32), pltpu.VMEM((D, FF), F32), pltpu.VMEM((FF, D), F32),
                pltpu.VMEM((D, FF), BF16), pltpu.VMEM((D, FF), BF16), pltpu.VMEM((FF, D), BF16),
                pltpu.SemaphoreType.DMA((2,)), pltpu.SemaphoreType.DMA((3,)),
            ],
        ),
        out_shape=jax.ShapeDtypeStruct((P * sub, LANES), I32),
        compiler_params=_cparams("arbitrary"),
        name="moe_ffn",
    )(blk_e, n_used, nxt_e, src3, src3, v_tiles, w1, w3, w2)


def _combine_kernel(dest_ref, destn_ref, h_ref, gate_ref, g2_ref, fg_ref, ys_ref, o_ref, buf, sem, *, final, nsteps):
    i = pl.program_id(0)
    slot = i & 1
    tm = h_ref.shape[0]
    rows = buf.shape[2]
    sub = rows // tm

    def gather(idx_ref, sl):
        for j in range(tm * TOP_K):
            src = pl.multiple_of(idx_ref[j % TOP_K, 0, 0, j // TOP_K] * sub, sub)
            cp = pltpu.make_async_copy(ys_ref.at[pl.ds(src, sub)], buf.at[sl, j % TOP_K, pl.ds((j // TOP_K) * sub, sub)],
                                       sem.at[sl])
            cp.start(priority=j % 2)

    def wait_rows(sl):
        for kk in range(TOP_K):
            pltpu.make_async_copy(ys_ref.at[pl.ds(0, rows)], buf.at[sl, kk], sem.at[sl]).wait()

    @pl.when(i == 0)
    def _():
        gather(dest_ref, 0)

    gather(destn_ref, 1 - slot)
    wait_rows(slot)
    gate = gate_ref[...]
    f = (_unpack_bf16_pairs(_load_token_tiles(buf.at[slot, 0], tm)) * gate[:, 0:1]
         + _unpack_bf16_pairs(_load_token_tiles(buf.at[slot, 1], tm)) * gate[:, 1:2])

    @pl.when(i == nsteps - 1)
    def _():
        wait_rows(1 - slot)

    hn = h_ref[...] + g2_ref[0] * f
    if final:
        hn = (hn * lax.rsqrt(jnp.mean(hn * hn, axis=-1, keepdims=True) + RMS_EPS)) * fg_ref[...]
    o_ref[...] = hn


def _combine(h, dest, gates, g2, fg, ys, seq_len, tok_off, final):
    rows, D = h.shape
    tm = _tile(seq_len, 256)
    per = seq_len // tm
    off = tok_off // tm
    nsteps = rows // tm
    dest3 = dest.reshape(TOP_K, dest.shape[1] // tm, 1, tm)
    return pl.pallas_call(
        functools.partial(_combine_kernel, final=final, nsteps=nsteps),
        grid=(nsteps,),
        in_specs=[
            pl.BlockSpec((TOP_K, 1, 1, tm), lambda i: (0, i + off, 0, 0), memory_space=pltpu.SMEM),
            pl.BlockSpec((TOP_K, 1, 1, tm), lambda i: (0, jnp.minimum(i + 1, nsteps - 1) + off, 0, 0),
                         memory_space=pltpu.SMEM),
            pl.BlockSpec((tm, D), lambda i: (i, 0)),
            pl.BlockSpec((tm, LANES), lambda i: (i + off, 0)),
            pl.BlockSpec((1, 1, D), lambda i: (i // per, 0, 0)),
            pl.BlockSpec((1, D), lambda i: (0, 0)),
            pl.BlockSpec(memory_space=pl.ANY),
        ],
        out_specs=pl.BlockSpec((tm, D), lambda i: (i, 0)),
        out_shape=jax.ShapeDtypeStruct((rows, D), F32),
        scratch_shapes=[pltpu.VMEM((2, TOP_K, tm * (D // 2 // LANES), LANES), I32), pltpu.SemaphoreType.DMA((2,))],
        compiler_params=_cparams("arbitrary"),
        name="combine",
    )(dest3, dest3, h, gates, g2, fg, ys)


def _rope_tables(S):
    rows = S // GRID_W
    row = jnp.repeat(jnp.arange(rows, dtype=F32), GRID_W)
    col = jnp.tile(jnp.arange(GRID_W, dtype=F32), rows)
    n_freq = HEAD_DIM // 4
    inv = ROPE_THETA ** (-jnp.arange(n_freq, dtype=F32) / n_freq)
    ang = jnp.concatenate([row[:, None] * inv, col[:, None] * inv], axis=-1)
    cos = jnp.repeat(jnp.cos(ang), 2, axis=1)
    sin = jnp.repeat(jnp.sin(ang), 2, axis=1)
    even = (jnp.arange(HEAD_DIM) % 2) == 0
    return cos, jnp.where(even, -sin, 0.0), jnp.where(even, 0.0, sin)


def _block_diag(w):
    n, c, _ = w.shape
    eye = jnp.eye(n, dtype=w.dtype)
    return (eye[:, None, :, None] * w[:, :, None, :]).reshape(n * c, n * c)


def _slot_plan(ids, counts_row, n_tok):
    e = ids[0:TOP_K]
    rank = ids[TOP_K:2 * TOP_K]
    counts = counts_row[:N_EXPERTS].astype(I32)
    padded = (counts + ROW_BLOCK - 1) // ROW_BLOCK * ROW_BLOCK
    pad_end = jnp.cumsum(padded)
    pad_start = pad_end - padded
    hit = e[None] == jnp.arange(N_EXPERTS, dtype=I32)[:, None, None]
    dest = jnp.sum(jnp.where(hit, pad_start[:, None, None], 0), axis=0) + rank
    n_blocks = -(-(n_tok * TOP_K + N_EXPERTS * (ROW_BLOCK - 1)) // ROW_BLOCK)
    starts = jnp.arange(n_blocks, dtype=I32) * ROW_BLOCK
    blk_e = jnp.minimum(jnp.sum(pad_end[None, :] <= starts[:, None], axis=1), N_EXPERTS - 1).astype(I32)
    n_used = (pad_end[-1] // ROW_BLOCK).astype(I32).reshape(1)
    run_next = pad_end[blk_e] // ROW_BLOCK
    nxt_e = jnp.where(run_next < n_used[0], blk_e[jnp.minimum(run_next, n_blocks - 1)], -1).astype(I32)
    tok = jnp.broadcast_to(jnp.arange(n_tok, dtype=I32), (TOP_K, n_tok))
    src = jnp.zeros((n_blocks * ROW_BLOCK,), I32).at[dest.reshape(-1)].set(tok.reshape(-1), unique_indices=True)
    return dest, src, blk_e, n_used, nxt_e


def kernel(x, c, ctx, c_ctx, ada_w, ada_b, norm_mix_g, norm_ffn_g, w_in, attn_sink, gla_gate_w2, gla_gate_b, gla_norm_g, lru_conv_w, lru_conv_b, lru_wa, lru_ba, lru_wx, lru_bx, lru_lambda, w_out, router_g_w, router_g_b, router_e_w, router_e_b, moe_w1, moe_w3, moe_w2, final_norm_g):
    B, S, D = x.shape
    C = ctx.shape[1]
    L = ada_w.shape[0]
    assert S % max(ATTN_BLOCK, GLA_CHUNK, LRU_CHUNK, GRID_W) == 0 and C % max(ATTN_BLOCK, GLA_CHUNK, LRU_CHUNK) == 0
    assert B + 1 <= SUBLANES and D % (2 * LANES) == 0

    cond = jnp.concatenate([c, c_ctx[None], jnp.zeros((SUBLANES - B - 1, D), F32)], axis=0)
    mods = _ada_mods(cond, ada_w, ada_b).reshape(L, SUBLANES, 6, D)

    cos_l, se_l, so_l = _rope_tables(S)
    cos_c = jnp.ones((C, HEAD_DIM), F32)
    zero_c = jnp.zeros((C, HEAD_DIM), F32)
    e2_np, m2_np = _gla_constants()
    gla_e = jnp.asarray(e2_np, BF16)
    gla_m = jnp.asarray(m2_np, F32)

    h_lat = x.reshape(B * S, D)
    h_ctx = ctx.reshape(B * C, D)
    out = None
    for l in range(L):
        last = l == L - 1
        ml = mods[l, :B]
        mc = jnp.broadcast_to(mods[l, B][None], (B, 6, D))
        per_b = lambda m, j: m[:, j][:, None, :]
        a1_l, a1_c = [(1.0 + per_b(m, 1)) * norm_mix_g[l] for m in (ml, mc)]
        a2_l, a2_c = [(1.0 + per_b(m, 4)) * norm_ffn_g[l] for m in (ml, mc)]

        w = w_in[l]
        c_gz = ATTN_Q + 2 * ATTN_KV + 2 * GLA_QK + 2 * GLA_V
        w_packed = jnp.concatenate(
            [w[:, :c_gz], w[:, c_gz:c_gz + 2 * GLA_RANK], jnp.zeros((D, GZ_W - 2 * GLA_RANK), F32), w[:, c_gz + 2 * GLA_RANK:]],
            axis=1).astype(BF16)
        qkv_l, gla_l, gz_l, lru_l = _in_proj(h_lat, a1_l, per_b(ml, 0), cos_l, se_l, so_l, w_packed, S)
        qkv_c, gla_c, gz_c, lru_c = _in_proj(h_ctx, a1_c, per_b(mc, 0), cos_c, zero_c, zero_c, w_packed, C)

        sink_b = jnp.broadcast_to(attn_sink[l][:, None], (N_Q_HEADS, LANES)).astype(F32)
        attn_l = _attn_latent(qkv_l, qkv_c, sink_b, B, S, C)

        w2p = jnp.zeros((2, GZ_W, GLA_QK), F32)
        w2p = w2p.at[0, :GLA_RANK].set(gla_gate_w2[l, 0]).at[1, GLA_RANK:2 * GLA_RANK].set(gla_gate_w2[l, 1]).astype(BF16)
        gbias = gla_gate_b[l].reshape(2, 1, GLA_QK)
        s_zero = jnp.zeros((B, 2, GLA_QK, GLA_DV), F32)
        go_c, s_ctx = _gla(gla_c, gz_c, w2p, gbias, gla_e, gla_m, s_zero, B, C)
        go_l, _ = _gla(gla_l, gz_l, w2p, gbias, gla_e, gla_m, s_ctx, B, S)

        cw = jnp.concatenate([lru_conv_w[l], jnp.zeros((SUBLANES - CONV_W, LRU_WIDTH), F32)], axis=0)
        cb = lru_conv_b[l].reshape(1, LRU_WIDTH)
        wg = jnp.stack([jnp.concatenate([_block_diag(lru_wa[l, d]), _block_diag(lru_wx[l, d])], axis=1) for d in range(2)]).astype(BF16)
        bg = jnp.concatenate([lru_ba[l], lru_bx[l]], axis=1).reshape(2, 1, 2 * LRU_WIDTH)
        lam = lru_lambda[l].reshape(2, 1, LRU_WIDTH)
        h_zero = jnp.zeros((B, 2, 1, LRU_WIDTH), F32)
        lh_c, hs_ctx = _lru(lru_c, cw, cb, wg, bg, lam, h_zero, B, C)
        lh_l, _ = _lru(lru_l, cw, cb, wg, bg, lam, hs_ctx, B, S)

        wo = w_out[l].astype(BF16)
        gg = gla_norm_g[l].reshape(1, GLA_DV)
        wr = jnp.concatenate([router_g_w[l], router_e_w[l], jnp.zeros((D, LANES - N_GROUPS - N_EXPERTS), F32)], axis=1)
        wr_hi = wr.astype(BF16)
        wr = jnp.concatenate([wr_hi, (wr - wr_hi.astype(F32)).astype(BF16)], axis=1)
        br =jnp.concatenate([router_g_b[l], router_e_b[l], jnp.zeros((LANES - N_GROUPS - N_EXPERTS,), F32)]).reshape(1, LANES)
        n_tok = B * S if last else B * (S + C)
        tail = None
        if not last:
            attn_c = _attn_context(qkv_c, sink_b, B, C)
            h_ctx, v_c, lgt_c = _mixer_out(h_ctx, attn_c, go_c, gla_c, gg, lh_c, lru_c, wo, per_b(mc, 2), a2_c, per_b(mc, 3),
                                           wr, br, C, None)
            tail = (v_c, lgt_c)
        h_lat, v_all, lgt_all = _mixer_out(h_lat, attn_l, go_l, gla_l, gg, lh_l, lru_l, wo, per_b(ml, 2), a2_l, per_b(ml, 3),
                                           wr, br, S, tail)

        ids, gates, counts = _route(lgt_all)
        dest, src, blk_e, n_used, nxt_e = _slot_plan(ids, counts[0], n_tok)
        ys = _moe_experts(v_all, src, blk_e, n_used, nxt_e, moe_w1, moe_w3, moe_w2, l)
        fg = final_norm_g.reshape(1, D)
        h_lat = _combine(h_lat, dest, gates, per_b(ml, 5), fg, ys, S, 0, last)
        if not last:
            h_ctx = _combine(h_ctx, dest, gates, per_b(mc, 5), fg, ys, C, B * S, False)
        out = h_lat
    return out.reshape(B, S, D)
```

```python
import functools

import numpy as np
import jax
import jax.numpy as jnp
from jax import lax
from jax.experimental import pallas as pl
from jax.experimental.pallas import tpu as pltpu

F32 = jnp.float32
BF16 = jnp.bfloat16
I32 = jnp.int32

GRID_W = 64
RMS_EPS = 1e-6
N_Q_HEADS = 8
N_KV_HEADS = 2
HEAD_DIM = 128
WINDOW = 128
ATTN_BLOCK = 128
ROPE_THETA = 10000.0
GLA_HEADS = 4
GLA_DK = 64
GLA_DV = 128
GLA_RANK = 16
GLA_TAU = 16.0
LRU_WIDTH = 512
LRU_BLOCKS = 8
LRU_C = 8.0
CONV_W = 4
N_GROUPS = 4
EXPERTS_PER_GROUP = 8
N_EXPERTS = 32
TOP_K = 2
ATTN_Q = N_Q_HEADS * HEAD_DIM
ATTN_KV = N_KV_HEADS * HEAD_DIM
GLA_QK = GLA_HEADS * GLA_DK
GLA_V = GLA_HEADS * GLA_DV
D_MIX = ATTN_Q + GLA_V + LRU_WIDTH

LANES = 128
SUBLANES = 8
VMEM_LIMIT_BYTES = 56 * 1024 * 1024

QKV_W = ATTN_Q + 2 * ATTN_KV
GLA_W = 2 * GLA_QK + 2 * GLA_V
GZ_W = LANES
LRU_W = 2 * LRU_WIDTH
COL_GLA = QKV_W
COL_GZ = COL_GLA + GLA_W
COL_LRU = COL_GZ + GZ_W
W_IN_PACKED = COL_LRU + LRU_W

GLA_CHUNK = 128
GLA_LEVELS = 7
LRU_CHUNK = 256
ROW_BLOCK = 256


def _cparams(*sem):
    return pltpu.CompilerParams(dimension_semantics=sem, vmem_limit_bytes=VMEM_LIMIT_BYTES)


def _tile(n, pref):
    t = min(n, pref)
    while n % t:
        t -= SUBLANES
    return t


def _sigmoid(x):
    return 0.5 * (jnp.tanh(0.5 * x) + 1.0)


def _softplus(x):
    return jnp.maximum(x, 0.0) + jnp.log1p(jnp.exp(-jnp.abs(x)))


def _pack_bf16_pairs(x):
    n = x.shape[1] // 2
    lo = lax.bitcast_convert_type(x[:, :n].astype(BF16).astype(F32), I32)
    hi = lax.bitcast_convert_type(x[:, n:].astype(BF16).astype(F32), I32)
    return hi | lax.shift_right_logical(lo, jnp.full(lo.shape, 16, I32))


def _unpack_bf16_pairs(p):
    lo = lax.bitcast_convert_type(lax.shift_left(p, jnp.full(p.shape, 16, I32)), F32)
    hi = lax.bitcast_convert_type(p & jnp.int32(-65536), F32)
    return jnp.concatenate([lo, hi], axis=1)


def _store_token_tiles(ref, packed):
    m = packed.shape[0]
    sub = packed.shape[1] // LANES
    for s in range(sub):
        ref[pl.ds(s, m, stride=sub), :] = packed[:, s * LANES:(s + 1) * LANES]


def _load_token_tiles(ref, m):
    sub = ref.shape[0] // m
    return jnp.concatenate([ref[pl.ds(s, m, stride=sub), :] for s in range(sub)], axis=1)


def _ada_kernel(c_ref, w_ref, b_ref, o_ref):
    c = c_ref[...]
    s = c * _sigmoid(c)
    o_ref[0] = jnp.dot(s.astype(BF16), w_ref[0].astype(BF16), preferred_element_type=F32) + b_ref[0]


def _ada_mods(cond, ada_w, ada_b):
    L, D, D6 = ada_w.shape
    tn = _tile(D6, 1536)
    while tn % LANES:
        tn -= SUBLANES
    return pl.pallas_call(
        _ada_kernel,
        grid=(L, D6 // tn),
        in_specs=[
            pl.BlockSpec((SUBLANES, D), lambda l, j: (0, 0)),
            pl.BlockSpec((1, D, tn), lambda l, j: (l, 0, j)),
            pl.BlockSpec((1, 1, tn), lambda l, j: (l, 0, j)),
        ],
        out_specs=pl.BlockSpec((1, SUBLANES, tn), lambda l, j: (l, 0, j)),
        out_shape=jax.ShapeDtypeStruct((L, SUBLANES, D6), F32),
        compiler_params=_cparams("parallel", "parallel"),
        name="ada_mods",
    )(cond, ada_w, ada_b.reshape(L, 1, D6))


def _in_kernel(x_ref, a_ref, s_ref, cos_ref, se_ref, so_ref, w_ref, qkv_ref, gla_ref, gz_ref, lru_ref):
    x = x_ref[...]
    ms = jnp.mean(x * x, axis=-1, keepdims=True)
    u = (x * lax.rsqrt(ms + RMS_EPS)) * a_ref[0] + s_ref[0]
    ub = u.astype(BF16)
    cos, se, so = cos_ref[...], se_ref[...], so_ref[...]
    n_rot = N_Q_HEADS + N_KV_HEADS
    for jp in range(n_rot // 2):
        z2 = jnp.dot(ub, w_ref[:, 2 * jp * HEAD_DIM:(2 * jp + 2) * HEAD_DIM], preferred_element_type=F32)
        for j in (2 * jp, 2 * jp + 1):
            zh = z2[:, (j - 2 * jp) * HEAD_DIM:(j - 2 * jp + 1) * HEAD_DIM]
            rot = zh * cos + pltpu.roll(zh, HEAD_DIM - 1, 1) * se + pltpu.roll(zh, 1, 1) * so
            qkv_ref[:, j * HEAD_DIM:(j + 1) * HEAD_DIM] = rot.astype(BF16)
    c0 = n_rot * HEAD_DIM
    qkv_ref[:, c0:QKV_W] = jnp.dot(ub, w_ref[:, c0:QKV_W], preferred_element_type=F32).astype(BF16)
    gla_ref[...] = jnp.dot(ub, w_ref[:, COL_GLA:COL_GZ], preferred_element_type=F32)
    gz_ref[...] = jnp.dot(ub, w_ref[:, COL_GZ:COL_LRU], preferred_element_type=F32)
    lru_ref[...] = jnp.dot(ub, w_ref[:, COL_LRU:W_IN_PACKED], preferred_element_type=F32)


def _in_proj(h, a, s, cos, se, so, w, seq_len):
    rows, D = h.shape
    tm = _tile(seq_len, 512)
    per = seq_len // tm
    row = lambda i: (i, 0)
    bat = lambda i: (i // per, 0, 0)
    tab = lambda i: (i % per, 0)
    return pl.pallas_call(
        _in_kernel,
        grid=(rows // tm,),
        in_specs=[
            pl.BlockSpec((tm, D), row),
            pl.BlockSpec((1, 1, D), bat),
            pl.BlockSpec((1, 1, D), bat),
            pl.BlockSpec((tm, HEAD_DIM), tab),
            pl.BlockSpec((tm, HEAD_DIM), tab),
            pl.BlockSpec((tm, HEAD_DIM), tab),
            pl.BlockSpec((D, W_IN_PACKED), lambda i: (0, 0), pipeline_mode=pl.Buffered(1)),
        ],
        out_specs=[
            pl.BlockSpec((tm, QKV_W), row),
            pl.BlockSpec((tm, GLA_W), row),
            pl.BlockSpec((tm, GZ_W), row),
            pl.BlockSpec((tm, LRU_W), row),
        ],
        out_shape=[
            jax.ShapeDtypeStruct((rows, QKV_W), BF16),
            jax.ShapeDtypeStruct((rows, GLA_W), F32),
            jax.ShapeDtypeStruct((rows, GZ_W), F32),
            jax.ShapeDtypeStruct((rows, LRU_W), F32),
        ],
        compiler_params=_cparams("parallel"),
        name="in_proj",
    )(h, a, s, cos, se, so, w)


def _attn_heads(q, k_loc, v_loc, valid, k_ctx, v_ctx, sink_ref, o_ref):
    scale = HEAD_DIM ** -0.5
    group = N_Q_HEADS // N_KV_HEADS
    nq = q.shape[0]
    nt = (((1,), (1,)), ((), ()))
    if valid is not None:
        valid = jnp.concatenate([valid] * group, axis=0)
    for hk in range(N_KV_HEADS):
        heads = range(hk * group, (hk + 1) * group)
        qg = jnp.concatenate([q[:, h * HEAD_DIM:(h + 1) * HEAD_DIM] for h in heads], axis=0)
        sink = jnp.concatenate([jnp.broadcast_to(sink_ref[h:h + 1, 0:1], (nq, 1)) for h in heads], axis=0)
        s_ctx = lax.dot_general(qg, k_ctx[hk], nt, preferred_element_type=F32) * scale
        m = jnp.maximum(jnp.max(s_ctx, axis=-1, keepdims=True), sink)
        if k_loc is not None:
            s_loc = lax.dot_general(qg, k_loc[hk], nt, preferred_element_type=F32) * scale
            s_loc = jnp.where(valid, s_loc, -jnp.inf)
            m = jnp.maximum(m, jnp.max(s_loc, axis=-1, keepdims=True))
        ov = jnp.dot(jnp.exp(s_ctx - m).astype(BF16), _with_ones(v_ctx[hk]), preferred_element_type=F32)
        if k_loc is not None:
            ov = ov + jnp.dot(jnp.exp(s_loc - m).astype(BF16), _with_ones(v_loc[hk]), preferred_element_type=F32)
        den = ov[:, HEAD_DIM:HEAD_DIM + 1] + jnp.exp(sink - m)
        o = (ov[:, :HEAD_DIM] * (1.0 / den)).astype(o_ref.dtype)
        for n, h in enumerate(heads):
            o_ref[:, h * HEAD_DIM:(h + 1) * HEAD_DIM] = o[n * nq:(n + 1) * nq]


def _with_ones(v):
    return jnp.concatenate([v, jnp.ones_like(v)], axis=1)


def _split_kv(kv):
    ks = [kv[:, h * HEAD_DIM:(h + 1) * HEAD_DIM] for h in range(N_KV_HEADS)]
    vs = [kv[:, ATTN_KV + h * HEAD_DIM:ATTN_KV + (h + 1) * HEAD_DIM] for h in range(N_KV_HEADS)]
    return ks, vs


def _attn_lat_kernel(q_ref, kvp_ref, kvc_ref, kvn_ref, kvx_ref, sink_ref, o_ref, *, nb):
    n = pl.program_id(1)
    kp, vp = _split_kv(kvp_ref[...])
    kc, vc = _split_kv(kvc_ref[...])
    kn, vn = _split_kv(kvn_ref[...])
    k_ctx, v_ctx = _split_kv(kvx_ref[...])
    k_loc = [jnp.concatenate([kp[h], kc[h], kn[h]], axis=0) for h in range(N_KV_HEADS)]
    v_loc = [jnp.concatenate([vp[h], vc[h], vn[h]], axis=0) for h in range(N_KV_HEADS)]
    i = lax.broadcasted_iota(I32, (ATTN_BLOCK, 3 * ATTN_BLOCK), 0)
    j = lax.broadcasted_iota(I32, (ATTN_BLOCK, 3 * ATTN_BLOCK), 1)
    valid = (j >= i) & (j <= i + 2 * WINDOW)
    valid = valid & ((j >= ATTN_BLOCK) | (n > 0)) & ((j < 2 * ATTN_BLOCK) | (n < nb - 1))
    _attn_heads(q_ref[...], k_loc, v_loc, valid, k_ctx, v_ctx, sink_ref, o_ref)


def _attn_ctx_kernel(q_ref, kvx_ref, sink_ref, o_ref):
    k_ctx, v_ctx = _split_kv(kvx_ref[...])
    _attn_heads(q_ref[...], None, None, None, k_ctx, v_ctx, sink_ref, o_ref)


def _attn_latent(qkv_l, qkv_c, sink_b, B, S, C):
    nb = S // ATTN_BLOCK
    kvw = 2 * ATTN_KV
    kvcol = ATTN_Q // kvw
    return pl.pallas_call(
        functools.partial(_attn_lat_kernel, nb=nb),
        grid=(B, nb),
        in_specs=[
            pl.BlockSpec((ATTN_BLOCK, ATTN_Q), lambda b, n: (b * nb + n, 0)),
            pl.BlockSpec((ATTN_BLOCK, kvw), lambda b, n: (b * nb + jnp.maximum(n - 1, 0), kvcol)),
            pl.BlockSpec((ATTN_BLOCK, kvw), lambda b, n: (b * nb + n, kvcol)),
            pl.BlockSpec((ATTN_BLOCK, kvw), lambda b, n: (b * nb + jnp.minimum(n + 1, nb - 1), kvcol)),
            pl.BlockSpec((C, kvw), lambda b, n: (b, kvcol)),
            pl.BlockSpec((SUBLANES, LANES), lambda b, n: (0, 0)),
        ],
        out_specs=pl.BlockSpec((ATTN_BLOCK, ATTN_Q), lambda b, n: (b * nb + n, 0)),
        out_shape=jax.ShapeDtypeStruct((B * S, ATTN_Q), BF16),
        compiler_params=_cparams("parallel", "parallel"),
        name="attn_latent",
    )(qkv_l, qkv_l, qkv_l, qkv_l, qkv_c, sink_b)


def _attn_context(qkv_c, sink_b, B, C):
    nb = C // ATTN_BLOCK
    kvw = 2 * ATTN_KV
    kvcol = ATTN_Q // kvw
    return pl.pallas_call(
        _attn_ctx_kernel,
        grid=(B, nb),
        in_specs=[
            pl.BlockSpec((ATTN_BLOCK, ATTN_Q), lambda b, n: (b * nb + n, 0)),
            pl.BlockSpec((C, kvw), lambda b, n: (b, kvcol)),
            pl.BlockSpec((SUBLANES, LANES), lambda b, n: (0, 0)),
        ],
        out_specs=pl.BlockSpec((ATTN_BLOCK, ATTN_Q), lambda b, n: (b * nb + n, 0)),
        out_shape=jax.ShapeDtypeStruct((B * C, ATTN_Q), BF16),
        compiler_params=_cparams("parallel", "parallel"),
        name="attn_context",
    )(qkv_c, qkv_c, sink_b)


def _gla_constants():
    Lc = GLA_CHUNK
    e = np.zeros((GLA_LEVELS + 2, Lc, Lc), np.float32)
    msk = np.zeros((GLA_LEVELS + 1, Lc, Lc), np.float32)
    t = np.arange(Lc)
    for l in range(GLA_LEVELS):
        m = 1 << l
        blk = t // (2 * m)
        upper = (t % (2 * m)) >= m
        bnd = blk * 2 * m + m
        r = t[None, :]
        eq = upper[:, None] & (r >= bnd[:, None]) & (r <= t[:, None])
        ek = (~upper)[:, None] & (r > t[:, None]) & (r <= bnd[:, None] - 1)
        e[l] = (eq | ek).astype(np.float32)
        msk[l] = (upper[:, None] & (~upper)[None, :] & (blk[:, None] == blk[None, :])).astype(np.float32)
    e[GLA_LEVELS] = (t[None, :] <= t[:, None]).astype(np.float32)
    e[GLA_LEVELS + 1] = (t[None, :] > t[:, None]).astype(np.float32)
    msk[GLA_LEVELS] = np.eye(Lc, dtype=np.float32)
    e2 = np.stack([e, e[:, ::-1, ::-1]]).reshape(2, (GLA_LEVELS + 2) * Lc, Lc)
    m2 = np.stack([msk, msk[:, ::-1, ::-1]])
    return e2, m2


def _gla_kernel(x_ref, gz_ref, w2_ref, bias_ref, e_ref, m_ref, s0_ref, o_ref, sfin_ref, s_scr, *, nch):
    i = pl.program_id(2)

    @pl.when(i == 0)
    def _():
        s_scr[...] = s0_ref[:, 0]

    for bb in range(x_ref.shape[0]):
        _gla_chunk(x_ref.at[bb], gz_ref.at[bb], w2_ref, bias_ref, e_ref, m_ref, o_ref.at[0, bb], s_scr.at[bb])

    @pl.when(i == nch - 1)
    def _():
        sfin_ref[:, 0] = s_scr[...]


def _gla_chunk(x_ref, gz_ref, w2_ref, bias_ref, e_ref, m_ref, o_ref, s_scr):
    Lc = GLA_CHUNK
    z = jnp.dot(gz_ref[...].astype(BF16), w2_ref[0], preferred_element_type=F32) + bias_ref[0]
    la = (jnp.minimum(z, 0.0) - jnp.log1p(jnp.exp(-jnp.abs(z)))) * (1.0 / GLA_TAU)
    la_hi = la.astype(BF16)
    la_lo = (la - la_hi.astype(F32)).astype(BF16)
    la2 = jnp.concatenate([la_hi, la_lo], axis=1)
    ex = jnp.dot(e_ref[0], la2, preferred_element_type=F32)
    decay = jnp.exp(ex[:, :GLA_QK] + ex[:, GLA_QK:])
    tot = lax.dot_general(la2, jnp.ones((Lc, LANES), BF16), (((0,), (0,)), ((), ())), preferred_element_type=F32)
    a_tot = jnp.exp(tot[:GLA_QK] + tot[GLA_QK:])

    q = x_ref[:, 0:GLA_QK] * (GLA_DK ** -0.5)
    k = x_ref[:, GLA_QK:2 * GLA_QK]
    v = x_ref[:, 2 * GLA_QK:2 * GLA_QK + GLA_V].astype(BF16)
    row_head = lax.broadcasted_iota(I32, (GLA_QK, GLA_DV), 0) >> 6
    first_of_pair = (lax.broadcasted_iota(I32, (Lc, LANES), 1) >> 6) == 0
    nt = (((1,), (1,)), ((), ()))

    att = [jnp.zeros((Lc, Lc), F32) for _ in range(GLA_HEADS)]
    for l in range(GLA_LEVELS + 1):
        if l < GLA_LEVELS:
            dl = decay[l * Lc:(l + 1) * Lc]
            ql = (q * dl).astype(BF16)
            kl = (k * dl).astype(BF16)
        else:
            ql = q.astype(BF16)
            kl = k.astype(BF16)
        ml = m_ref[0, l]
        for pair in range(GLA_HEADS // 2):
            qp = ql[:, pair * LANES:(pair + 1) * LANES]
            kp = kl[:, pair * LANES:(pair + 1) * LANES]
            zero = jnp.zeros_like(kp)
            kk = jnp.concatenate([jnp.where(first_of_pair, kp, zero), jnp.where(first_of_pair, zero, kp)], axis=0)
            sc = lax.dot_general(qp, kk, nt, preferred_element_type=F32)
            att[2 * pair] = att[2 * pair] + ml * sc[:, :Lc]
            att[2 * pair + 1] = att[2 * pair + 1] + ml * sc[:, Lc:]

    s_old = s_scr[...]
    s_b = s_old.astype(BF16)
    zero_s = jnp.zeros_like(s_b)
    s_bd = jnp.concatenate([jnp.where(row_head == h, s_b, zero_s) for h in range(GLA_HEADS)], axis=1)
    q_in = (q * decay[GLA_LEVELS * Lc:(GLA_LEVELS + 1) * Lc]).astype(BF16)
    k_out = (k * decay[(GLA_LEVELS + 1) * Lc:(GLA_LEVELS + 2) * Lc]).astype(BF16)
    o_inter = jnp.dot(q_in, s_bd, preferred_element_type=F32)
    contrib = lax.dot_general(k_out, v, (((0,), (0,)), ((), ())), preferred_element_type=F32)
    s_new = a_tot * s_old
    for h in range(GLA_HEADS):
        cols = slice(h * GLA_DV, (h + 1) * GLA_DV)
        o_ref[:, cols] = jnp.dot(att[h].astype(BF16), v[:, cols], preferred_element_type=F32) + o_inter[:, cols]
        s_new = s_new + jnp.where(row_head == h, contrib[:, cols], 0.0)
    s_scr[...] = s_new


def _gla(gla_arr, gz_arr, w2p, bias, e2, m2, s0, B, T):
    nch = T // GLA_CHUNK
    nseq = 4 if B % 4 == 0 else (2 if B % 2 == 0 else 1)
    chunk = lambda d, i: jnp.where(d == 0, i, nch - 1 - i)
    st = lambda b, d, i: (b, d, 0, 0)
    o, s_fin = pl.pallas_call(
        functools.partial(_gla_kernel, nch=nch),
        grid=(B // nseq, 2, nch),
        in_specs=[
            pl.BlockSpec((nseq, GLA_CHUNK, 2 * GLA_QK + GLA_V), lambda b, d, i: (b, chunk(d, i), 0)),
            pl.BlockSpec((nseq, GLA_CHUNK, GZ_W), lambda b, d, i: (b, chunk(d, i), 0)),
            pl.BlockSpec((1, GZ_W, GLA_QK), lambda b, d, i: (d, 0, 0)),
            pl.BlockSpec((1, 1, GLA_QK), lambda b, d, i: (d, 0, 0)),
            pl.BlockSpec((1, (GLA_LEVELS + 2) * GLA_CHUNK, GLA_CHUNK), lambda b, d, i: (d, 0, 0)),
            pl.BlockSpec((1, GLA_LEVELS + 1, GLA_CHUNK, GLA_CHUNK), lambda b, d, i: (d, 0, 0, 0)),
            pl.BlockSpec((nseq, 1, GLA_QK, GLA_DV), st),
        ],
        out_specs=[
            pl.BlockSpec((1, nseq, GLA_CHUNK, GLA_V), lambda b, d, i: (d, b, chunk(d, i), 0)),
            pl.BlockSpec((nseq, 1, GLA_QK, GLA_DV), st),
        ],
        out_shape=[
            jax.ShapeDtypeStruct((2, B, T, GLA_V), F32),
            jax.ShapeDtypeStruct((B, 2, GLA_QK, GLA_DV), F32),
        ],
        scratch_shapes=[pltpu.VMEM((nseq, GLA_QK, GLA_DV), F32)],
        compiler_params=_cparams("parallel", "parallel", "arbitrary"),
        name="gla_scan",
    )(gla_arr.reshape(B, T, GLA_W), gz_arr.reshape(B, T, GZ_W), w2p, bias, e2, m2, s0)
    return o, s_fin


def _lru_kernel(x_ref, pv_ref, nx_ref, cw_ref, cb_ref, wg_ref, bg_ref, lam_ref, h0_ref, o_ref, hfin_ref, carry, a_s, u_s,
                h_s, *, nblk):
    d = pl.program_id(1)
    i = pl.program_id(2)
    T = LRU_CHUNK
    W = LRU_WIDTH

    @pl.when(i == 0)
    def _():
        carry[...] = jnp.broadcast_to(h0_ref[0, 0], (SUBLANES, W))

    li = jnp.where(d == 0, i, nblk - 1 - i)
    pv = jnp.where(li > 0, pv_ref[...], 0.0)
    nx = jnp.where(li < nblk - 1, nx_ref[...], 0.0)
    xe = jnp.concatenate([pv, x_ref[...], nx], axis=0)
    n_ext = T + 2 * SUBLANES
    win = lambda off: pltpu.roll(xe, n_ext - off, 0)[0:T]
    cw = cw_ref[...]
    xc = cb_ref[...] + win(6) * cw[0:1] + win(7) * cw[1:2] + xe[SUBLANES:SUBLANES + T] * cw[2:3] + win(9) * cw[3:4]

    g = jnp.dot(xc.astype(BF16), wg_ref[0], preferred_element_type=F32) + bg_ref[0]
    r = _sigmoid(g[:, :W])
    gi = _sigmoid(g[:, W:])
    log_a = (-LRU_C * _softplus(-lam_ref[0])) * r
    a = jnp.exp(log_a)
    u = jnp.sqrt(-jnp.tanh(log_a) * (a * a + 1.0)) * (gi * xc)
    panels = W // LANES
    for c in range(panels):
        a_s[c] = a[:, c * LANES:(c + 1) * LANES]
        u_s[c] = u[:, c * LANES:(c + 1) * LANES]

    groups = T // SUBLANES
    slab = lambda ref, j: jnp.concatenate([ref[c, pl.ds(j, groups, stride=SUBLANES), :] for c in range(panels)], axis=1)
    g_row = lax.broadcasted_iota(I32, (groups, W), 0)

    def scan(order, shift, edge):
        us = {order[0]: slab(u_s, order[0])}
        ps = {order[0]: slab(a_s, order[0])}
        for prev, j in zip(order[:-1], order[1:]):
            aj = slab(a_s, j)
            us[j] = slab(u_s, j) + aj * us[prev]
            ps[j] = aj * ps[prev]
        cu, cp = us[order[-1]], ps[order[-1]]
        sh = 1
        while sh < groups:
            ok = (g_row >= sh) if shift == 1 else (g_row < groups - sh)
            amt = sh if shift == 1 else groups - sh
            cu = cu + cp * jnp.where(ok, pltpu.roll(cu, amt, 0), 0.0)
            cp = cp * jnp.where(ok, pltpu.roll(cp, amt, 0), 1.0)
            sh *= 2
        c_out = cu + cp * carry[0:1]
        c_in = jnp.where(g_row == edge, carry[0:1], pltpu.roll(c_out, shift if shift == 1 else groups - 1, 0))
        for j in order:
            hj = us[j] + ps[j] * c_in
            for c in range(panels):
                h_s[c, pl.ds(j, groups, stride=SUBLANES), :] = hj[:, c * LANES:(c + 1) * LANES]
        o_ref[0] = jnp.concatenate([h_s[c] for c in range(panels)], axis=1)
        last = groups - 1 - edge
        carry[...] = jnp.broadcast_to(c_out[last:last + 1], (SUBLANES, W))

    @pl.when(d == 0)
    def _():
        scan(list(range(SUBLANES)), 1, 0)

    @pl.when(d == 1)
    def _():
        scan(list(reversed(range(SUBLANES))), -1, groups - 1)

    @pl.when(i == nblk - 1)
    def _():
        hfin_ref[0, 0] = carry[0:1]


def _lru(lru_arr, cw, cb, wg, bg, lam, h0, B, T):
    nblk = T // LRU_CHUNK
    per8 = LRU_CHUNK // SUBLANES
    n8 = B * T // SUBLANES
    blk = lambda b, d, i: b * nblk + jnp.where(d == 0, i, nblk - 1 - i)
    dirw = lambda b, d, i: (d, 0, 0)
    st = lambda b, d, i: (b, d, 0, 0)
    return pl.pallas_call(
        functools.partial(_lru_kernel, nblk=nblk),
        grid=(B, 2, nblk),
        in_specs=[
            pl.BlockSpec((LRU_CHUNK, LRU_WIDTH), lambda b, d, i: (blk(b, d, i), 0)),
            pl.BlockSpec((SUBLANES, LRU_WIDTH), lambda b, d, i: (jnp.maximum(blk(b, d, i) * per8 - 1, 0), 0)),
            pl.BlockSpec((SUBLANES, LRU_WIDTH), lambda b, d, i: (jnp.minimum((blk(b, d, i) + 1) * per8, n8 - 1), 0)),
            pl.BlockSpec((SUBLANES, LRU_WIDTH), lambda b, d, i: (0, 0)),
            pl.BlockSpec((1, LRU_WIDTH), lambda b, d, i: (0, 0)),
            pl.BlockSpec((1, LRU_WIDTH, 2 * LRU_WIDTH), dirw),
            pl.BlockSpec((1, 1, 2 * LRU_WIDTH), dirw),
            pl.BlockSpec((1, 1, LRU_WIDTH), dirw),
            pl.BlockSpec((1, 1, 1, LRU_WIDTH), st),
        ],
        out_specs=[
            pl.BlockSpec((1, LRU_CHUNK, LRU_WIDTH), lambda b, d, i: (d, blk(b, d, i), 0)),
            pl.BlockSpec((1, 1, 1, LRU_WIDTH), st),
        ],
        out_shape=[
            jax.ShapeDtypeStruct((2, B * T, LRU_WIDTH), F32),
            jax.ShapeDtypeStruct((B, 2, 1, LRU_WIDTH), F32),
        ],
        scratch_shapes=[pltpu.VMEM((SUBLANES, LRU_WIDTH), F32)]
        + [pltpu.VMEM((LRU_WIDTH // LANES, LRU_CHUNK, LANES), F32) for _ in range(3)],
        compiler_params=_cparams("parallel", "parallel", "arbitrary"),
        name="lru_scan",
    )(lru_arr, lru_arr, lru_arr, cw, cb, wg, bg, lam, h0)


def _out_kernel(h_ref, attn_ref, go_ref, gr_ref, gg_ref, lh_ref, lg_ref, w_ref, g1_ref, a2_ref, s2_ref, wr_ref, br_ref,
                vt_ref, lt_ref, hn_ref, v_ref, lgt_ref, *, n_main):
    i = pl.program_id(0)

    @pl.when(i < n_main)
    def _():
        _out_body(h_ref, attn_ref, go_ref, gr_ref, gg_ref, lh_ref, lg_ref, w_ref, g1_ref, a2_ref, s2_ref, wr_ref, br_ref,
                  hn_ref, v_ref, lgt_ref)

    @pl.when(i >= n_main)
    def _():
        v_ref[...] = vt_ref[...]
        lgt_ref[...] = lt_ref[...]


def _out_body(h_ref, attn_ref, go_ref, gr_ref, gg_ref, lh_ref, lg_ref, w_ref, g1_ref, a2_ref, s2_ref, wr_ref, br_ref,
              hn_ref, v_ref, lgt_ref):
    o = go_ref[0, 0] + go_ref[1, 0]
    gr = gr_ref[...]
    gate = gr * _sigmoid(gr)
    parts = []
    for hh in range(GLA_HEADS):
        oh = o[:, hh * GLA_DV:(hh + 1) * GLA_DV]
        y = (oh * lax.rsqrt(jnp.mean(oh * oh, axis=-1, keepdims=True) + RMS_EPS)) * gg_ref[...]
        parts.append(y * gate[:, hh * GLA_DV:(hh + 1) * GLA_DV])
    gla = jnp.concatenate(parts, axis=1).astype(BF16)
    lg = lg_ref[...]
    gelu = lg * (0.5 * (1.0 + jnp.tanh(np.sqrt(2.0 / np.pi).astype(np.float32) * (lg + 0.044715 * (lg * lg * lg)))))
    lru = ((lh_ref[0] + lh_ref[1]) * gelu).astype(BF16)
    y = jnp.dot(attn_ref[...], w_ref[0:ATTN_Q], preferred_element_type=F32)
    y = y + jnp.dot(gla, w_ref[ATTN_Q:ATTN_Q + GLA_V], preferred_element_type=F32)
    y = y + jnp.dot(lru, w_ref[ATTN_Q + GLA_V:D_MIX], preferred_element_type=F32)
    hn = h_ref[...] + g1_ref[0] * y
    hn_ref[...] = hn
    v = (hn * lax.rsqrt(jnp.mean(hn * hn, axis=-1, keepdims=True) + RMS_EPS)) * a2_ref[0] + s2_ref[0]
    _store_token_tiles(v_ref, _pack_bf16_pairs(v))
    v_hi = v.astype(BF16)
    v_lo = (v - v_hi.astype(F32)).astype(BF16)
    t = jnp.dot(v_hi, wr_ref[...], preferred_element_type=F32)
    t_lo = jnp.dot(v_lo, wr_ref[:, 0:LANES], preferred_element_type=F32)
    lgt_ref[...] = (t[:, :LANES] + t[:, LANES:]) + t_lo + br_ref[...]


def _mixer_out(h, attn, go, gla_arr, gg, lh, lru_arr, w_out, g1, a2, s2, wr, br, seq_len, tail):
    rows, D = h.shape
    tm = _tile(seq_len, 256)
    per = seq_len // tm
    n_main = rows // tm
    tw = tm * (D // 2 // LANES)
    if tail is None:
        tail = (jnp.zeros((tw, LANES), I32), jnp.zeros((tm, LANES), F32))
        n_tail = 0
    else:
        assert tail[1].shape[0] % tm == 0
        n_tail = tail[1].shape[0] // tm
    n_tok = rows + n_tail * tm
    main = lambda i: jnp.minimum(i, n_main - 1)
    row = lambda i: (main(i), 0)
    row3 = lambda i: (0, main(i), 0)
    bat = lambda i: (main(i) // per, 0, 0)
    const = lambda i: (0, 0)
    tok = lambda i: (i, 0)
    trow = lambda i: (jnp.maximum(i - n_main, 0), 0)
    in_specs = [
        pl.BlockSpec((tm, D), row),
        pl.BlockSpec((tm, ATTN_Q), row),
        pl.BlockSpec((2, 1, tm, GLA_V), lambda i: (0, main(i) // per, main(i) % per, 0)),
        pl.BlockSpec((tm, GLA_V), lambda i: (main(i), 2)),
        pl.BlockSpec((1, GLA_DV), const),
        pl.BlockSpec((2, tm, LRU_WIDTH), row3),
        pl.BlockSpec((tm, LRU_WIDTH), lambda i: (main(i), 1)),
        pl.BlockSpec((D_MIX, D), const, pipeline_mode=pl.Buffered(1)),
        pl.BlockSpec((1, 1, D), bat),
        pl.BlockSpec((1, 1, D), bat),
        pl.BlockSpec((1, 1, D), bat),
        pl.BlockSpec((D, 2 * LANES), const),
        pl.BlockSpec((1, LANES), const),
        pl.BlockSpec((tw, LANES), trow),
        pl.BlockSpec((tm, LANES), trow),
    ]
    args = [h, attn, go, gla_arr, gg, lh, lru_arr, w_out, g1, a2, s2, wr, br, tail[0], tail[1]]
    return pl.pallas_call(
        functools.partial(_out_kernel, n_main=n_main),
        grid=(n_main + n_tail,),
        in_specs=in_specs,
        out_specs=[
            pl.BlockSpec((tm, D), row),
            pl.BlockSpec((tw, LANES), tok),
            pl.BlockSpec((tm, LANES), tok),
        ],
        out_shape=[
            jax.ShapeDtypeStruct((rows, D), F32),
            jax.ShapeDtypeStruct((n_tok * (tw // tm), LANES), I32),
            jax.ShapeDtypeStruct((n_tok, LANES), F32),
        ],
        compiler_params=_cparams("arbitrary"),
        name="mixer_out",
    )(*args)


def _route_kernel(lg_ref, tri_ref, oi_ref, of_ref, cnt_ref, carry):
    i = pl.program_id(0)

    @pl.when(i == 0)
    def _():
        carry[...] = jnp.zeros_like(carry)

    lg = lg_ref[...]
    col = lax.broadcasted_iota(I32, lg.shape, 1)
    colf = col.astype(F32)
    big = float(LANES)
    is_g = col < N_GROUPS
    gm = jnp.max(jnp.where(is_g, lg, -jnp.inf), axis=-1, keepdims=True)
    eg = jnp.where(is_g, jnp.exp(lg - gm), 0.0)
    pg = eg / jnp.sum(eg, axis=-1, keepdims=True)
    p_grp = jnp.max(pg, axis=-1, keepdims=True)
    grp = jnp.min(jnp.where(is_g & (pg == p_grp), colf, big), axis=-1, keepdims=True).astype(I32)

    sel = (col >= N_GROUPS) & (col < N_GROUPS + N_EXPERTS) & (((col - N_GROUPS) >> 3) == grp)
    em = jnp.max(jnp.where(sel, lg, -jnp.inf), axis=-1, keepdims=True)
    ee = jnp.where(sel, jnp.exp(lg - em), 0.0)
    pe = ee / jnp.sum(ee, axis=-1, keepdims=True)
    p1 = jnp.max(jnp.where(sel, pe, -1.0), axis=-1, keepdims=True)
    c1 = jnp.min(jnp.where(sel & (pe == p1), colf, big), axis=-1, keepdims=True).astype(I32)
    rest = sel & (col != c1)
    p2 = jnp.max(jnp.where(rest, pe, -1.0), axis=-1, keepdims=True)
    c2 = jnp.min(jnp.where(rest & (pe == p2), colf, big), axis=-1, keepdims=True).astype(I32)
    e1 = c1 - N_GROUPS
    e2 = c2 - N_GROUPS
    den = p1 + p2
    g1 = p_grp * (p1 / den)
    g2 = p_grp * (p2 / den)

    hit1 = col == e1
    hit2 = col == e2
    oh = jnp.where(hit1 | hit2, 1.0, 0.0)
    before = jnp.dot(tri_ref[...], oh.astype(BF16), preferred_element_type=F32) + carry[0:1]
    r1 = jnp.sum(jnp.where(hit1, before, 0.0), axis=-1, keepdims=True)
    r2 = jnp.sum(jnp.where(hit2, before, 0.0), axis=-1, keepdims=True)
    new = carry[0:1] + jnp.sum(oh, axis=0, keepdims=True)
    carry[...] = jnp.broadcast_to(new, carry.shape)
    cnt_ref[...] = jnp.broadcast_to(new, cnt_ref.shape)

    packed = jnp.where(col == 0, e1.astype(F32), jnp.where(col == 1, e2.astype(F32),
                                                           jnp.where(col == 2, r1, jnp.where(col == 3, r2, 0.0))))
    oi_ref[...] = packed.T[0:SUBLANES].astype(I32)
    of_ref[...] = jnp.where(col == 0, g1, jnp.where(col == 1, g2, 0.0))


def _route(logits):
    n_tok = logits.shape[0]
    tm = _tile(n_tok, 1024)
    row = lambda i: (i, 0)
    tri = jnp.tril(jnp.ones((tm, tm), BF16), -1)
    return pl.pallas_call(
        _route_kernel,
        grid=(n_tok // tm,),
        in_specs=[pl.BlockSpec((tm, LANES), row), pl.BlockSpec((tm, tm), lambda i: (0, 0))],
        out_specs=[
            pl.BlockSpec((SUBLANES, tm), lambda i: (0, i)),
            pl.BlockSpec((tm, LANES), row),
            pl.BlockSpec((SUBLANES, LANES), lambda i: (0, 0)),
        ],
        out_shape=[
            jax.ShapeDtypeStruct((SUBLANES, n_tok), I32),
            jax.ShapeDtypeStruct((n_tok, LANES), F32),
            jax.ShapeDtypeStruct((SUBLANES, LANES), F32),
        ],
        scratch_shapes=[pltpu.VMEM((SUBLANES, LANES), F32)],
        compiler_params=_cparams("arbitrary"),
        name="route",
    )(logits, tri)


def _dot_casting(x, stage, dst, rows, cols):
    n = rows.stop - rows.start
    step = _tile(n, 256)
    acc = None
    for c in range(n // step):
        r = slice(rows.start + c * step, rows.start + (c + 1) * step)
        w = stage[r, cols].astype(BF16)
        dst[r, cols] = w
        part = jnp.dot(x[:, c * step:(c + 1) * step], w, preferred_element_type=F32)
        acc = part if acc is None else acc + part
    return acc


def _moe_ffn_kernel(be_ref, nu_ref, nx_ref, src_ref, srcn_ref, v_hbm, w1_hbm, w3_hbm, w2_hbm, o_ref,
                    xbuf, st1, st3, st2, w1s, w3s, w2s, sem_x, sem_w, *, layer, sub):
    i = pl.program_id(0)
    nu = nu_ref[0]
    rows = ROW_BLOCK * sub

    def gather(idx_ref, slot):
        for r in range(ROW_BLOCK):
            s = pl.multiple_of(idx_ref[0, 0, r] * sub, sub)
            pltpu.make_async_copy(v_hbm.at[pl.ds(s, sub)], xbuf.at[slot, pl.ds(r * sub, sub)], sem_x.at[slot]).start()

    def wait_rows(slot):
        pltpu.make_async_copy(v_hbm.at[pl.ds(0, rows)], xbuf.at[slot], sem_x.at[slot]).wait()

    def weight_copies(e):
        return (pltpu.make_async_copy(w1_hbm.at[layer, e], st1, sem_w.at[0]),
                pltpu.make_async_copy(w3_hbm.at[layer, e], st3, sem_w.at[1]),
                pltpu.make_async_copy(w2_hbm.at[layer, e], st2, sem_w.at[2]))

    @pl.when(i < nu)
    def _():
        e = be_ref[i]
        slot = i & 1

        @pl.when(i == 0)
        def _():
            for cp in weight_copies(e):
                cp.start(priority=1)
            gather(src_ref, 0)

        wait_rows(slot)
        gather(srcn_ref, 1 - slot)
        xb = _unpack_bf16_pairs(_load_token_tiles(xbuf.at[slot], ROW_BLOCK)).astype(BF16)
        first = (i == 0) | (e != be_ref[jnp.maximum(i - 1, 0)])

        d_model, d_ff = w1s.shape
        n_ff = 1
        ffs = [slice(c * d_ff // n_ff, (c + 1) * d_ff // n_ff) for c in range(n_ff)]
        every = slice(0, d_model)

        def mlp(up1, up3, down):
            y = None
            for ff in ffs:
                a = up1(xb, ff)
                b = up3(xb, ff)
                part = down(((a * _sigmoid(a)) * b).astype(BF16), ff)
                y = part if y is None else y + part
            _store_token_tiles(o_ref, _pack_bf16_pairs(y))

        @pl.when(first)
        def _():
            for cp in weight_copies(e):
                cp.wait()
            mlp(lambda x, ff: _dot_casting(x, st1, w1s, every, ff), lambda x, ff: _dot_casting(x, st3, w3s, every, ff),
                lambda h, ff: _dot_casting(h, st2, w2s, ff, every))

            @pl.when(nx_ref[i] >= 0)
            def _():
                for cp in weight_copies(nx_ref[i]):
                    cp.start(priority=1)

        @pl.when(jnp.logical_not(first))
        def _():
            mlp(lambda x, ff: jnp.dot(x, w1s[:, ff], preferred_element_type=F32),
                lambda x, ff: jnp.dot(x, w3s[:, ff], preferred_element_type=F32),
                lambda h, ff: jnp.dot(h, w2s[ff, :], preferred_element_type=F32))

        @pl.when(i == nu - 1)
        def _():
            wait_rows(1 - slot)

    @pl.when(i >= nu)
    def _():
        o_ref[...] = jnp.zeros_like(o_ref)


def _moe_experts(v_tiles, src, blk_e, n_used, nxt_e, w1, w3, w2, layer):
    D, FF = w1.shape[-2:]
    sub = D // 2 // LANES
    P = src.shape[0]
    nb = P // ROW_BLOCK
    tw = ROW_BLOCK * sub
    cur = lambda i, be, nu, nx: (jnp.minimum(i, nu[0] - 1), 0, 0)
    nxt = lambda i, be, nu, nx: (jnp.minimum(i + 1, nu[0] - 1), 0, 0)
    src3 = src.reshape(nb, 1, ROW_BLOCK)
    hbm = pl.BlockSpec(memory_space=pl.ANY)
    return pl.pallas_call(
        functools.partial(_moe_ffn_kernel, layer=layer, sub=sub),
        grid_spec=pltpu.PrefetchScalarGridSpec(
            num_scalar_prefetch=3,
            grid=(nb,),
            in_specs=[
                pl.BlockSpec((1, 1, ROW_BLOCK), cur, memory_space=pltpu.SMEM),
                pl.BlockSpec((1, 1, ROW_BLOCK), nxt, memory_space=pltpu.SMEM),
                hbm, hbm, hbm, hbm,
            ],
            out_specs=pl.BlockSpec((tw, LANES), lambda i, be, nu, nx: (i, 0)),
            scratch_shapes=[
                pltpu.VMEM((2, tw, LANES), I32),
                pltpu.VMEM((D, FF), F32), pltpu.VMEM((D, FF), F32), pltpu.VMEM((FF, D), F32),
                pltpu.VMEM((D, FF), BF16), pltpu.VMEM((D, FF), BF16), pltpu.VMEM((FF, D), BF16),
                pltpu.SemaphoreType.DMA((2,)), pltpu.SemaphoreType.DMA((3,)),
            ],
        ),
        out_shape=jax.ShapeDtypeStruct((P * sub, LANES), I32),
        compiler_params=_cparams("arbitrary"),
        name="moe_ffn",
    )(blk_e, n_used, nxt_e, src3, src3, v_tiles, w1, w3, w2)


def _combine_kernel(dest_ref, destn_ref, h_ref, gate_ref, g2_ref, fg_ref, ys_ref, o_ref, buf, sem, *, final, nsteps):
    i = pl.program_id(0)
    slot = i & 1
    tm = h_ref.shape[0]
    rows = buf.shape[2]
    sub = rows // tm

    def gather(idx_ref, sl):
        for j in range(tm * TOP_K):
            src = pl.multiple_of(idx_ref[j % TOP_K, 0, 0, j // TOP_K] * sub, sub)
            cp = pltpu.make_async_copy(ys_ref.at[pl.ds(src, sub)], buf.at[sl, j % TOP_K, pl.ds((j // TOP_K) * sub, sub)],
                                       sem.at[sl])
            cp.start(priority=j % 2)

    def wait_rows(sl):
        for kk in range(TOP_K):
            pltpu.make_async_copy(ys_ref.at[pl.ds(0, rows)], buf.at[sl, kk], sem.at[sl]).wait()

    @pl.when(i == 0)
    def _():
        gather(dest_ref, 0)

    gather(destn_ref, 1 - slot)
    wait_rows(slot)
    gate = gate_ref[...]
    f = (_unpack_bf16_pairs(_load_token_tiles(buf.at[slot, 0], tm)) * gate[:, 0:1]
         + _unpack_bf16_pairs(_load_token_tiles(buf.at[slot, 1], tm)) * gate[:, 1:2])

    @pl.when(i == nsteps - 1)
    def _():
        wait_rows(1 - slot)

    hn = h_ref[...] + g2_ref[0] * f
    if final:
        hn = (hn * lax.rsqrt(jnp.mean(hn * hn, axis=-1, keepdims=True) + RMS_EPS)) * fg_ref[...]
    o_ref[...] = hn


def _combine(h, dest, gates, g2, fg, ys, seq_len, tok_off, final):
    rows, D = h.shape
    tm = _tile(seq_len, 256)
    per = seq_len // tm
    off = tok_off // tm
    nsteps = rows // tm
    dest3 = dest.reshape(TOP_K, dest.shape[1] // tm, 1, tm)
    return pl.pallas_call(
        functools.partial(_combine_kernel, final=final, nsteps=nsteps),
        grid=(nsteps,),
        in_specs=[
            pl.BlockSpec((TOP_K, 1, 1, tm), lambda i: (0, i + off, 0, 0), memory_space=pltpu.SMEM),
            pl.BlockSpec((TOP_K, 1, 1, tm), lambda i: (0, jnp.minimum(i + 1, nsteps - 1) + off, 0, 0),
                         memory_space=pltpu.SMEM),
            pl.BlockSpec((tm, D), lambda i: (i, 0)),
            pl.BlockSpec((tm, LANES), lambda i: (i + off, 0)),
            pl.BlockSpec((1, 1, D), lambda i: (i // per, 0, 0)),
            pl.BlockSpec((1, D), lambda i: (0, 0)),
            pl.BlockSpec(memory_space=pl.ANY),
        ],
        out_specs=pl.BlockSpec((tm, D), lambda i: (i, 0)),
        out_shape=jax.ShapeDtypeStruct((rows, D), F32),
        scratch_shapes=[pltpu.VMEM((2, TOP_K, tm * (D // 2 // LANES), LANES), I32), pltpu.SemaphoreType.DMA((2,))],
        compiler_params=_cparams("arbitrary"),
        name="combine",
    )(dest3, dest3, h, gates, g2, fg, ys)


def _rope_tables(S):
    rows = S // GRID_W
    row = jnp.repeat(jnp.arange(rows, dtype=F32), GRID_W)
    col = jnp.tile(jnp.arange(GRID_W, dtype=F32), rows)
    n_freq = HEAD_DIM // 4
    inv = ROPE_THETA ** (-jnp.arange(n_freq, dtype=F32) / n_freq)
    ang = jnp.concatenate([row[:, None] * inv, col[:, None] * inv], axis=-1)
    cos = jnp.repeat(jnp.cos(ang), 2, axis=1)
    sin = jnp.repeat(jnp.sin(ang), 2, axis=1)
    even = (jnp.arange(HEAD_DIM) % 2) == 0
    return cos, jnp.where(even, -sin, 0.0), jnp.where(even, 0.0, sin)


def _block_diag(w):
    n, c, _ = w.shape
    eye = jnp.eye(n, dtype=w.dtype)
    return (eye[:, None, :, None] * w[:, :, None, :]).reshape(n * c, n * c)


def _slot_plan(ids, counts_row, n_tok):
    e = ids[0:TOP_K]
    rank = ids[TOP_K:2 * TOP_K]
    counts = counts_row[:N_EXPERTS].astype(I32)
    padded = (counts + ROW_BLOCK - 1) // ROW_BLOCK * ROW_BLOCK
    pad_end = jnp.cumsum(padded)
    pad_start = pad_end - padded
    hit = e[None] == jnp.arange(N_EXPERTS, dtype=I32)[:, None, None]
    dest = jnp.sum(jnp.where(hit, pad_start[:, None, None], 0), axis=0) + rank
    n_blocks = -(-(n_tok * TOP_K + N_EXPERTS * (ROW_BLOCK - 1)) // ROW_BLOCK)
    starts = jnp.arange(n_blocks, dtype=I32) * ROW_BLOCK
    blk_e = jnp.minimum(jnp.sum(pad_end[None, :] <= starts[:, None], axis=1), N_EXPERTS - 1).astype(I32)
    n_used = (pad_end[-1] // ROW_BLOCK).astype(I32).reshape(1)
    run_next = pad_end[blk_e] // ROW_BLOCK
    nxt_e = jnp.where(run_next < n_used[0], blk_e[jnp.minimum(run_next, n_blocks - 1)], -1).astype(I32)
    tok = jnp.broadcast_to(jnp.arange(n_tok, dtype=I32), (TOP_K, n_tok))
    src = jnp.zeros((n_blocks * ROW_BLOCK,), I32).at[dest.reshape(-1)].set(tok.reshape(-1), unique_indices=True)
    return dest, src, blk_e, n_used, nxt_e


def kernel(x, c, ctx, c_ctx, ada_w, ada_b, norm_mix_g, norm_ffn_g, w_in, attn_sink, gla_gate_w2, gla_gate_b, gla_norm_g, lru_conv_w, lru_conv_b, lru_wa, lru_ba, lru_wx, lru_bx, lru_lambda, w_out, router_g_w, router_g_b, router_e_w, router_e_b, moe_w1, moe_w3, moe_w2, final_norm_g):
    B, S, D = x.shape
    C = ctx.shape[1]
    L = ada_w.shape[0]
    assert S % max(ATTN_BLOCK, GLA_CHUNK, LRU_CHUNK, GRID_W) == 0 and C % max(ATTN_BLOCK, GLA_CHUNK, LRU_CHUNK) == 0
    assert B + 1 <= SUBLANES and D % (2 * LANES) == 0

    cond = jnp.concatenate([c, c_ctx[None], jnp.zeros((SUBLANES - B - 1, D), F32)], axis=0)
    mods = _ada_mods(cond, ada_w, ada_b).reshape(L, SUBLANES, 6, D)

    cos_l, se_l, so_l = _rope_tables(S)
    cos_c = jnp.ones((C, HEAD_DIM), F32)
    zero_c = jnp.zeros((C, HEAD_DIM), F32)
    e2_np, m2_np = _gla_constants()
    gla_e = jnp.asarray(e2_np, BF16)
    gla_m = jnp.asarray(m2_np, F32)

    h_lat = x.reshape(B * S, D)
    h_ctx = ctx.reshape(B * C, D)
    out = None
    for l in range(L):
        last = l == L - 1
        ml = mods[l, :B]
        mc = jnp.broadcast_to(mods[l, B][None], (B, 6, D))
        per_b = lambda m, j: m[:, j][:, None, :]
        a1_l, a1_c = [(1.0 + per_b(m, 1)) * norm_mix_g[l] for m in (ml, mc)]
        a2_l, a2_c = [(1.0 + per_b(m, 4)) * norm_ffn_g[l] for m in (ml, mc)]

        w = w_in[l]
        c_gz = ATTN_Q + 2 * ATTN_KV + 2 * GLA_QK + 2 * GLA_V
        w_packed = jnp.concatenate(
            [w[:, :c_gz], w[:, c_gz:c_gz + 2 * GLA_RANK], jnp.zeros((D, GZ_W - 2 * GLA_RANK), F32), w[:, c_gz + 2 * GLA_RANK:]],
            axis=1).astype(BF16)
        qkv_l, gla_l, gz_l, lru_l = _in_proj(h_lat, a1_l, per_b(ml, 0), cos_l, se_l, so_l, w_packed, S)
        qkv_c, gla_c, gz_c, lru_c = _in_proj(h_ctx, a1_c, per_b(mc, 0), cos_c, zero_c, zero_c, w_packed, C)

        sink_b = jnp.broadcast_to(attn_sink[l][:, None], (N_Q_HEADS, LANES)).astype(F32)
        attn_l = _attn_latent(qkv_l, qkv_c, sink_b, B, S, C)

        w2p = jnp.zeros((2, GZ_W, GLA_QK), F32)
        w2p = w2p.at[0, :GLA_RANK].set(gla_gate_w2[l, 0]).at[1, GLA_RANK:2 * GLA_RANK].set(gla_gate_w2[l, 1]).astype(BF16)
        gbias = gla_gate_b[l].reshape(2, 1, GLA_QK)
        s_zero = jnp.zeros((B, 2, GLA_QK, GLA_DV), F32)
        go_c, s_ctx = _gla(gla_c, gz_c, w2p, gbias, gla_e, gla_m, s_zero, B, C)
        go_l, _ = _gla(gla_l, gz_l, w2p, gbias, gla_e, gla_m, s_ctx, B, S)

        cw = jnp.concatenate([lru_conv_w[l], jnp.zeros((SUBLANES - CONV_W, LRU_WIDTH), F32)], axis=0)
        cb = lru_conv_b[l].reshape(1, LRU_WIDTH)
        wg = jnp.stack([jnp.concatenate([_block_diag(lru_wa[l, d]), _block_diag(lru_wx[l, d])], axis=1) for d in range(2)]).astype(BF16)
        bg = jnp.concatenate([lru_ba[l], lru_bx[l]], axis=1).reshape(2, 1, 2 * LRU_WIDTH)
        lam = lru_lambda[l].reshape(2, 1, LRU_WIDTH)
        h_zero = jnp.zeros((B, 2, 1, LRU_WIDTH), F32)
        lh_c, hs_ctx = _lru(lru_c, cw, cb, wg, bg, lam, h_zero, B, C)
        lh_l, _ = _lru(lru_l, cw, cb, wg, bg, lam, hs_ctx, B, S)

        wo = w_out[l].astype(BF16)
        gg = gla_norm_g[l].reshape(1, GLA_DV)
        wr = jnp.concatenate([router_g_w[l], router_e_w[l], jnp.zeros((D, LANES - N_GROUPS - N_EXPERTS), F32)], axis=1)
        wr_hi = wr.astype(BF16)
        wr = jnp.concatenate([wr_hi, (wr - wr_hi.astype(F32)).astype(BF16)], axis=1)
        br =jnp.concatenate([router_g_b[l], router_e_b[l], jnp.zeros((LANES - N_GROUPS - N_EXPERTS,), F32)]).reshape(1, LANES)
        n_tok = B * S if last else B * (S + C)
        tail = None
        if not last:
            attn_c = _attn_context(qkv_c, sink_b, B, C)
            h_ctx, v_c, lgt_c = _mixer_out(h_ctx, attn_c, go_c, gla_c, gg, lh_c, lru_c, wo, per_b(mc, 2), a2_c, per_b(mc, 3),
                                           wr, br, C, None)
            tail = (v_c, lgt_c)
        h_lat, v_all, lgt_all = _mixer_out(h_lat, attn_l, go_l, gla_l, gg, lh_l, lru_l, wo, per_b(ml, 2), a2_l, per_b(ml, 3),
                                           wr, br, S, tail)

        ids, gates, counts = _route(lgt_all)
        dest, src, blk_e, n_used, nxt_e = _slot_plan(ids, counts[0], n_tok)
        ys = _moe_experts(v_all, src, blk_e, n_used, nxt_e, moe_w1, moe_w3, moe_w2, l)
        fg = final_norm_g.reshape(1, D)
        h_lat = _combine(h_lat, dest, gates, per_b(ml, 5), fg, ys, S, 0, last)
        if not last:
            h_ctx = _combine(h_ctx, dest, gates, per_b(mc, 5), fg, ys, C, B * S, False)
        out = h_lat
    return out.reshape(B, S, D)
```
